```python
import jax, jax.numpy as jnp
from jax import lax
import numpy as np

D_MODEL = 1024
BATCH = 8
SEQ = 2048
DEPTH = 2

CTX_LEN = 256
GRID_W = 64
NORM_EPS = 1e-6

N_HEADS = 8
N_KV_HEADS = 2
HEAD_DIM = 64
Q_GROUP = N_HEADS // N_KV_HEADS
Q_WIDTH = N_HEADS * HEAD_DIM
KV_WIDTH = N_KV_HEADS * HEAD_DIM
ROPE_THETA = 10000.0
AXIS_PAIRS = HEAD_DIM // 4
Q_BLOCK = 128

SC_WIDTH = 256
FOURIER_GROUPS = 4
FOURIER_WIDTH = 256
FOURIER_GROUP = FOURIER_WIDTH // FOURIER_GROUPS
POOL_WINDOWS = (2, 4, 8, 16)
POOL_WIDTH = 256
POOL_GROUP = POOL_WIDTH // len(POOL_WINDOWS)

N_BRANCHES = 4
D_FF = 2816

OFF_Q = 0
OFF_K = OFF_Q + Q_WIDTH
OFF_V = OFF_K + KV_WIDTH
OFF_SC = OFF_V + KV_WIDTH
OFF_F = OFF_SC + 3 * SC_WIDTH
OFF_P = OFF_F + FOURIER_WIDTH
OFF_G = OFF_P + POOL_WIDTH
IN_WIDTH = OFF_G + N_BRANCHES * D_MODEL

kernel_name = "hybrid_parallel_gated_diffusion_block"


def rmsnorm(x, g):
    xf = x.astype(jnp.float32)
    y = xf * lax.rsqrt(jnp.mean(xf * xf, axis=-1, keepdims=True) + NORM_EPS)
    return (y * g.astype(jnp.float32)).astype(x.dtype)


def adaln(cvec, w_mod, b_mod):
    m = jax.nn.silu(cvec) @ w_mod + b_mod
    return m.reshape(cvec.shape[0], 6, D_MODEL)


def modulate(h, shift, scale):
    return h * (1.0 + scale[:, None, :]) + shift[:, None, :]


def dwconv3(x, w):
    xp = jnp.pad(x, ((0, 0), (1, 1), (0, 0)))
    return xp[:, :-2] * w[0] + xp[:, 1:-1] * w[1] + xp[:, 2:] * w[2]


def axial_rope(n):
    rows = n // GRID_W
    row = jnp.repeat(jnp.arange(rows), GRID_W).astype(jnp.float32)
    col = jnp.tile(jnp.arange(GRID_W), rows).astype(jnp.float32)
    inv = ROPE_THETA ** (-jnp.arange(AXIS_PAIRS, dtype=jnp.float32) / AXIS_PAIRS)
    ang = jnp.stack([row[:, None] * inv, col[:, None] * inv], axis=1)
    ang = ang[None, :, None, :, None, :]
    return jnp.cos(ang), jnp.sin(ang)


def apply_rope(x, cos, sin):
    b, n, h, _ = x.shape
    xr = x.astype(jnp.float32).reshape(b, n, h, 2, 2, AXIS_PAIRS)
    rot = jnp.concatenate([-xr[..., 1:2, :], xr[..., 0:1, :]], axis=-2)
    return (xr * cos + rot * sin).reshape(b, n, h, HEAD_DIM).astype(x.dtype)


def kv_heads(z_kv, k_gain):
    b, n, _ = z_kv.shape
    k = rmsnorm(z_kv[..., :KV_WIDTH].reshape(b, n, N_KV_HEADS, HEAD_DIM), k_gain)
    v = z_kv[..., KV_WIDTH:].reshape(b, n, N_KV_HEADS, HEAD_DIM)
    return k, v


def attend_blocked(q, k, v):
    b, n, _, _ = q.shape
    nb = n // Q_BLOCK
    qb = q.reshape(b, nb, Q_BLOCK, N_KV_HEADS, Q_GROUP, HEAD_DIM).transpose(1, 0, 2, 3, 4, 5)
    scale = HEAD_DIM ** -0.5

    def one_block(qq):
        s = jnp.einsum("bqkgd,bskd->bkgqs", qq, k).astype(jnp.float32) * scale
        p = jax.nn.softmax(s, axis=-1).astype(v.dtype)
        return jnp.einsum("bkgqs,bskd->bqkgd", p, v)

    o = lax.map(one_block, qb)
    return o.transpose(1, 0, 2, 3, 4, 5).reshape(b, n, Q_WIDTH)


def pool_minus_identity(x):
    _, n, _ = x.shape
    xf = x.astype(jnp.float32)
    cs = jnp.pad(jnp.cumsum(xf, axis=1), ((0, 0), (1, 0), (0, 0)))
    t = jnp.arange(n)
    outs = []
    for gi, w in enumerate(POOL_WINDOWS):
        left, right = (w - 1) // 2, w // 2
        lo = jnp.maximum(t - left, 0)
        hi = jnp.minimum(t + right + 1, n)
        sl = slice(gi * POOL_GROUP, (gi + 1) * POOL_GROUP)
        csg = cs[..., sl]
        cnt = (hi - lo).astype(jnp.float32)[None, :, None]
        mean = (jnp.take(csg, hi, axis=1) - jnp.take(csg, lo, axis=1)) / cnt
        outs.append(mean - xf[..., sl])
    return jnp.concatenate(outs, axis=-1).astype(x.dtype)


def token_mixers(z, k, v, q_gain, conv_sc, pool_mat, pool_scale,
                 w_br_attn, w_br_sc, w_br_f, w_br_p, w_out, rope, ctx_kv):
    b, n, _ = z.shape
    q = rmsnorm(z[..., OFF_Q:OFF_K].reshape(b, n, N_HEADS, HEAD_DIM), q_gain)
    if rope is not None:
        cos, sin = rope
        q = apply_rope(q, cos, sin)
        k = apply_rope(k, cos, sin)
    if ctx_kv is not None:
        k = jnp.concatenate([k, ctx_kv[0]], axis=1)
        v = jnp.concatenate([v, ctx_kv[1]], axis=1)
    o_attn = attend_blocked(q, k, v) @ w_br_attn
    zs = z[..., OFF_SC:OFF_F]
    gb, gc, xs = zs[..., :SC_WIDTH], zs[..., SC_WIDTH:2 * SC_WIDTH], zs[..., 2 * SC_WIDTH:]
    o_sc = (gb * dwconv3(gc * xs, conv_sc)) @ w_br_sc
    xg = z[..., OFF_F:OFF_P].astype(jnp.float32).reshape(b, n, FOURIER_GROUPS, FOURIER_GROUP)
    xfour = jnp.fft.fft2(xg, axes=(1, 3), norm="ortho").real.astype(z.dtype).reshape(b, n, FOURIER_WIDTH)
    o_f = xfour @ w_br_f
    pooled = pool_minus_identity(z[..., OFF_P:OFF_G]).reshape(b, n, len(POOL_WINDOWS), POOL_GROUP)
    yp = jnp.einsum("bngc,gcd->bngd", pooled, pool_mat).reshape(b, n, POOL_WIDTH) * pool_scale
    o_p = yp @ w_br_p
    g = jax.nn.sigmoid(z[..., OFF_G:].reshape(b, n, N_BRANCHES, D_MODEL))
    y = g[:, :, 0] * o_attn + g[:, :, 1] * o_sc + g[:, :, 2] * o_f + g[:, :, 3] * o_p
    return y @ w_out


def conv_ffn(h, w_up, conv_w, w_down):
    u = dwconv3(h @ w_up, conv_w)
    a, bval = u[..., :D_FF], u[..., D_FF:]
    return (jax.nn.silu(a) * bval) @ w_down


def setup_inputs(seed: int = 0) -> dict:
    key = jax.random.key(seed)
    ks = jax.random.split(key, 24)
    f32 = jnp.float32

    def nrm(k, shape, scale):
        return jax.random.normal(k, shape, f32) * scale

    L, D = DEPTH, D_MODEL
    return {
        "x": nrm(ks[0], (BATCH, SEQ, D), 1.0),
        "c": nrm(ks[1], (BATCH, D), 1.0),
        "ctx": nrm(ks[2], (BATCH, CTX_LEN, D), 1.0),
        "c_ctx": nrm(ks[3], (D,), 1.0),
        "w_mod": nrm(ks[4], (L, D, 6 * D), 0.5 * D ** -0.5),
        "b_mod": nrm(ks[5], (L, 6 * D), 0.02),
        "norm1": 1.0 + nrm(ks[6], (L, D), 0.02),
        "norm2": 1.0 + nrm(ks[7], (L, D), 0.02),
        "w_in": nrm(ks[8], (L, D, IN_WIDTH), D ** -0.5),
        "conv_sc": nrm(ks[9], (L, 3, SC_WIDTH), 3 ** -0.5),
        "qk_gain": 1.0 + nrm(ks[10], (L, 2, HEAD_DIM), 0.02),
        "pool_mat": nrm(ks[11], (L, len(POOL_WINDOWS), POOL_GROUP, POOL_GROUP), POOL_GROUP ** -0.5),
        "pool_scale": 1.0 + nrm(ks[12], (L, POOL_WIDTH), 0.02),
        "w_br_attn": nrm(ks[13], (L, Q_WIDTH, D), Q_WIDTH ** -0.5),
        "w_br_sc": nrm(ks[14], (L, SC_WIDTH, D), SC_WIDTH ** -0.5),
        "w_br_f": nrm(ks[15], (L, FOURIER_WIDTH, D), FOURIER_WIDTH ** -0.5),
        "w_br_p": nrm(ks[16], (L, POOL_WIDTH, D), POOL_WIDTH ** -0.5),
        "w_out": nrm(ks[17], (L, D, D), D ** -0.5),
        "w_up": nrm(ks[18], (L, D, 2 * D_FF), D ** -0.5),
        "w_conv_ffn": nrm(ks[19], (L, 3, 2 * D_FF), 3 ** -0.5),
        "w_down": nrm(ks[20], (L, D_FF, D), D_FF ** -0.5),
        "final_norm": 1.0 + nrm(ks[21], (D,), 0.02),
    }


def reference(x, c, ctx, c_ctx, w_mod, b_mod, norm1, norm2, w_in, conv_sc, qk_gain, pool_mat,
              pool_scale, w_br_attn, w_br_sc, w_br_f, w_br_p, w_out, w_up, w_conv_ffn, w_down, final_norm):
    rope = axial_rope(x.shape[1])
    lat, cx = x, ctx
    for l in range(DEPTH):
        m_lat = adaln(c, w_mod[l], b_mod[l])
        m_ctx = adaln(c_ctx[None], w_mod[l], b_mod[l])
        q_gain, k_gain = qk_gain[l, 0], qk_gain[l, 1]
        mixer_params = (q_gain, conv_sc[l], pool_mat[l], pool_scale[l],
                        w_br_attn[l], w_br_sc[l], w_br_f[l], w_br_p[l], w_out[l])
        last = l == DEPTH - 1

        hc = modulate(rmsnorm(cx, norm1[l]), m_ctx[:, 0], m_ctx[:, 1])
        if last:
            kc, vc = kv_heads(hc @ w_in[l][:, OFF_K:OFF_SC], k_gain)
        else:
            zc = hc @ w_in[l]
            kc, vc = kv_heads(zc[..., OFF_K:OFF_SC], k_gain)

        hl = modulate(rmsnorm(lat, norm1[l]), m_lat[:, 0], m_lat[:, 1])
        zl = hl @ w_in[l]
        kl, vl = kv_heads(zl[..., OFF_K:OFF_SC], k_gain)
        lat = lat + m_lat[:, 2][:, None, :] * token_mixers(zl, kl, vl, *mixer_params, rope, (kc, vc))
        hf = modulate(rmsnorm(lat, norm2[l]), m_lat[:, 3], m_lat[:, 4])
        lat = lat + m_lat[:, 5][:, None, :] * conv_ffn(hf, w_up[l], w_conv_ffn[l], w_down[l])

        if not last:
            cx = cx + m_ctx[:, 2][:, None, :] * token_mixers(zc, kc, vc, *mixer_params, None, None)
            hfc = modulate(rmsnorm(cx, norm2[l]), m_ctx[:, 3], m_ctx[:, 4])
            cx = cx + m_ctx[:, 5][:, None, :] * conv_ffn(hfc, w_up[l], w_conv_ffn[l], w_down[l])
    return rmsnorm(lat, final_norm)
```

```python
import functools
import math

import jax
import jax.numpy as jnp
from jax import lax
from jax.experimental import pallas as pl
from jax.experimental.pallas import tpu as pltpu

D_MODEL = 1024
DEPTH = 2
GRID_W = 64
NORM_EPS = 1e-6
N_HEADS = 8
N_KV_HEADS = 2
HEAD_DIM = 64
Q_WIDTH = N_HEADS * HEAD_DIM
KV_WIDTH = N_KV_HEADS * HEAD_DIM
QK_WIDTH = Q_WIDTH + KV_WIDTH
ROPE_THETA = 10000.0
AXIS_PAIRS = HEAD_DIM // 4
SC_WIDTH = 256
FOURIER_GROUPS = 4
FOURIER_WIDTH = 256
FOURIER_GROUP = FOURIER_WIDTH // FOURIER_GROUPS
POOL_WINDOWS = (2, 4, 8, 16)
POOL_WIDTH = 256
POOL_GROUP = POOL_WIDTH // len(POOL_WINDOWS)
N_BRANCHES = 4
D_FF = 2816

OFF_K = Q_WIDTH
OFF_V = OFF_K + KV_WIDTH
OFF_SC = OFF_V + KV_WIDTH
OFF_F = OFF_SC + 3 * SC_WIDTH
OFF_P = OFF_F + FOURIER_WIDTH
OFF_G = OFF_P + POOL_WIDTH

LANES = 128
HALO = 8
FF_CHUNK = 256
N_FF_CHUNKS = D_FF // FF_CHUNK
MOD_ROWS = 8
VMEM_LIMIT = 56 * 1024 * 1024

F32 = jnp.float32
BF16 = jnp.bfloat16


def _dot(a, b):
    return jnp.dot(a, b, preferred_element_type=F32)


def _const_spec(shape):
    nd = len(shape)
    return pl.BlockSpec(shape, lambda *_: (0,) * nd, pipeline_mode=pl.Buffered(1))


def _params():
    return pltpu.CompilerParams(dimension_semantics=("arbitrary", "arbitrary"),
                                vmem_limit_bytes=VMEM_LIMIT)


def _rms_mod(xv, norm, shift, scale):
    ms = jnp.mean(xv * xv, axis=-1, keepdims=True)
    y = xv * lax.rsqrt(ms + NORM_EPS) * norm
    return y * (1.0 + scale) + shift


def _halo_specs(n, tq, width, mod_index):
    nblk = n // HALO
    per = tq // HALO
    prev = pl.BlockSpec((1, HALO, width), lambda i, b: (b, jnp.maximum(i * per - 1, 0), 0))
    cur = pl.BlockSpec((1, tq, width), lambda i, b: (b, i, 0))
    nxt = pl.BlockSpec((1, HALO, width), lambda i, b: (b, jnp.minimum((i + 1) * per, nblk - 1), 0))
    return [prev, cur, nxt]


def _fill_halo_rows(h_scr, xp_ref, xc_ref, xn_ref, fn, tq):
    i = pl.program_id(0)
    last = pl.num_programs(0) - 1
    h_scr[HALO:HALO + tq, :] = fn(xc_ref[0])
    h_scr[0:HALO, :] = jnp.where(i > 0, fn(xp_ref[0]), 0.0)
    h_scr[HALO + tq:2 * HALO + tq, :] = jnp.where(i < last, fn(xn_ref[0]), 0.0)


def _mod_kernel(c_ref, w_ref, b_ref, o_ref):
    cv = c_ref[...]
    a = (cv * jax.nn.sigmoid(cv)).astype(BF16)
    o_ref[0] = _dot(a, w_ref[0].astype(BF16)) + b_ref[0]


def _modulation(cvec, w_mod, b_mod):
    rows = cvec.shape[0]
    nl = w_mod.shape[0]
    return pl.pallas_call(
        _mod_kernel,
        out_shape=jax.ShapeDtypeStruct((nl, rows, 6 * D_MODEL), F32),
        grid=(nl, 6),
        in_specs=[pl.BlockSpec((rows, D_MODEL), lambda l, j: (0, 0)),
                  pl.BlockSpec((1, D_MODEL, D_MODEL), lambda l, j: (l, 0, j)),
                  pl.BlockSpec((1, 1, D_MODEL), lambda l, j: (l, 0, j))],
        out_specs=pl.BlockSpec((1, rows, D_MODEL), lambda l, j: (l, 0, j)),
        compiler_params=_params(),
        name="adaln_mod",
    )(cvec, w_mod, b_mod.reshape(nl, 1, 6 * D_MODEL))


def _in_proj_kernel(xp_ref, xc_ref, xn_ref, m_ref, norm_ref, w_ref, e_ref, gain_ref, cos_ref, sin_ref,
                    conv_ref, pm_ref, ps_ref,
                    q_ref, k_ref, v_ref, sc_ref, xf_ref, yp_ref,
                    h_scr, z_scr, *, n, tq, rope):
    i = pl.program_id(0)
    norm = norm_ref[...]
    shift = m_ref[0, 0:1, :]
    scale = m_ref[0, 1:2, :]
    _fill_halo_rows(h_scr, xp_ref, xc_ref, xn_ref,
                    lambda xv: _rms_mod(xv, norm, shift, scale), tq)
    z_scr[...] = _dot(h_scr[...].astype(BF16), w_ref[...])

    lane = lax.broadcasted_iota(jnp.int32, (tq, LANES), 1)
    low_half = lane < HEAD_DIM
    rows = slice(HALO, HALO + tq)

    def dup_heads(a):
        sw = pltpu.roll(a, HEAD_DIM, 1)
        return jnp.where(low_half, a, sw), jnp.where(low_half, sw, a)

    zqk = z_scr[rows, 0:QK_WIDTH]
    sq = zqk * zqk
    sq_hi = sq.astype(BF16)
    sq_lo = (sq - sq_hi.astype(F32)).astype(BF16)
    ss = _dot(sq_hi, e_ref[...]) + _dot(sq_lo, e_ref[...])
    y = zqk * lax.rsqrt(ss * (1.0 / HEAD_DIM) + NORM_EPS) * gain_ref[...]
    second_half = (lane & AXIS_PAIRS) != 0
    for c in range(QK_WIDTH // LANES):
        yc = y[:, c * LANES:(c + 1) * LANES]
        if rope:
            partner = jnp.where(second_half, pltpu.roll(yc, AXIS_PAIRS, 1),
                                pltpu.roll(yc, LANES - AXIS_PAIRS, 1))
            yc = yc * cos_ref[...] + partner * sin_ref[...]
        if c < Q_WIDTH // LANES:
            q_ref[0, :, c * LANES:(c + 1) * LANES] = yc.astype(BF16)
        else:
            k0, k1 = dup_heads(yc)
            k_ref[0, :, 0:LANES] = k0.astype(BF16)
            k_ref[0, :, LANES:2 * LANES] = k1.astype(BF16)
    v0, v1 = dup_heads(z_scr[rows, OFF_V:OFF_SC])
    v_ref[0, :, 0:LANES] = v0.astype(BF16)
    v_ref[0, :, LANES:2 * LANES] = v1.astype(BF16)

    def u_at(off):
        r = slice(HALO + off, HALO + off + tq)
        return z_scr[r, OFF_SC + SC_WIDTH:OFF_SC + 2 * SC_WIDTH] * z_scr[r, OFF_SC + 2 * SC_WIDTH:OFF_F]

    conv = u_at(-1) * conv_ref[0:1, :] + u_at(0) * conv_ref[1:2, :] + u_at(1) * conv_ref[2:3, :]
    sc_ref[0] = (z_scr[rows, OFF_SC:OFF_SC + SC_WIDTH] * conv).astype(BF16)

    xf_ref[0] = z_scr[rows, OFF_F:OFF_P].astype(BF16)

    t = i * tq + lax.broadcasted_iota(jnp.int32, (tq, LANES), 0)

    def p_at(off, c):
        return z_scr[HALO + off:HALO + off + tq, OFF_P + c * LANES:OFF_P + (c + 1) * LANES]

    def window_mean(total, w_low, w_high):
        w = jnp.where(low_half, w_low, w_high)
        left, right = (w - 1) // 2, w // 2
        cnt = jnp.minimum(t + right + 1, n) - jnp.maximum(t - left, 0)
        return total / cnt.astype(F32)

    p0 = p_at(0, 0)
    s2 = p0 + p_at(1, 0)
    s4 = s2 + p_at(-1, 0) + p_at(2, 0)
    pooled0 = window_mean(jnp.where(low_half, s2, s4), POOL_WINDOWS[0], POOL_WINDOWS[1]) - p0
    p1 = p_at(0, 1)
    s8 = p1
    for off in (-3, -2, -1, 1, 2, 3, 4):
        s8 = s8 + p_at(off, 1)
    s16 = s8
    for off in (-7, -6, -5, -4, 5, 6, 7, 8):
        s16 = s16 + p_at(off, 1)
    pooled1 = window_mean(jnp.where(low_half, s8, s16), POOL_WINDOWS[2], POOL_WINDOWS[3]) - p1
    pooled = jnp.concatenate([pooled0, pooled1], axis=1).astype(BF16)
    yp_ref[0] = (_dot(pooled, pm_ref[...]) * ps_ref[...]).astype(BF16)


def _in_proj(xs, m, norm, w_a, e_mat, gain, cos_t, sin_t, conv, pm, ps, *, tq, rope):
    b, n, _ = xs.shape
    m_index = (lambda i, bb: (bb, 0, 0)) if m.shape[0] > 1 else (lambda i, bb: (0, 0, 0))
    out_w = (Q_WIDTH, 2 * LANES, 2 * LANES, SC_WIDTH, FOURIER_WIDTH, POOL_WIDTH)
    kern = functools.partial(_in_proj_kernel, n=n, tq=tq, rope=rope)
    return pl.pallas_call(
        kern,
        out_shape=[jax.ShapeDtypeStruct((b, n, w), BF16) for w in out_w],
        grid=(n // tq, b),
        in_specs=_halo_specs(n, tq, D_MODEL, None) + [
            pl.BlockSpec((1, MOD_ROWS, D_MODEL), m_index),
            _const_spec((1, D_MODEL)),
            _const_spec((D_MODEL, OFF_G)),
            _const_spec((QK_WIDTH, QK_WIDTH)),
            _const_spec((1, QK_WIDTH)),
            pl.BlockSpec((tq, LANES), lambda i, bb: (i, 0)),
            pl.BlockSpec((tq, LANES), lambda i, bb: (i, 0)),
            _const_spec((3, SC_WIDTH)),
            _const_spec((POOL_WIDTH, POOL_WIDTH)),
            _const_spec((1, POOL_WIDTH)),
        ],
        out_specs=[pl.BlockSpec((1, tq, w), lambda i, bb: (bb, i, 0)) for w in out_w],
        scratch_shapes=[pltpu.VMEM((tq + 2 * HALO, D_MODEL), F32),
                        pltpu.VMEM((tq + 2 * HALO, OFF_G), F32)],
        compiler_params=_params(),
        name="in_proj",
    )(xs, xs, xs, m, norm, w_a, e_mat, gain, cos_t, sin_t, conv, pm, ps)


def _attn_kernel(q_ref, k_ref, v_ref, o_ref, *, tq):
    lane = lax.broadcasted_iota(jnp.int32, (tq, LANES), 1)
    low_half = lane < HEAD_DIM
    for j in range(Q_WIDTH // LANES):
        g = (2 * j) // (N_HEADS // N_KV_HEADS)
        qp = q_ref[0, :, j * LANES:(j + 1) * LANES]
        zero = jnp.zeros_like(qp)
        q2 = jnp.concatenate([jnp.where(low_half, qp, zero), jnp.where(low_half, zero, qp)], axis=0)
        kg = k_ref[0, :, g * LANES:(g + 1) * LANES]
        s = lax.dot_general(q2, kg, (((1,), (1,)), ((), ())), preferred_element_type=F32)
        p = jnp.exp(s - jnp.max(s, axis=-1, keepdims=True))
        denom = jnp.sum(p, axis=-1, keepdims=True)
        o = _dot(p.astype(BF16), v_ref[0, :, g * LANES:(g + 1) * LANES]) / denom
        o_ref[0, :, j * LANES:(j + 1) * LANES] = jnp.where(low_half, o[:tq], o[tq:]).astype(BF16)


def _attention(q, k, v, *, tq):
    b, n, _ = q.shape
    s = k.shape[1]
    return pl.pallas_call(
        functools.partial(_attn_kernel, tq=tq),
        out_shape=jax.ShapeDtypeStruct((b, n, Q_WIDTH), BF16),
        grid=(b, n // tq),
        in_specs=[pl.BlockSpec((1, tq, Q_WIDTH), lambda bb, i: (bb, i, 0)),
                  pl.BlockSpec((1, s, 2 * LANES), lambda bb, i: (bb, 0, 0)),
                  pl.BlockSpec((1, s, 2 * LANES), lambda bb, i: (bb, 0, 0))],
        out_specs=pl.BlockSpec((1, tq, Q_WIDTH), lambda bb, i: (bb, i, 0)),
        compiler_params=_params(),
        name="attention",
    )(q, k, v)


def _fourier_kernel(x_ref, wc_ref, cn_ref, nsn_ref, o_ref, t_scr, *, n, tr):
    t_scr[...] = _dot(x_ref[0], wc_ref[...]).astype(BF16)
    scale = 1.0 / math.sqrt(n * FOURIER_GROUP)
    for r in range(n // tr):
        rows = slice(r * tr, (r + 1) * tr)
        acc = (_dot(cn_ref[rows, :], t_scr[:, 0:FOURIER_WIDTH])
               + _dot(nsn_ref[rows, :], t_scr[:, FOURIER_WIDTH:2 * FOURIER_WIDTH]))
        o_ref[0, rows, :] = (acc * scale).astype(BF16)


def _fourier(xf, wc, cn, nsn):
    b, n, _ = xf.shape
    tr = min(n, 512)
    return pl.pallas_call(
        functools.partial(_fourier_kernel, n=n, tr=tr),
        out_shape=jax.ShapeDtypeStruct((b, n, FOURIER_WIDTH), BF16),
        grid=(b, 1),
        in_specs=[pl.BlockSpec((1, n, FOURIER_WIDTH), lambda bb, i: (bb, 0, 0)),
                  _const_spec((FOURIER_WIDTH, 2 * FOURIER_WIDTH)),
                  _const_spec((n, n)),
                  _const_spec((n, n))],
        out_specs=pl.BlockSpec((1, n, FOURIER_WIDTH), lambda bb, i: (bb, 0, 0)),
        scratch_shapes=[pltpu.VMEM((n, 2 * FOURIER_WIDTH), BF16)],
        compiler_params=_params(),
        name="fourier",
    )(xf, wc, cn, nsn)


def _mix_kernel(x_ref, m_ref, norm_ref, oa_ref, sc_ref, xf_ref, yp_ref,
                wg_ref, wa_ref, wsc_ref, wf_ref, wp_ref, wo_ref, out_ref, y_scr):
    xv = x_ref[0]
    h = _rms_mod(xv, norm_ref[...], m_ref[0, 0:1, :], m_ref[0, 1:2, :]).astype(BF16)
    branches = ((oa_ref, wa_ref), (sc_ref, wsc_ref), (xf_ref, wf_ref), (yp_ref, wp_ref))
    for j in range(D_MODEL // FF_CHUNK):
        cols = slice(j * FF_CHUNK, (j + 1) * FF_CHUNK)
        y = None
        for bi, (a_ref, w_ref) in enumerate(branches):
            gcols = slice(bi * D_MODEL + j * FF_CHUNK, bi * D_MODEL + (j + 1) * FF_CHUNK)
            gate = jax.nn.sigmoid(_dot(h, wg_ref[:, gcols]))
            term = gate * _dot(a_ref[0], w_ref[:, cols])
            y = term if y is None else y + term
        y_scr[:, cols] = y.astype(BF16)
    out_ref[0] = xv + m_ref[0, 2:3, :] * _dot(y_scr[...], wo_ref[...])


def _mix(xs, m, norm, oa, sc, xf, yp, wg, wa, wsc, wf, wp, wo, *, tq):
    b, n, _ = xs.shape
    m_index = (lambda i, bb: (bb, 0, 0)) if m.shape[0] > 1 else (lambda i, bb: (0, 0, 0))

    def tok(w):
        return pl.BlockSpec((1, tq, w), lambda i, bb: (bb, i, 0))

    return pl.pallas_call(
        _mix_kernel,
        out_shape=jax.ShapeDtypeStruct((b, n, D_MODEL), F32),
        grid=(n // tq, b),
        in_specs=[tok(D_MODEL), pl.BlockSpec((1, MOD_ROWS, D_MODEL), m_index), _const_spec((1, D_MODEL)),
                  tok(Q_WIDTH), tok(SC_WIDTH), tok(FOURIER_WIDTH), tok(POOL_WIDTH),
                  _const_spec((D_MODEL, N_BRANCHES * D_MODEL)),
                  _const_spec((Q_WIDTH, D_MODEL)), _const_spec((SC_WIDTH, D_MODEL)),
                  _const_spec((FOURIER_WIDTH, D_MODEL)), _const_spec((POOL_WIDTH, D_MODEL)),
                  _const_spec((D_MODEL, D_MODEL))],
        out_specs=tok(D_MODEL),
        scratch_shapes=[pltpu.VMEM((tq, D_MODEL), BF16)],
        compiler_params=_params(),
        name="mix",
    )(xs, m, norm, oa, sc, xf, yp, wg, wa, wsc, wf, wp, wo)


def _ffn_kernel(xp_ref, xc_ref, xn_ref, m_ref, norm_ref, wu_ref, cw_ref, wd_ref, fn_ref, out_ref,
                h_scr, ua_scr, ub_scr, acc_scr, *, tq, final):
    norm = norm_ref[...]
    shift = m_ref[0, 3:4, :]
    scale = m_ref[0, 4:5, :]
    _fill_halo_rows(h_scr, xp_ref, xc_ref, xn_ref,
                    lambda xv: _rms_mod(xv, norm, shift, scale), tq)
    hb = h_scr[...].astype(BF16)
    acc_scr[...] = jnp.zeros_like(acc_scr)

    def conv3(u_scr, w):
        return (u_scr[HALO - 1:HALO - 1 + tq, :] * w[0:1, :] + u_scr[HALO:HALO + tq, :] * w[1:2, :]
                + u_scr[HALO + 1:HALO + 1 + tq, :] * w[2:3, :])

    def chunk(c, carry):
        ua_scr[...] = _dot(hb, wu_ref[c])
        ub_scr[...] = _dot(hb, wu_ref[N_FF_CHUNKS + c])
        a = conv3(ua_scr, cw_ref[c])
        bv = conv3(ub_scr, cw_ref[N_FF_CHUNKS + c])
        act = (a * jax.nn.sigmoid(a) * bv).astype(BF16)
        acc_scr[...] += _dot(act, wd_ref[c])
        return carry

    lax.fori_loop(0, N_FF_CHUNKS, chunk, 0)
    out = xc_ref[0] + m_ref[0, 5:6, :] * acc_scr[...]
    if final:
        ms = jnp.mean(out * out, axis=-1, keepdims=True)
        out = out * lax.rsqrt(ms + NORM_EPS) * fn_ref[...]
    out_ref[0] = out


def _ffn(xs, m, norm, wu, cw, wd, fnorm, *, tq, final):
    b, n, _ = xs.shape
    m_index = (lambda i, bb: (bb, 0, 0)) if m.shape[0] > 1 else (lambda i, bb: (0, 0, 0))
    return pl.pallas_call(
        functools.partial(_ffn_kernel, tq=tq, final=final),
        out_shape=jax.ShapeDtypeStruct((b, n, D_MODEL), F32),
        grid=(n // tq, b),
        in_specs=_halo_specs(n, tq, D_MODEL, None) + [
            pl.BlockSpec((1, MOD_ROWS, D_MODEL), m_index),
            _const_spec((1, D_MODEL)),
            _const_spec((2 * N_FF_CHUNKS, D_MODEL, FF_CHUNK)),
            _const_spec((2 * N_FF_CHUNKS, 3, FF_CHUNK)),
            _const_spec((N_FF_CHUNKS, FF_CHUNK, D_MODEL)),
            _const_spec((1, D_MODEL))],
        out_specs=pl.BlockSpec((1, tq, D_MODEL), lambda i, bb: (bb, i, 0)),
        scratch_shapes=[pltpu.VMEM((tq + 2 * HALO, D_MODEL), F32),
                        pltpu.VMEM((tq + 2 * HALO, FF_CHUNK), F32),
                        pltpu.VMEM((tq + 2 * HALO, FF_CHUNK), F32),
                        pltpu.VMEM((tq, D_MODEL), F32)],
        compiler_params=_params(),
        name="ffn",
    )(xs, xs, xs, m, norm, wu, cw, wd, fnorm)


def _rope_tables(n):
    d = jnp.arange(LANES)
    hd = d % HEAD_DIM
    axis = hd // (2 * AXIS_PAIRS)
    half = (hd % (2 * AXIS_PAIRS)) // AXIS_PAIRS
    pair = hd % AXIS_PAIRS
    t = jnp.arange(n)
    row = (t // GRID_W).astype(F32)
    col = (t % GRID_W).astype(F32)
    inv = ROPE_THETA ** (-jnp.arange(AXIS_PAIRS, dtype=F32) / AXIS_PAIRS)
    pos = jnp.where(axis[None, :] == 0, row[:, None], col[:, None])
    ang = pos * inv[pair][None, :]
    sin = jnp.sin(ang)
    return jnp.cos(ang), jnp.where(half[None, :] == 0, -sin, sin)


def _dft_tables(n):
    k = jnp.arange(n, dtype=jnp.int32)
    kt = (k[:, None] * k[None, :]) & (n - 1)
    ang = kt.astype(F32) * (2.0 * math.pi / n)
    return jnp.cos(ang).astype(BF16), (-jnp.sin(ang)).astype(BF16)


def _channel_dft():
    c = jnp.arange(FOURIER_WIDTH, dtype=jnp.int32)
    same = (c[:, None] // FOURIER_GROUP) == (c[None, :] // FOURIER_GROUP)
    mc = ((c[:, None] % FOURIER_GROUP) * (c[None, :] % FOURIER_GROUP)) & (FOURIER_GROUP - 1)
    ang = mc.astype(F32) * (2.0 * math.pi / FOURIER_GROUP)
    cc = jnp.where(same, jnp.cos(ang), 0.0)
    sc = jnp.where(same, jnp.sin(ang), 0.0)
    return jnp.concatenate([cc, sc], axis=1).astype(BF16)


def _head_sum_matrix():
    r = jnp.arange(QK_WIDTH) // HEAD_DIM
    return (r[:, None] == r[None, :]).astype(BF16)


def _block_diag(mats):
    g, a, b = mats.shape
    eye = jnp.eye(g, dtype=mats.dtype)
    return (mats[:, :, None, :] * eye[:, None, :, None]).reshape(g * a, g * b)


def kernel(x, c, ctx, c_ctx, w_mod, b_mod, norm1, norm2, w_in, conv_sc, qk_gain, pool_mat, pool_scale,
           w_br_attn, w_br_sc, w_br_f, w_br_p, w_out, w_up, w_conv_ffn, w_down, final_norm):
    bsz, n_lat, _ = x.shape
    n_ctx = ctx.shape[1]
    tq_lat = 512
    tq_ctx = n_ctx

    cvec = jnp.concatenate([c, c_ctx[None], jnp.zeros((16 - bsz - 1, D_MODEL), F32)], axis=0)
    mod = _modulation(cvec, w_mod, b_mod)
    pad = jnp.zeros((bsz, MOD_ROWS - 6, D_MODEL), F32)

    cos_t, sin_t = _rope_tables(n_lat)
    cos_c, sin_c = cos_t[:n_ctx], sin_t[:n_ctx]
    e_mat = _head_sum_matrix()
    wc = _channel_dft()
    dft_lat = _dft_tables(n_lat)
    dft_ctx = _dft_tables(n_ctx)
    fnorm = final_norm.reshape(1, D_MODEL)

    lat, cx = x, ctx
    for l in range(DEPTH):
        last = l == DEPTH - 1
        m_lat = jnp.concatenate([mod[l, :bsz].reshape(bsz, 6, D_MODEL), pad], axis=1)
        m_ctx = jnp.concatenate([mod[l, bsz:bsz + 1].reshape(1, 6, D_MODEL), pad[:1]], axis=1)
        n1 = norm1[l].reshape(1, D_MODEL)
        n2 = norm2[l].reshape(1, D_MODEL)
        w_a = w_in[l, :, :OFF_G].astype(BF16)
        w_g = w_in[l, :, OFF_G:].astype(BF16)
        q_gain = jnp.tile(qk_gain[l, 0], N_HEADS) * (HEAD_DIM ** -0.5)
        k_gain = jnp.tile(qk_gain[l, 1], N_KV_HEADS)
        gain = jnp.concatenate([q_gain, k_gain]).reshape(1, QK_WIDTH)
        pm = _block_diag(pool_mat[l]).astype(BF16)
        ps = pool_scale[l].reshape(1, POOL_WIDTH)
        local = (e_mat, gain)
        mixer_w = (w_g, w_br_attn[l].astype(BF16), w_br_sc[l].astype(BF16), w_br_f[l].astype(BF16),
                   w_br_p[l].astype(BF16), w_out[l].astype(BF16))
        wu = w_up[l].astype(BF16).reshape(D_MODEL, 2 * N_FF_CHUNKS, FF_CHUNK).transpose(1, 0, 2)
        cw = w_conv_ffn[l].reshape(3, 2 * N_FF_CHUNKS, FF_CHUNK).transpose(1, 0, 2)
        wd = w_down[l].astype(BF16).reshape(N_FF_CHUNKS, FF_CHUNK, D_MODEL)

        qc, kc, vc, scc, xfc, ypc = _in_proj(cx, m_ctx, n1, w_a, *local, cos_c, sin_c, conv_sc[l], pm, ps,
                                             tq=tq_ctx, rope=False)
        ql, kl, vl, scl, xfl, ypl = _in_proj(lat, m_lat, n1, w_a, *local, cos_t, sin_t, conv_sc[l], pm, ps,
                                             tq=tq_lat, rope=True)
        oa = _attention(ql, jnp.concatenate([kl, kc], axis=1), jnp.concatenate([vl, vc], axis=1), tq=256)
        xfour = _fourier(xfl, wc, *dft_lat)
        lat = _mix(lat, m_lat, n1, oa, scl, xfour, ypl, *mixer_w, tq=tq_lat)
        lat = _ffn(lat, m_lat, n2, wu, cw, wd, fnorm, tq=tq_lat, final=last)

        if not last:
            oac = _attention(qc, kc, vc, tq=tq_ctx)
            xfourc = _fourier(xfc, wc, *dft_ctx)
            cx = _mix(cx, m_ctx, n1, oac, scc, xfourc, ypc, *mixer_w, tq=tq_ctx)
            cx = _ffn(cx, m_ctx, n2, wu, cw, wd, fnorm, tq=tq_ctx, final=False)
    return lat
```

```python
import functools
import math

import jax
import jax.numpy as jnp
from jax import lax
from jax.experimental import pallas as pl
from jax.experimental.pallas import tpu as pltpu

D_MODEL = 1024
DEPTH = 2
GRID_W = 64
NORM_EPS = 1e-6
N_HEADS = 8
N_KV_HEADS = 2
HEAD_DIM = 64
Q_WIDTH = N_HEADS * HEAD_DIM
KV_WIDTH = N_KV_HEADS * HEAD_DIM
QK_WIDTH = Q_WIDTH + KV_WIDTH
ROPE_THETA = 10000.0
AXIS_PAIRS = HEAD_DIM // 4
SC_WIDTH = 256
FOURIER_GROUPS = 4
FOURIER_WIDTH = 256
FOURIER_GROUP = FOURIER_WIDTH // FOURIER_GROUPS
POOL_WINDOWS = (2, 4, 8, 16)
POOL_WIDTH = 256
POOL_GROUP = POOL_WIDTH // len(POOL_WINDOWS)
N_BRANCHES = 4
D_FF = 2816

OFF_K = Q_WIDTH
OFF_V = OFF_K + KV_WIDTH
OFF_SC = OFF_V + KV_WIDTH
OFF_F = OFF_SC + 3 * SC_WIDTH
OFF_P = OFF_F + FOURIER_WIDTH
OFF_G = OFF_P + POOL_WIDTH

LANES = 128
HALO = 8
FF_CHUNK = 256
N_FF_CHUNKS = D_FF // FF_CHUNK
ATTN_SUB = 256
TOKEN_TILE = 512
ATTN_TILE = 1024
FFN_TILE = 256
MOD_ROWS = 8
VMEM_LIMIT = 56 * 1024 * 1024

F32 = jnp.float32
BF16 = jnp.bfloat16


def _dot(a, b):
    return jnp.dot(a, b, preferred_element_type=F32)


def _const_spec(shape):
    nd = len(shape)
    return pl.BlockSpec(shape, lambda *_: (0,) * nd, pipeline_mode=pl.Buffered(1))


def _params():
    return pltpu.CompilerParams(dimension_semantics=("arbitrary", "arbitrary"),
                                vmem_limit_bytes=VMEM_LIMIT)


def _rms_mod(xv, norm, shift, scale):
    ms = jnp.mean(xv * xv, axis=-1, keepdims=True)
    y = xv * lax.rsqrt(ms + NORM_EPS) * norm
    return y * (1.0 + scale) + shift


def _halo_specs(n, tq, width):
    nblk = n // HALO
    per = tq // HALO
    prev = pl.BlockSpec((1, HALO, width), lambda i, b: (b, jnp.maximum(i * per - 1, 0), 0))
    cur = pl.BlockSpec((1, tq, width), lambda i, b: (b, i, 0))
    nxt = pl.BlockSpec((1, HALO, width), lambda i, b: (b, jnp.minimum((i + 1) * per, nblk - 1), 0))
    return [prev, cur, nxt]


def _fill_halo_rows(h_scr, xp_ref, xc_ref, xn_ref, fn, tq):
    i = pl.program_id(0)
    last = pl.num_programs(0) - 1
    h_scr[HALO:HALO + tq, :] = fn(xc_ref[0])
    h_scr[0:HALO, :] = jnp.where(i > 0, fn(xp_ref[0]), 0.0)
    h_scr[HALO + tq:2 * HALO + tq, :] = jnp.where(i < last, fn(xn_ref[0]), 0.0)


def _mod_kernel(c_ref, w_ref, b_ref, o_ref):
    cv = c_ref[...]
    a = (cv * jax.nn.sigmoid(cv)).astype(BF16)
    o_ref[0] = _dot(a, w_ref[0].astype(BF16)) + b_ref[0]


def _modulation(cvec, w_mod, b_mod):
    rows = cvec.shape[0]
    nl = w_mod.shape[0]
    return pl.pallas_call(
        _mod_kernel,
        out_shape=jax.ShapeDtypeStruct((nl, rows, 6 * D_MODEL), F32),
        grid=(nl, 6),
        in_specs=[pl.BlockSpec((rows, D_MODEL), lambda l, j: (0, 0)),
                  pl.BlockSpec((1, D_MODEL, D_MODEL), lambda l, j: (l, 0, j)),
                  pl.BlockSpec((1, 1, D_MODEL), lambda l, j: (l, 0, j))],
        out_specs=pl.BlockSpec((1, rows, D_MODEL), lambda l, j: (l, 0, j)),
        compiler_params=_params(),
        name="adaln_mod",
    )(cvec, w_mod, b_mod.reshape(nl, 1, 6 * D_MODEL))


def _in_proj_kernel(xp_ref, xc_ref, xn_ref, m_ref, norm_ref, w_ref, e_ref, gain_ref, cos_ref, sin_ref,
                    conv_ref, pm_ref, ps_ref,
                    q_ref, k_ref, v_ref, sc_ref, xf_ref, yp_ref,
                    h_scr, z_scr, *, n, tq, rope):
    i = pl.program_id(0)
    norm = norm_ref[...]
    shift = m_ref[0, 0:1, :]
    scale = m_ref[0, 1:2, :]
    _fill_halo_rows(h_scr, xp_ref, xc_ref, xn_ref,
                    lambda xv: _rms_mod(xv, norm, shift, scale), tq)
    z_scr[...] = _dot(h_scr[...].astype(BF16), w_ref[...])

    lane = lax.broadcasted_iota(jnp.int32, (tq, LANES), 1)
    low_half = lane < HEAD_DIM
    rows = slice(HALO, HALO + tq)

    def dup_heads(a):
        sw = pltpu.roll(a, HEAD_DIM, 1)
        return jnp.where(low_half, a, sw), jnp.where(low_half, sw, a)

    zqk = z_scr[rows, 0:QK_WIDTH]
    sq = zqk * zqk
    sq_hi = sq.astype(BF16)
    sq_lo = (sq - sq_hi.astype(F32)).astype(BF16)
    ss = _dot(sq_hi, e_ref[...]) + _dot(sq_lo, e_ref[...])
    y = zqk * lax.rsqrt(ss * (1.0 / HEAD_DIM) + NORM_EPS) * gain_ref[...]
    second_half = (lane & AXIS_PAIRS) != 0
    for c in range(QK_WIDTH // LANES):
        yc = y[:, c * LANES:(c + 1) * LANES]
        if rope:
            partner = jnp.where(second_half, pltpu.roll(yc, AXIS_PAIRS, 1),
                                pltpu.roll(yc, LANES - AXIS_PAIRS, 1))
            yc = yc * cos_ref[...] + partner * sin_ref[...]
        if c < Q_WIDTH // LANES:
            q_ref[0, :, c * LANES:(c + 1) * LANES] = yc.astype(BF16)
        else:
            k0, k1 = dup_heads(yc)
            k_ref[0, :, 0:LANES] = k0.astype(BF16)
            k_ref[0, :, LANES:2 * LANES] = k1.astype(BF16)
    v0, v1 = dup_heads(z_scr[rows, OFF_V:OFF_SC])
    v_ref[0, :, 0:LANES] = v0.astype(BF16)
    v_ref[0, :, LANES:2 * LANES] = v1.astype(BF16)

    def u_at(off):
        r = slice(HALO + off, HALO + off + tq)
        return z_scr[r, OFF_SC + SC_WIDTH:OFF_SC + 2 * SC_WIDTH] * z_scr[r, OFF_SC + 2 * SC_WIDTH:OFF_F]

    conv = u_at(-1) * conv_ref[0:1, :] + u_at(0) * conv_ref[1:2, :] + u_at(1) * conv_ref[2:3, :]
    sc_ref[0] = (z_scr[rows, OFF_SC:OFF_SC + SC_WIDTH] * conv).astype(BF16)

    xf_ref[0] = z_scr[rows, OFF_F:OFF_P].astype(BF16)

    t = i * tq + lax.broadcasted_iota(jnp.int32, (tq, LANES), 0)

    def p_at(off, c):
        return z_scr[HALO + off:HALO + off + tq, OFF_P + c * LANES:OFF_P + (c + 1) * LANES]

    def window_mean(total, w_low, w_high):
        w = jnp.where(low_half, w_low, w_high)
        left, right = (w - 1) // 2, w // 2
        cnt = jnp.minimum(t + right + 1, n) - jnp.maximum(t - left, 0)
        return total / cnt.astype(F32)

    p0 = p_at(0, 0)
    s2 = p0 + p_at(1, 0)
    s4 = s2 + p_at(-1, 0) + p_at(2, 0)
    pooled0 = window_mean(jnp.where(low_half, s2, s4), POOL_WINDOWS[0], POOL_WINDOWS[1]) - p0
    p1 = p_at(0, 1)
    s8 = p1
    for off in (-3, -2, -1, 1, 2, 3, 4):
        s8 = s8 + p_at(off, 1)
    s16 = s8
    for off in (-7, -6, -5, -4, 5, 6, 7, 8):
        s16 = s16 + p_at(off, 1)
    pooled1 = window_mean(jnp.where(low_half, s8, s16), POOL_WINDOWS[2], POOL_WINDOWS[3]) - p1
    pooled = jnp.concatenate([pooled0, pooled1], axis=1).astype(BF16)
    yp_ref[0] = (_dot(pooled, pm_ref[...]) * ps_ref[...]).astype(BF16)


def _in_proj(xs, m, norm, w_a, e_mat, gain, cos_t, sin_t, conv, pm, ps, *, tq, rope):
    b, n, _ = xs.shape
    m_index = (lambda i, bb: (bb, 0, 0)) if m.shape[0] > 1 else (lambda i, bb: (0, 0, 0))
    out_w = (Q_WIDTH, 2 * LANES, 2 * LANES, SC_WIDTH, FOURIER_WIDTH, POOL_WIDTH)
    kern = functools.partial(_in_proj_kernel, n=n, tq=tq, rope=rope)
    return pl.pallas_call(
        kern,
        out_shape=[jax.ShapeDtypeStruct((b, n, w), BF16) for w in out_w],
        grid=(n // tq, b),
        in_specs=_halo_specs(n, tq, D_MODEL) + [
            pl.BlockSpec((1, MOD_ROWS, D_MODEL), m_index),
            _const_spec((1, D_MODEL)),
            _const_spec((D_MODEL, OFF_G)),
            _const_spec((QK_WIDTH, QK_WIDTH)),
            _const_spec((1, QK_WIDTH)),
            pl.BlockSpec((tq, LANES), lambda i, bb: (i, 0)),
            pl.BlockSpec((tq, LANES), lambda i, bb: (i, 0)),
            _const_spec((3, SC_WIDTH)),
            _const_spec((POOL_WIDTH, POOL_WIDTH)),
            _const_spec((1, POOL_WIDTH)),
        ],
        out_specs=[pl.BlockSpec((1, tq, w), lambda i, bb: (bb, i, 0)) for w in out_w],
        scratch_shapes=[pltpu.VMEM((tq + 2 * HALO, D_MODEL), F32),
                        pltpu.VMEM((tq + 2 * HALO, OFF_G), F32)],
        compiler_params=_params(),
        name="in_proj",
    )(xs, xs, xs, m, norm, w_a, e_mat, gain, cos_t, sin_t, conv, pm, ps)


def _attn_kernel(q_ref, k_ref, v_ref, o_ref, *, tq):
    sub = ATTN_SUB
    lane = lax.broadcasted_iota(jnp.int32, (sub, LANES), 1)
    low_half = lane < HEAD_DIM
    for j in range(Q_WIDTH // LANES):
        g = (2 * j) // (N_HEADS // N_KV_HEADS)
        for r in range(tq // sub):
            rows = slice(r * sub, (r + 1) * sub)
            qp = q_ref[0, rows, j * LANES:(j + 1) * LANES]
            zero = jnp.zeros_like(qp)
            q2 = jnp.concatenate([jnp.where(low_half, qp, zero), jnp.where(low_half, zero, qp)], axis=0)
            kg = k_ref[0, :, g * LANES:(g + 1) * LANES]
            s = lax.dot_general(q2, kg, (((1,), (1,)), ((), ())), preferred_element_type=F32)
            p = jnp.exp(s - jnp.max(s, axis=-1, keepdims=True))
            denom = jnp.sum(p, axis=-1, keepdims=True)
            o = _dot(p.astype(BF16), v_ref[0, :, g * LANES:(g + 1) * LANES]) / denom
            o_ref[0, rows, j * LANES:(j + 1) * LANES] = jnp.where(low_half, o[:sub], o[sub:]).astype(BF16)


def _attention(q, k, v, *, tq):
    b, n, _ = q.shape
    s = k.shape[1]
    return pl.pallas_call(
        functools.partial(_attn_kernel, tq=tq),
        out_shape=jax.ShapeDtypeStruct((b, n, Q_WIDTH), BF16),
        grid=(b, n // tq),
        in_specs=[pl.BlockSpec((1, tq, Q_WIDTH), lambda bb, i: (bb, i, 0)),
                  pl.BlockSpec((1, s, 2 * LANES), lambda bb, i: (bb, 0, 0)),
                  pl.BlockSpec((1, s, 2 * LANES), lambda bb, i: (bb, 0, 0))],
        out_specs=pl.BlockSpec((1, tq, Q_WIDTH), lambda bb, i: (bb, i, 0)),
        compiler_params=_params(),
        name="attention",
    )(q, k, v)


def _fourier_kernel(x_ref, wc_ref, cn_ref, nsn_ref, o_ref, t_scr, *, n, tr):
    t_scr[...] = _dot(x_ref[0], wc_ref[...]).astype(BF16)
    scale = 1.0 / math.sqrt(n * FOURIER_GROUP)
    for r in range(n // tr):
        rows = slice(r * tr, (r + 1) * tr)
        acc = (_dot(cn_ref[rows, :], t_scr[:, 0:FOURIER_WIDTH])
               + _dot(nsn_ref[rows, :], t_scr[:, FOURIER_WIDTH:2 * FOURIER_WIDTH]))
        o_ref[0, rows, :] = (acc * scale).astype(BF16)


def _fourier(xf, wc, cn, nsn):
    b, n, _ = xf.shape
    tr = min(n, 512)
    return pl.pallas_call(
        functools.partial(_fourier_kernel, n=n, tr=tr),
        out_shape=jax.ShapeDtypeStruct((b, n, FOURIER_WIDTH), BF16),
        grid=(b, 1),
        in_specs=[pl.BlockSpec((1, n, FOURIER_WIDTH), lambda bb, i: (bb, 0, 0)),
                  _const_spec((FOURIER_WIDTH, 2 * FOURIER_WIDTH)),
                  _const_spec((n, n)),
                  _const_spec((n, n))],
        out_specs=pl.BlockSpec((1, n, FOURIER_WIDTH), lambda bb, i: (bb, 0, 0)),
        scratch_shapes=[pltpu.VMEM((n, 2 * FOURIER_WIDTH), BF16)],
        compiler_params=_params(),
        name="fourier",
    )(xf, wc, cn, nsn)


def _mix_kernel(x_ref, m_ref, norm_ref, oa_ref, sc_ref, xf_ref, yp_ref,
                wg_ref, wa_ref, wsc_ref, wf_ref, wp_ref, wo_ref, out_ref, y_scr):
    xv = x_ref[0]
    h = _rms_mod(xv, norm_ref[...], m_ref[0, 0:1, :], m_ref[0, 1:2, :]).astype(BF16)
    branches = ((oa_ref, wa_ref), (sc_ref, wsc_ref), (xf_ref, wf_ref), (yp_ref, wp_ref))
    for j in range(D_MODEL // FF_CHUNK):
        cols = slice(j * FF_CHUNK, (j + 1) * FF_CHUNK)
        y = None
        for bi, (a_ref, w_ref) in enumerate(branches):
            gcols = slice(bi * D_MODEL + j * FF_CHUNK, bi * D_MODEL + (j + 1) * FF_CHUNK)
            gate = jax.nn.sigmoid(_dot(h, wg_ref[:, gcols]))
            term = gate * _dot(a_ref[0], w_ref[:, cols])
            y = term if y is None else y + term
        y_scr[:, cols] = y.astype(BF16)
    out_ref[0] = xv + m_ref[0, 2:3, :] * _dot(y_scr[...], wo_ref[...])


def _mix(xs, m, norm, oa, sc, xf, yp, wg, wa, wsc, wf, wp, wo, *, tq):
    b, n, _ = xs.shape
    m_index = (lambda i, bb: (bb, 0, 0)) if m.shape[0] > 1 else (lambda i, bb: (0, 0, 0))

    def tok(w):
        return pl.BlockSpec((1, tq, w), lambda i, bb: (bb, i, 0))

    return pl.pallas_call(
        _mix_kernel,
        out_shape=jax.ShapeDtypeStruct((b, n, D_MODEL), F32),
        grid=(n // tq, b),
        in_specs=[tok(D_MODEL), pl.BlockSpec((1, MOD_ROWS, D_MODEL), m_index), _const_spec((1, D_MODEL)),
                  tok(Q_WIDTH), tok(SC_WIDTH), tok(FOURIER_WIDTH), tok(POOL_WIDTH),
                  _const_spec((D_MODEL, N_BRANCHES * D_MODEL)),
                  _const_spec((Q_WIDTH, D_MODEL)), _const_spec((SC_WIDTH, D_MODEL)),
                  _const_spec((FOURIER_WIDTH, D_MODEL)), _const_spec((POOL_WIDTH, D_MODEL)),
                  _const_spec((D_MODEL, D_MODEL))],
        out_specs=tok(D_MODEL),
        scratch_shapes=[pltpu.VMEM((tq, D_MODEL), BF16)],
        compiler_params=_params(),
        name="mix",
    )(xs, m, norm, oa, sc, xf, yp, wg, wa, wsc, wf, wp, wo)


def _ffn_kernel(xp_ref, xc_ref, xn_ref, m_ref, norm_ref, wu_ref, cw_ref, wd_ref, fn_ref, out_ref,
                h_scr, hb_scr, u0_scr, u1_scr, acc_scr, *, tq, final):
    norm = norm_ref[...]
    shift = m_ref[0, 3:4, :]
    scale = m_ref[0, 4:5, :]
    _fill_halo_rows(h_scr, xp_ref, xc_ref, xn_ref,
                    lambda xv: _rms_mod(xv, norm, shift, scale), tq)
    hb_scr[...] = h_scr[...].astype(BF16)
    acc_scr[...] = jnp.zeros_like(acc_scr)
    ext = tq + 2 * HALO

    def up(c, u_scr):
        hb = hb_scr[...]
        u_scr[0] = _dot(hb, wu_ref[c])
        u_scr[1] = _dot(hb, wu_ref[N_FF_CHUNKS + c])

    def conv3(u, w):
        full = pltpu.roll(u, 1, 0) * w[0:1, :] + u * w[1:2, :] + pltpu.roll(u, ext - 1, 0) * w[2:3, :]
        return full[HALO:HALO + tq]

    def down(c, u_scr):
        a = conv3(u_scr[0], cw_ref[c])
        bv = conv3(u_scr[1], cw_ref[N_FF_CHUNKS + c])
        act = (a * jax.nn.sigmoid(a) * bv).astype(BF16)
        acc_scr[...] += _dot(act, wd_ref[c])

    bufs = (u0_scr, u1_scr)
    up(0, u0_scr)
    for c in range(N_FF_CHUNKS):
        if c + 1 < N_FF_CHUNKS:
            up(c + 1, bufs[(c + 1) % 2])
        down(c, bufs[c % 2])
    out = xc_ref[0] + m_ref[0, 5:6, :] * acc_scr[...]
    if final:
        ms = jnp.mean(out * out, axis=-1, keepdims=True)
        out = out * lax.rsqrt(ms + NORM_EPS) * fn_ref[...]
    out_ref[0] = out


def _ffn(xs, m, norm, wu, cw, wd, fnorm, *, tq, final):
    b, n, _ = xs.shape
    m_index = (lambda i, bb: (bb, 0, 0)) if m.shape[0] > 1 else (lambda i, bb: (0, 0, 0))
    return pl.pallas_call(
        functools.partial(_ffn_kernel, tq=tq, final=final),
        out_shape=jax.ShapeDtypeStruct((b, n, D_MODEL), F32),
        grid=(n // tq, b),
        in_specs=_halo_specs(n, tq, D_MODEL) + [
            pl.BlockSpec((1, MOD_ROWS, D_MODEL), m_index),
            _const_spec((1, D_MODEL)),
            _const_spec((2 * N_FF_CHUNKS, D_MODEL, FF_CHUNK)),
            _const_spec((2 * N_FF_CHUNKS, 3, FF_CHUNK)),
            _const_spec((N_FF_CHUNKS, FF_CHUNK, D_MODEL)),
            _const_spec((1, D_MODEL))],
        out_specs=pl.BlockSpec((1, tq, D_MODEL), lambda i, bb: (bb, i, 0)),
        scratch_shapes=[pltpu.VMEM((tq + 2 * HALO, D_MODEL), F32),
                        pltpu.VMEM((tq + 2 * HALO, D_MODEL), BF16),
                        pltpu.VMEM((2, tq + 2 * HALO, FF_CHUNK), F32),
                        pltpu.VMEM((2, tq + 2 * HALO, FF_CHUNK), F32),
                        pltpu.VMEM((tq, D_MODEL), F32)],
        compiler_params=_params(),
        name="ffn",
    )(xs, xs, xs, m, norm, wu, cw, wd, fnorm)


def _rope_tables(n):
    d = jnp.arange(LANES)
    hd = d % HEAD_DIM
    axis = hd // (2 * AXIS_PAIRS)
    half = (hd % (2 * AXIS_PAIRS)) // AXIS_PAIRS
    pair = hd % AXIS_PAIRS
    t = jnp.arange(n)
    row = (t // GRID_W).astype(F32)
    col = (t % GRID_W).astype(F32)
    inv = ROPE_THETA ** (-jnp.arange(AXIS_PAIRS, dtype=F32) / AXIS_PAIRS)
    pos = jnp.where(axis[None, :] == 0, row[:, None], col[:, None])
    ang = pos * inv[pair][None, :]
    sin = jnp.sin(ang)
    return jnp.cos(ang), jnp.where(half[None, :] == 0, -sin, sin)


def _dft_tables(n):
    k = jnp.arange(n, dtype=jnp.int32)
    kt = (k[:, None] * k[None, :]) & (n - 1)
    ang = kt.astype(F32) * (2.0 * math.pi / n)
    return jnp.cos(ang).astype(BF16), (-jnp.sin(ang)).astype(BF16)


def _channel_dft():
    c = jnp.arange(FOURIER_WIDTH, dtype=jnp.int32)
    same = (c[:, None] // FOURIER_GROUP) == (c[None, :] // FOURIER_GROUP)
    mc = ((c[:, None] % FOURIER_GROUP) * (c[None, :] % FOURIER_GROUP)) & (FOURIER_GROUP - 1)
    ang = mc.astype(F32) * (2.0 * math.pi / FOURIER_GROUP)
    cc = jnp.where(same, jnp.cos(ang), 0.0)
    sc = jnp.where(same, jnp.sin(ang), 0.0)
    return jnp.concatenate([cc, sc], axis=1).astype(BF16)


def _head_sum_matrix():
    r = jnp.arange(QK_WIDTH) // HEAD_DIM
    return (r[:, None] == r[None, :]).astype(BF16)


def _block_diag(mats):
    g, a, b = mats.shape
    eye = jnp.eye(g, dtype=mats.dtype)
    return (mats[:, :, None, :] * eye[:, None, :, None]).reshape(g * a, g * b)


def kernel(x, c, ctx, c_ctx, w_mod, b_mod, norm1, norm2, w_in, conv_sc, qk_gain, pool_mat, pool_scale,
           w_br_attn, w_br_sc, w_br_f, w_br_p, w_out, w_up, w_conv_ffn, w_down, final_norm):
    bsz, n_lat, _ = x.shape
    n_ctx = ctx.shape[1]
    tq_lat = min(n_lat, TOKEN_TILE)
    tq_ctx = min(n_ctx, TOKEN_TILE)
    tq_attn = min(n_lat, ATTN_TILE)
    tq_ffn_lat = min(n_lat, FFN_TILE)
    tq_ffn_ctx = min(n_ctx, FFN_TILE)

    cvec = jnp.concatenate([c, c_ctx[None], jnp.zeros((16 - bsz - 1, D_MODEL), F32)], axis=0)
    mod = _modulation(cvec, w_mod, b_mod)
    pad = jnp.zeros((bsz, MOD_ROWS - 6, D_MODEL), F32)

    cos_t, sin_t = _rope_tables(n_lat)
    cos_c, sin_c = cos_t[:n_ctx], sin_t[:n_ctx]
    e_mat = _head_sum_matrix()
    wc = _channel_dft()
    dft_lat = _dft_tables(n_lat)
    dft_ctx = _dft_tables(n_ctx)
    fnorm = final_norm.reshape(1, D_MODEL)

    lat, cx = x, ctx
    for l in range(DEPTH):
        last = l == DEPTH - 1
        m_lat = jnp.concatenate([mod[l, :bsz].reshape(bsz, 6, D_MODEL), pad], axis=1)
        m_ctx = jnp.concatenate([mod[l, bsz:bsz + 1].reshape(1, 6, D_MODEL), pad[:1]], axis=1)
        n1 = norm1[l].reshape(1, D_MODEL)
        n2 = norm2[l].reshape(1, D_MODEL)
        w_a = w_in[l, :, :OFF_G].astype(BF16)
        w_g = w_in[l, :, OFF_G:].astype(BF16)
        q_gain = jnp.tile(qk_gain[l, 0], N_HEADS) * (HEAD_DIM ** -0.5)
        k_gain = jnp.tile(qk_gain[l, 1], N_KV_HEADS)
        gain = jnp.concatenate([q_gain, k_gain]).reshape(1, QK_WIDTH)
        pm = _block_diag(pool_mat[l]).astype(BF16)
        ps = pool_scale[l].reshape(1, POOL_WIDTH)
        local = (e_mat, gain)
        mixer_w = (w_g, w_br_attn[l].astype(BF16), w_br_sc[l].astype(BF16), w_br_f[l].astype(BF16),
                   w_br_p[l].astype(BF16), w_out[l].astype(BF16))
        wu = w_up[l].astype(BF16).reshape(D_MODEL, 2 * N_FF_CHUNKS, FF_CHUNK).transpose(1, 0, 2)
        cw = w_conv_ffn[l].reshape(3, 2 * N_FF_CHUNKS, FF_CHUNK).transpose(1, 0, 2)
        wd = w_down[l].astype(BF16).reshape(N_FF_CHUNKS, FF_CHUNK, D_MODEL)

        qc, kc, vc, scc, xfc, ypc = _in_proj(cx, m_ctx, n1, w_a, *local, cos_c, sin_c, conv_sc[l], pm, ps,
                                             tq=tq_ctx, rope=False)
        ql, kl, vl, scl, xfl, ypl = _in_proj(lat, m_lat, n1, w_a, *local, cos_t, sin_t, conv_sc[l], pm, ps,
                                             tq=tq_lat, rope=True)
        oa = _attention(ql, jnp.concatenate([kl, kc], axis=1), jnp.concatenate([vl, vc], axis=1), tq=tq_attn)
        xfour = _fourier(xfl, wc, *dft_lat)
        lat = _mix(lat, m_lat, n1, oa, scl, xfour, ypl, *mixer_w, tq=tq_lat)
        lat = _ffn(lat, m_lat, n2, wu, cw, wd, fnorm, tq=tq_ffn_lat, final=last)

        if not last:
            oac = _attention(qc, kc, vc, tq=tq_ctx)
            xfourc = _fourier(xfc, wc, *dft_ctx)
            cx = _mix(cx, m_ctx, n1, oac, scc, xfourc, ypc, *mixer_w, tq=tq_ctx)
            cx = _ffn(cx, m_ctx, n2, wu, cw, wd, fnorm, tq=tq_ffn_ctx, final=False)
    return lat
```

```python
import functools
import math

import numpy as np
import jax
import jax.numpy as jnp
from jax import lax
from jax.experimental import pallas as pl
from jax.experimental.pallas import tpu as pltpu

D_MODEL = 1024
DEPTH = 2
GRID_W = 64
NORM_EPS = 1e-6
N_HEADS = 8
N_KV_HEADS = 2
HEAD_DIM = 64
Q_WIDTH = N_HEADS * HEAD_DIM
KV_WIDTH = N_KV_HEADS * HEAD_DIM
QK_WIDTH = Q_WIDTH + KV_WIDTH
ROPE_THETA = 10000.0
AXIS_PAIRS = HEAD_DIM // 4
SC_WIDTH = 256
FOURIER_GROUPS = 4
FOURIER_WIDTH = 256
FOURIER_GROUP = FOURIER_WIDTH // FOURIER_GROUPS
POOL_WINDOWS = (2, 4, 8, 16)
POOL_WIDTH = 256
POOL_GROUP = POOL_WIDTH // len(POOL_WINDOWS)
N_BRANCHES = 4
D_FF = 2816

OFF_K = Q_WIDTH
OFF_V = OFF_K + KV_WIDTH
OFF_SC = OFF_V + KV_WIDTH
OFF_F = OFF_SC + 3 * SC_WIDTH
OFF_P = OFF_F + FOURIER_WIDTH
OFF_G = OFF_P + POOL_WIDTH
IN_WIDTH = OFF_G + N_BRANCHES * D_MODEL

LANES = 128
HALO = 8
FF_CHUNK = 256
N_FF_CHUNKS = D_FF // FF_CHUNK
ATTN_SUB = 256
TOKEN_TILE = 512
ATTN_TILE = 1024
FFN_TILE = 256
N_MOD = 6
VMEM_LIMIT = 62 * 1024 * 1024

F32 = jnp.float32
BF16 = jnp.bfloat16


def _dot(a, b):
    return jnp.dot(a, b, preferred_element_type=F32)


def _dot_nt(a, b):
    return lax.dot_general(a, b, (((1,), (1,)), ((), ())), preferred_element_type=F32)


def _fixed_spec(block, index):
    return pl.BlockSpec(block, lambda *_: index, pipeline_mode=pl.Buffered(1))


def _layer_spec(tail, layer, tail_index=None):
    tail_index = (0,) * len(tail) if tail_index is None else tail_index
    return _fixed_spec((None,) + tuple(tail), (layer,) + tuple(tail_index))


def _mod_spec(layer, row):
    if row is None:
        return pl.BlockSpec((None, None, N_MOD, D_MODEL), lambda i, b: (layer, b, 0, 0))
    return _fixed_spec((None, None, N_MOD, D_MODEL), (layer, row, 0, 0))


def _params():
    return pltpu.CompilerParams(dimension_semantics=("arbitrary", "arbitrary"),
                                vmem_limit_bytes=VMEM_LIMIT)


def _rms_mod(xv, norm, shift, scale):
    ms = jnp.mean(xv * xv, axis=-1, keepdims=True)
    y = xv * lax.rsqrt(ms + NORM_EPS) * norm
    return y * (1.0 + scale) + shift


def _halo_specs(n, tq, width):
    nblk = n // HALO
    per = tq // HALO
    prev = pl.BlockSpec((1, HALO, width), lambda i, b: (b, jnp.maximum(i * per - 1, 0), 0))
    cur = pl.BlockSpec((1, tq, width), lambda i, b: (b, i, 0))
    nxt = pl.BlockSpec((1, HALO, width), lambda i, b: (b, jnp.minimum((i + 1) * per, nblk - 1), 0))
    return [prev, cur, nxt]


def _fill_halo_rows(h_scr, xp_ref, xc_ref, xn_ref, fn, tq):
    i = pl.program_id(0)
    last = pl.num_programs(0) - 1
    h_scr[HALO:HALO + tq, :] = fn(xc_ref[0])
    h_scr[0:HALO, :] = jnp.where(i > 0, fn(xp_ref[0]), 0.0)
    h_scr[HALO + tq:2 * HALO + tq, :] = jnp.where(i < last, fn(xn_ref[0]), 0.0)


def _head_rms(z, e_ref, gain):
    sq = z * z
    sq_hi = sq.astype(BF16)
    sq_lo = (sq - sq_hi.astype(F32)).astype(BF16)
    ss = _dot(sq_hi, e_ref[...]) + _dot(sq_lo, e_ref[...])
    return z * lax.rsqrt(ss * (1.0 / HEAD_DIM) + NORM_EPS) * gain


def _dup_heads(a, low_half):
    sw = pltpu.roll(a, HEAD_DIM, 1)
    return jnp.where(low_half, a, sw), jnp.where(low_half, sw, a)


def _mod_kernel(c_ref, w_ref, b_ref, o_ref):
    cv = c_ref[...]
    a = (cv * jax.nn.sigmoid(cv)).astype(BF16)
    o_ref[0] = _dot(a, w_ref[0].astype(BF16)) + b_ref[0]


def _modulation(cvec, w_mod, b_mod):
    rows = cvec.shape[0]
    nl = w_mod.shape[0]
    out = pl.pallas_call(
        _mod_kernel,
        out_shape=jax.ShapeDtypeStruct((nl, rows, N_MOD * D_MODEL), F32),
        grid=(nl, N_MOD),
        in_specs=[pl.BlockSpec((rows, D_MODEL), lambda l, j: (0, 0)),
                  pl.BlockSpec((1, D_MODEL, D_MODEL), lambda l, j: (l, 0, j)),
                  pl.BlockSpec((1, 1, D_MODEL), lambda l, j: (l, 0, j))],
        out_specs=pl.BlockSpec((1, rows, D_MODEL), lambda l, j: (l, 0, j)),
        compiler_params=_params(),
        name="adaln_mod",
    )(cvec, w_mod, b_mod.reshape(nl, 1, N_MOD * D_MODEL))
    return out.reshape(nl, rows, N_MOD, D_MODEL)


def _in_proj_kernel(xp_ref, xc_ref, xn_ref, m_ref, norm_ref, w_ref, e_ref, gain_ref, cos_ref, sin_ref,
                    conv_ref, pm_ref, ps_ref,
                    q_ref, k_ref, v_ref, sc_ref, xf_ref, yp_ref,
                    h_scr, z_scr, *, n, tq, rope):
    i = pl.program_id(0)
    norm = norm_ref[...]
    shift = m_ref[0:1, :]
    scale = m_ref[1:2, :]
    _fill_halo_rows(h_scr, xp_ref, xc_ref, xn_ref,
                    lambda xv: _rms_mod(xv, norm, shift, scale), tq)
    z_scr[...] = _dot(h_scr[...].astype(BF16), w_ref[...])

    lane = lax.broadcasted_iota(jnp.int32, (tq, LANES), 1)
    low_half = lane < HEAD_DIM
    rows = slice(HALO, HALO + tq)

    y = _head_rms(z_scr[rows, 0:QK_WIDTH], e_ref, gain_ref[...])
    second_half = (lane & AXIS_PAIRS) != 0
    for c in range(QK_WIDTH // LANES):
        yc = y[:, c * LANES:(c + 1) * LANES]
        if rope:
            partner = jnp.where(second_half, pltpu.roll(yc, AXIS_PAIRS, 1),
                                pltpu.roll(yc, LANES - AXIS_PAIRS, 1))
            yc = yc * cos_ref[...] + partner * sin_ref[...]
        if c < Q_WIDTH // LANES:
            q_ref[0, :, c * LANES:(c + 1) * LANES] = yc.astype(BF16)
        else:
            k0, k1 = _dup_heads(yc, low_half)
            k_ref[0, :, 0:LANES] = k0.astype(BF16)
            k_ref[0, :, LANES:2 * LANES] = k1.astype(BF16)
    v0, v1 = _dup_heads(z_scr[rows, OFF_V:OFF_SC], low_half)
    v_ref[0, :, 0:LANES] = v0.astype(BF16)
    v_ref[0, :, LANES:2 * LANES] = v1.astype(BF16)

    def u_at(off):
        r = slice(HALO + off, HALO + off + tq)
        return z_scr[r, OFF_SC + SC_WIDTH:OFF_SC + 2 * SC_WIDTH] * z_scr[r, OFF_SC + 2 * SC_WIDTH:OFF_F]

    conv = u_at(-1) * conv_ref[0:1, :] + u_at(0) * conv_ref[1:2, :] + u_at(1) * conv_ref[2:3, :]
    sc_ref[0] = (z_scr[rows, OFF_SC:OFF_SC + SC_WIDTH] * conv).astype(BF16)

    xf_ref[0] = z_scr[rows, OFF_F:OFF_P].astype(BF16)

    t = i * tq + lax.broadcasted_iota(jnp.int32, (tq, LANES), 0)

    def p_at(off, c):
        return z_scr[HALO + off:HALO + off + tq, OFF_P + c * LANES:OFF_P + (c + 1) * LANES]

    def window_mean(total, w_low, w_high):
        w = jnp.where(low_half, w_low, w_high)
        left, right = (w - 1) // 2, w // 2
        cnt = jnp.minimum(t + right + 1, n) - jnp.maximum(t - left, 0)
        return total / cnt.astype(F32)

    p0 = p_at(0, 0)
    s2 = p0 + p_at(1, 0)
    s4 = s2 + p_at(-1, 0) + p_at(2, 0)
    pooled0 = window_mean(jnp.where(low_half, s2, s4), POOL_WINDOWS[0], POOL_WINDOWS[1]) - p0
    p1 = p_at(0, 1)
    s8 = p1
    for off in (-3, -2, -1, 1, 2, 3, 4):
        s8 = s8 + p_at(off, 1)
    s16 = s8
    for off in (-7, -6, -5, -4, 5, 6, 7, 8):
        s16 = s16 + p_at(off, 1)
    pooled1 = window_mean(jnp.where(low_half, s8, s16), POOL_WINDOWS[2], POOL_WINDOWS[3]) - p1
    pooled = jnp.concatenate([pooled0, pooled1], axis=1).astype(BF16)
    yp_ref[0] = (_dot(pooled, pm_ref[...]) * ps_ref[...]).astype(BF16)


def _in_proj(xs, prm, layer, mod_row, cos_t, sin_t, *, tq, rope):
    b, n, _ = xs.shape
    out_w = (Q_WIDTH, 2 * LANES, 2 * LANES, SC_WIDTH, FOURIER_WIDTH, POOL_WIDTH)
    kern = functools.partial(_in_proj_kernel, n=n, tq=tq, rope=rope)
    return pl.pallas_call(
        kern,
        out_shape=[jax.ShapeDtypeStruct((b, n, w), BF16) for w in out_w],
        grid=(n // tq, b),
        in_specs=_halo_specs(n, tq, D_MODEL) + [
            _mod_spec(layer, mod_row),
            _layer_spec((1, D_MODEL), layer),
            _layer_spec((D_MODEL, OFF_G), layer),
            _fixed_spec((QK_WIDTH, QK_WIDTH), (0, 0)),
            _layer_spec((1, QK_WIDTH), layer),
            pl.BlockSpec((tq, LANES), lambda i, bb: (i, 0)),
            pl.BlockSpec((tq, LANES), lambda i, bb: (i, 0)),
            _layer_spec((3, SC_WIDTH), layer),
            _layer_spec((POOL_WIDTH, POOL_WIDTH), layer),
            _layer_spec((1, POOL_WIDTH), layer),
        ],
        out_specs=[pl.BlockSpec((1, tq, w), lambda i, bb: (bb, i, 0)) for w in out_w],
        scratch_shapes=[pltpu.VMEM((tq + 2 * HALO, D_MODEL), F32),
                        pltpu.VMEM((tq + 2 * HALO, OFF_G), F32)],
        compiler_params=_params(),
        name="in_proj",
    )(xs, xs, xs, prm["mod"], prm["norm1"], prm["w_in"], prm["e_qk"], prm["gain"], cos_t, sin_t,
      prm["conv_sc"], prm["pool_mat"], prm["pool_scale"])


def _kv_proj_kernel(x_ref, m_ref, norm_ref, w_ref, e_ref, gain_ref, k_ref, v_ref, *, tq):
    h = _rms_mod(x_ref[0], norm_ref[...], m_ref[0:1, :], m_ref[1:2, :]).astype(BF16)
    z = _dot(h, w_ref[...])
    low_half = lax.broadcasted_iota(jnp.int32, (tq, LANES), 1) < HEAD_DIM
    k0, k1 = _dup_heads(_head_rms(z[:, 0:KV_WIDTH], e_ref, gain_ref[...]), low_half)
    k_ref[0, :, 0:LANES] = k0.astype(BF16)
    k_ref[0, :, LANES:2 * LANES] = k1.astype(BF16)
    v0, v1 = _dup_heads(z[:, KV_WIDTH:2 * KV_WIDTH], low_half)
    v_ref[0, :, 0:LANES] = v0.astype(BF16)
    v_ref[0, :, LANES:2 * LANES] = v1.astype(BF16)


def _kv_proj(xs, prm, layer, mod_row, *, tq):
    b, n, _ = xs.shape
    kv_cols = 2 * KV_WIDTH
    return pl.pallas_call(
        functools.partial(_kv_proj_kernel, tq=tq),
        out_shape=[jax.ShapeDtypeStruct((b, n, 2 * LANES), BF16)] * 2,
        grid=(n // tq, b),
        in_specs=[pl.BlockSpec((1, tq, D_MODEL), lambda i, bb: (bb, i, 0)),
                  _mod_spec(layer, mod_row),
                  _layer_spec((1, D_MODEL), layer),
                  _layer_spec((D_MODEL, kv_cols), layer, (0, OFF_K // kv_cols)),
                  _fixed_spec((KV_WIDTH, KV_WIDTH), (0, 0)),
                  _layer_spec((1, KV_WIDTH), layer, (0, Q_WIDTH // KV_WIDTH))],
        out_specs=[pl.BlockSpec((1, tq, 2 * LANES), lambda i, bb: (bb, i, 0))] * 2,
        compiler_params=_params(),
        name="kv_proj",
    )(xs, prm["mod"], prm["norm1"], prm["w_in"], prm["e_qk"], prm["gain"])


def _attn_kernel(q_ref, *refs, tq):
    *kv_refs, o_ref = refs
    sub = min(tq, ATTN_SUB)
    lane = lax.broadcasted_iota(jnp.int32, (sub, LANES), 1)
    low_half = lane < HEAD_DIM
    sources = [(kv_refs[2 * a], kv_refs[2 * a + 1]) for a in range(len(kv_refs) // 2)]
    for j in range(Q_WIDTH // LANES):
        g = (2 * j) // (N_HEADS // N_KV_HEADS)
        gl = slice(g * LANES, (g + 1) * LANES)
        for r in range(tq // sub):
            rows = slice(r * sub, (r + 1) * sub)
            qp = q_ref[0, rows, j * LANES:(j + 1) * LANES]
            zero = jnp.zeros_like(qp)
            q2 = jnp.concatenate([jnp.where(low_half, qp, zero), jnp.where(low_half, zero, qp)], axis=0)
            scores = [_dot_nt(q2, k_ref[0, :, gl]) for k_ref, _ in sources]
            row_max = functools.reduce(jnp.maximum, [jnp.max(s, axis=-1, keepdims=True) for s in scores])
            o, denom = None, None
            for s, (_, v_ref) in zip(scores, sources):
                p = jnp.exp(s - row_max)
                part = jnp.sum(p, axis=-1, keepdims=True)
                term = _dot(p.astype(BF16), v_ref[0, :, gl])
                o, denom = (term, part) if o is None else (o + term, denom + part)
            o = o / denom
            o_ref[0, rows, j * LANES:(j + 1) * LANES] = jnp.where(low_half, o[:sub], o[sub:]).astype(BF16)


def _attention(q, kv_sources, *, tq):
    b, n, _ = q.shape
    kv_specs, kv_args = [], []
    for k, v in kv_sources:
        for a in (k, v):
            kv_specs.append(pl.BlockSpec((1, a.shape[1], 2 * LANES), lambda bb, i: (bb, 0, 0)))
            kv_args.append(a)
    return pl.pallas_call(
        functools.partial(_attn_kernel, tq=tq),
        out_shape=jax.ShapeDtypeStruct((b, n, Q_WIDTH), BF16),
        grid=(b, n // tq),
        in_specs=[pl.BlockSpec((1, tq, Q_WIDTH), lambda bb, i: (bb, i, 0))] + kv_specs,
        out_specs=pl.BlockSpec((1, tq, Q_WIDTH), lambda bb, i: (bb, i, 0)),
        compiler_params=_params(),
        name="attention",
    )(q, *kv_args)


def _fourier_kernel(x_ref, wc_ref, cn_ref, nsn_ref, o_ref, t_scr, *, n, tr):
    t_scr[...] = _dot(x_ref[0], wc_ref[...]).astype(BF16)
    scale = 1.0 / math.sqrt(n * FOURIER_GROUP)
    for r in range(n // tr):
        rows = slice(r * tr, (r + 1) * tr)
        acc = (_dot(cn_ref[rows, :], t_scr[:, 0:FOURIER_WIDTH])
               + _dot(nsn_ref[rows, :], t_scr[:, FOURIER_WIDTH:2 * FOURIER_WIDTH]))
        o_ref[0, rows, :] = (acc * scale).astype(BF16)


def _fourier(xf, wc, cn, nsn):
    b, n, _ = xf.shape
    tr = min(n, 512)
    return pl.pallas_call(
        functools.partial(_fourier_kernel, n=n, tr=tr),
        out_shape=jax.ShapeDtypeStruct((b, n, FOURIER_WIDTH), BF16),
        grid=(b, 1),
        in_specs=[pl.BlockSpec((1, n, FOURIER_WIDTH), lambda bb, i: (bb, 0, 0)),
                  _fixed_spec((FOURIER_WIDTH, 2 * FOURIER_WIDTH), (0, 0)),
                  _fixed_spec((n, n), (0, 0)),
                  _fixed_spec((n, n), (0, 0))],
        out_specs=pl.BlockSpec((1, n, FOURIER_WIDTH), lambda bb, i: (bb, 0, 0)),
        scratch_shapes=[pltpu.VMEM((n, 2 * FOURIER_WIDTH), BF16)],
        compiler_params=_params(),
        name="fourier",
    )(xf, wc, cn, nsn)


def _mix_kernel(x_ref, m_ref, norm_ref, oa_ref, sc_ref, xf_ref, yp_ref,
                wg01_ref, wg23_ref, wa_ref, wsc_ref, wf_ref, wp_ref, wo_ref, out_ref, y_scr):
    xv = x_ref[0]
    h = _rms_mod(xv, norm_ref[...], m_ref[0:1, :], m_ref[1:2, :]).astype(BF16)
    branches = ((oa_ref, wa_ref), (sc_ref, wsc_ref), (xf_ref, wf_ref), (yp_ref, wp_ref))
    gate_w = (wg01_ref, wg23_ref)
    for j in range(D_MODEL // FF_CHUNK):
        cols = slice(j * FF_CHUNK, (j + 1) * FF_CHUNK)
        y = None
        for bi, (a_ref, w_ref) in enumerate(branches):
            off = (bi % 2) * D_MODEL + j * FF_CHUNK
            gate = jax.nn.sigmoid(_dot(h, gate_w[bi // 2][:, off:off + FF_CHUNK]))
            term = gate * _dot(a_ref[0], w_ref[:, cols])
            y = term if y is None else y + term
        y_scr[:, cols] = y.astype(BF16)
    out_ref[0] = xv + m_ref[2:3, :] * _dot(y_scr[...], wo_ref[...])


def _mix(xs, prm, layer, mod_row, oa, sc, xf, yp, *, tq):
    b, n, _ = xs.shape
    half_gates = N_BRANCHES * D_MODEL // 2

    def tok(w):
        return pl.BlockSpec((1, tq, w), lambda i, bb: (bb, i, 0))

    return pl.pallas_call(
        _mix_kernel,
        out_shape=jax.ShapeDtypeStruct((b, n, D_MODEL), F32),
        grid=(n // tq, b),
        in_specs=[tok(D_MODEL), _mod_spec(layer, mod_row), _layer_spec((1, D_MODEL), layer),
                  tok(Q_WIDTH), tok(SC_WIDTH), tok(FOURIER_WIDTH), tok(POOL_WIDTH),
                  _layer_spec((D_MODEL, half_gates), layer, (0, OFF_G // half_gates)),
                  _layer_spec((D_MODEL, half_gates), layer, (0, OFF_G // half_gates + 1)),
                  _layer_spec((Q_WIDTH, D_MODEL), layer), _layer_spec((SC_WIDTH, D_MODEL), layer),
                  _layer_spec((FOURIER_WIDTH, D_MODEL), layer), _layer_spec((POOL_WIDTH, D_MODEL), layer),
                  _layer_spec((D_MODEL, D_MODEL), layer)],
        out_specs=tok(D_MODEL),
        scratch_shapes=[pltpu.VMEM((tq, D_MODEL), BF16)],
        compiler_params=_params(),
        name="mix",
    )(xs, prm["mod"], prm["norm1"], oa, sc, xf, yp, prm["w_in"], prm["w_in"],
      prm["w_br_attn"], prm["w_br_sc"], prm["w_br_f"], prm["w_br_p"], prm["w_out"])


def _ffn_kernel(xp_ref, xc_ref, xn_ref, m_ref, norm_ref, wu_ref, cw_ref, wd_ref, fn_ref, out_ref,
                h_scr, hb_scr, u0_scr, u1_scr, acc_scr, *, tq, final):
    norm = norm_ref[...]
    shift = m_ref[3:4, :]
    scale = m_ref[4:5, :]
    _fill_halo_rows(h_scr, xp_ref, xc_ref, xn_ref,
                    lambda xv: _rms_mod(xv, norm, shift, scale), tq)
    hb_scr[...] = h_scr[...].astype(BF16)
    acc_scr[...] = jnp.zeros_like(acc_scr)
    ext = tq + 2 * HALO

    def cols(c, half):
        return slice(half * D_FF + c * FF_CHUNK, half * D_FF + (c + 1) * FF_CHUNK)

    def up(c, u_scr):
        hb = hb_scr[...]
        u_scr[0] = _dot(hb, wu_ref[:, cols(c, 0)])
        u_scr[1] = _dot(hb, wu_ref[:, cols(c, 1)])

    def conv3(u, w):
        full = pltpu.roll(u, 1, 0) * w[0:1, :] + u * w[1:2, :] + pltpu.roll(u, ext - 1, 0) * w[2:3, :]
        return full[HALO:HALO + tq]

    def down(c, u_scr):
        a = conv3(u_scr[0], cw_ref[:, cols(c, 0)])
        bv = conv3(u_scr[1], cw_ref[:, cols(c, 1)])
        act = (a * jax.nn.sigmoid(a) * bv).astype(BF16)
        acc_scr[...] += _dot(act, wd_ref[c * FF_CHUNK:(c + 1) * FF_CHUNK, :])

    bufs = (u0_scr, u1_scr)
    up(0, u0_scr)
    for c in range(N_FF_CHUNKS):
        if c + 1 < N_FF_CHUNKS:
            up(c + 1, bufs[(c + 1) % 2])
        down(c, bufs[c % 2])
    out = xc_ref[0] + m_ref[5:6, :] * acc_scr[...]
    if final:
        ms = jnp.mean(out * out, axis=-1, keepdims=True)
        out = out * lax.rsqrt(ms + NORM_EPS) * fn_ref[...]
    out_ref[0] = out


def _ffn(xs, prm, layer, mod_row, *, tq, final):
    b, n, _ = xs.shape
    return pl.pallas_call(
        functools.partial(_ffn_kernel, tq=tq, final=final),
        out_shape=jax.ShapeDtypeStruct((b, n, D_MODEL), F32),
        grid=(n // tq, b),
        in_specs=_halo_specs(n, tq, D_MODEL) + [
            _mod_spec(layer, mod_row),
            _layer_spec((1, D_MODEL), layer),
            _layer_spec((D_MODEL, 2 * D_FF), layer),
            _layer_spec((3, 2 * D_FF), layer),
            _layer_spec((D_FF, D_MODEL), layer),
            _fixed_spec((1, D_MODEL), (0, 0))],
        out_specs=pl.BlockSpec((1, tq, D_MODEL), lambda i, bb: (bb, i, 0)),
        scratch_shapes=[pltpu.VMEM((tq + 2 * HALO, D_MODEL), F32),
                        pltpu.VMEM((tq + 2 * HALO, D_MODEL), BF16),
                        pltpu.VMEM((2, tq + 2 * HALO, FF_CHUNK), F32),
                        pltpu.VMEM((2, tq + 2 * HALO, FF_CHUNK), F32),
                        pltpu.VMEM((tq, D_MODEL), F32)],
        compiler_params=_params(),
        name="ffn",
    )(xs, xs, xs, prm["mod"], prm["norm2"], prm["w_up"], prm["w_conv_ffn"], prm["w_down"], prm["final_norm"])


def _rope_tables(n):
    d = np.arange(LANES)
    hd = d % HEAD_DIM
    axis = hd // (2 * AXIS_PAIRS)
    half = (hd % (2 * AXIS_PAIRS)) // AXIS_PAIRS
    pair = hd % AXIS_PAIRS
    t = np.arange(n)
    pos = np.where(axis[None, :] == 0, (t // GRID_W)[:, None], (t % GRID_W)[:, None]).astype(np.float64)
    inv = ROPE_THETA ** (-np.arange(AXIS_PAIRS, dtype=np.float64) / AXIS_PAIRS)
    ang = pos * inv[pair][None, :]
    sin = np.sin(ang)
    return (jnp.asarray(np.cos(ang), dtype=F32),
            jnp.asarray(np.where(half[None, :] == 0, -sin, sin), dtype=F32))


def _dft_tables(n):
    k = np.arange(n, dtype=np.int64)
    ang = ((k[:, None] * k[None, :]) % n).astype(np.float64) * (2.0 * math.pi / n)
    return (jnp.asarray(np.cos(ang).astype(np.float32).astype(BF16)),
            jnp.asarray((-np.sin(ang)).astype(np.float32).astype(BF16)))


def _channel_dft():
    c = np.arange(FOURIER_WIDTH, dtype=np.int64)
    same = (c[:, None] // FOURIER_GROUP) == (c[None, :] // FOURIER_GROUP)
    mc = ((c[:, None] % FOURIER_GROUP) * (c[None, :] % FOURIER_GROUP)) % FOURIER_GROUP
    ang = mc.astype(np.float64) * (2.0 * math.pi / FOURIER_GROUP)
    both = np.concatenate([np.where(same, np.cos(ang), 0.0), np.where(same, np.sin(ang), 0.0)], axis=1)
    return jnp.asarray(both.astype(np.float32).astype(BF16))


def _head_sum_matrix():
    r = np.arange(QK_WIDTH) // HEAD_DIM
    return jnp.asarray((r[:, None] == r[None, :]).astype(np.float32).astype(BF16))


def _block_diag(mats):
    nl, g, a, b = mats.shape
    eye = jnp.eye(g, dtype=mats.dtype)
    return (mats[:, :, :, None, :] * eye[None, :, None, :, None]).reshape(nl, g * a, g * b)


def kernel(x, c, ctx, c_ctx, w_mod, b_mod, norm1, norm2, w_in, conv_sc, qk_gain, pool_mat, pool_scale,
           w_br_attn, w_br_sc, w_br_f, w_br_p, w_out, w_up, w_conv_ffn, w_down, final_norm):
    bsz, n_lat, _ = x.shape
    n_ctx = ctx.shape[1]
    nl = w_in.shape[0]
    tq_lat = min(n_lat, TOKEN_TILE)
    tq_ctx = min(n_ctx, TOKEN_TILE)
    tq_attn = min(n_lat, ATTN_TILE)
    tq_ffn_lat = min(n_lat, FFN_TILE)
    tq_ffn_ctx = min(n_ctx, FFN_TILE)

    mod_rows = 16
    ctx_row = bsz
    cvec = jnp.concatenate([c, c_ctx[None], jnp.zeros((mod_rows - bsz - 1, D_MODEL), F32)], axis=0)
    gain = jnp.concatenate([jnp.tile(qk_gain[:, 0], (1, N_HEADS)) * (HEAD_DIM ** -0.5),
                            jnp.tile(qk_gain[:, 1], (1, N_KV_HEADS))], axis=1)
    prm = {
        "mod": _modulation(cvec, w_mod, b_mod),
        "norm1": norm1.reshape(nl, 1, D_MODEL),
        "norm2": norm2.reshape(nl, 1, D_MODEL),
        "w_in": w_in.astype(BF16),
        "e_qk": _head_sum_matrix(),
        "gain": gain.reshape(nl, 1, QK_WIDTH),
        "conv_sc": conv_sc,
        "pool_mat": _block_diag(pool_mat).astype(BF16),
        "pool_scale": pool_scale.reshape(nl, 1, POOL_WIDTH),
        "w_br_attn": w_br_attn.astype(BF16),
        "w_br_sc": w_br_sc.astype(BF16),
        "w_br_f": w_br_f.astype(BF16),
        "w_br_p": w_br_p.astype(BF16),
        "w_out": w_out.astype(BF16),
        "w_up": w_up.astype(BF16),
        "w_conv_ffn": w_conv_ffn,
        "w_down": w_down.astype(BF16),
        "final_norm": final_norm.reshape(1, D_MODEL),
    }
    cos_t, sin_t = _rope_tables(n_lat)
    wc = _channel_dft()
    dft_lat = _dft_tables(n_lat)
    dft_ctx = _dft_tables(n_ctx)

    lat, cx = x, ctx
    for l in range(nl):
        last = l == nl - 1
        if last:
            kc, vc = _kv_proj(cx, prm, l, ctx_row, tq=tq_ctx)
        else:
            qc, kc, vc, scc, xfc, ypc = _in_proj(cx, prm, l, ctx_row, cos_t, sin_t, tq=tq_ctx, rope=False)
        ql, kl, vl, scl, xfl, ypl = _in_proj(lat, prm, l, None, cos_t, sin_t, tq=tq_lat, rope=True)
        oa = _attention(ql, [(kl, vl), (kc, vc)], tq=tq_attn)
        xfour = _fourier(xfl, wc, *dft_lat)
        lat = _mix(lat, prm, l, None, oa, scl, xfour, ypl, tq=tq_lat)
        lat = _ffn(lat, prm, l, None, tq=tq_ffn_lat, final=last)

        if not last:
            oac = _attention(qc, [(kc, vc)], tq=tq_ctx)
            xfourc = _fourier(xfc, wc, *dft_ctx)
            cx = _mix(cx, prm, l, ctx_row, oac, scc, xfourc, ypc, tq=tq_ctx)
            cx = _ffn(cx, prm, l, ctx_row, tq=tq_ffn_ctx, final=False)
    return lat
```

```python
import functools
import math

import numpy as np
import jax
import jax.numpy as jnp
from jax import lax
from jax.experimental import pallas as pl
from jax.experimental.pallas import tpu as pltpu

D_MODEL = 1024
DEPTH = 2
GRID_W = 64
NORM_EPS = 1e-6
N_HEADS = 8
N_KV_HEADS = 2
HEAD_DIM = 64
Q_WIDTH = N_HEADS * HEAD_DIM
KV_WIDTH = N_KV_HEADS * HEAD_DIM
QK_WIDTH = Q_WIDTH + KV_WIDTH
ROPE_THETA = 10000.0
AXIS_PAIRS = HEAD_DIM // 4
SC_WIDTH = 256
FOURIER_GROUPS = 4
FOURIER_WIDTH = 256
FOURIER_GROUP = FOURIER_WIDTH // FOURIER_GROUPS
POOL_WINDOWS = (2, 4, 8, 16)
POOL_WIDTH = 256
POOL_GROUP = POOL_WIDTH // len(POOL_WINDOWS)
N_BRANCHES = 4
D_FF = 2816

OFF_K = Q_WIDTH
OFF_V = OFF_K + KV_WIDTH
OFF_SC = OFF_V + KV_WIDTH
OFF_F = OFF_SC + 3 * SC_WIDTH
OFF_P = OFF_F + FOURIER_WIDTH
OFF_G = OFF_P + POOL_WIDTH
IN_WIDTH = OFF_G + N_BRANCHES * D_MODEL

LANES = 128
HALO = 8
FF_CHUNK = 256
N_FF_CHUNKS = D_FF // FF_CHUNK
ATTN_SUB = 256
TOKEN_TILE = 512
ATTN_TILE = 1024
FFN_TILE = 256
N_MOD = 6
VMEM_LIMIT = 62 * 1024 * 1024

F32 = jnp.float32
BF16 = jnp.bfloat16


def _dot(a, b):
    return jnp.dot(a, b, preferred_element_type=F32)


def _dot_nt(a, b):
    return lax.dot_general(a, b, (((1,), (1,)), ((), ())), preferred_element_type=F32)


def _fixed_spec(block, index):
    return pl.BlockSpec(block, lambda *_: index, pipeline_mode=pl.Buffered(1))


def _layer_spec(tail, layer, tail_index=None):
    tail_index = (0,) * len(tail) if tail_index is None else tail_index
    return _fixed_spec((None,) + tuple(tail), (layer,) + tuple(tail_index))


def _mod_spec(layer, row):
    if row is None:
        return pl.BlockSpec((None, None, N_MOD, D_MODEL), lambda i, b: (layer, b, 0, 0))
    return _fixed_spec((None, None, N_MOD, D_MODEL), (layer, row, 0, 0))


def _params():
    return pltpu.CompilerParams(dimension_semantics=("arbitrary", "arbitrary"),
                                vmem_limit_bytes=VMEM_LIMIT)


def _rms_mod(xv, norm, shift, scale):
    ms = jnp.mean(xv * xv, axis=-1, keepdims=True)
    y = xv * lax.rsqrt(ms + NORM_EPS) * norm
    return y * (1.0 + scale) + shift


def _halo_specs(n, tq, width):
    nblk = n // HALO
    per = tq // HALO
    prev = pl.BlockSpec((1, HALO, width), lambda i, b: (b, jnp.maximum(i * per - 1, 0), 0))
    cur = pl.BlockSpec((1, tq, width), lambda i, b: (b, i, 0))
    nxt = pl.BlockSpec((1, HALO, width), lambda i, b: (b, jnp.minimum((i + 1) * per, nblk - 1), 0))
    return [prev, cur, nxt]


def _fill_halo_rows(h_scr, xp_ref, xc_ref, xn_ref, fn, tq):
    i = pl.program_id(0)
    last = pl.num_programs(0) - 1
    h_scr[HALO:HALO + tq, :] = fn(xc_ref[0])
    h_scr[0:HALO, :] = jnp.where(i > 0, fn(xp_ref[0]), 0.0)
    h_scr[HALO + tq:2 * HALO + tq, :] = jnp.where(i < last, fn(xn_ref[0]), 0.0)


def _split_bf16(a):
    hi = a.astype(BF16)
    return hi, (a - hi.astype(F32)).astype(BF16)


def _head_rms(z, sum_ref, bcast_ref, gain):
    sq_hi, sq_lo = _split_bf16(z * z)
    ss = _dot(sq_hi, sum_ref[...]) + _dot(sq_lo, sum_ref[...])
    r_hi, r_lo = _split_bf16(lax.rsqrt(ss * (1.0 / HEAD_DIM) + NORM_EPS))
    return z * (_dot(r_hi, bcast_ref[...]) + _dot(r_lo, bcast_ref[...])) * gain


def _dup_heads(a, low_half):
    sw = pltpu.roll(a, HEAD_DIM, 1)
    return jnp.where(low_half, a, sw), jnp.where(low_half, sw, a)


def _mod_kernel(c_ref, w_ref, b_ref, o_ref):
    cv = c_ref[...]
    a = (cv * jax.nn.sigmoid(cv)).astype(BF16)
    o_ref[0] = _dot(a, w_ref[0].astype(BF16)) + b_ref[0]


def _modulation(cvec, w_mod, b_mod):
    rows = cvec.shape[0]
    nl = w_mod.shape[0]
    out = pl.pallas_call(
        _mod_kernel,
        out_shape=jax.ShapeDtypeStruct((nl, rows, N_MOD * D_MODEL), F32),
        grid=(nl, N_MOD),
        in_specs=[pl.BlockSpec((rows, D_MODEL), lambda l, j: (0, 0)),
                  pl.BlockSpec((1, D_MODEL, D_MODEL), lambda l, j: (l, 0, j)),
                  pl.BlockSpec((1, 1, D_MODEL), lambda l, j: (l, 0, j))],
        out_specs=pl.BlockSpec((1, rows, D_MODEL), lambda l, j: (l, 0, j)),
        compiler_params=_params(),
        name="adaln_mod",
    )(cvec, w_mod, b_mod.reshape(nl, 1, N_MOD * D_MODEL))
    return out.reshape(nl, rows, N_MOD, D_MODEL)


def _in_proj_kernel(xp_ref, xc_ref, xn_ref, m_ref, norm_ref, w_ref, hs_ref, hb_ref, gain_ref, cos_ref, sin_ref,
                    conv_ref, pm_ref, ps_ref,
                    q_ref, k_ref, v_ref, sc_ref, xf_ref, yp_ref,
                    h_scr, z_scr, *, n, tq, rope):
    i = pl.program_id(0)
    norm = norm_ref[...]
    shift = m_ref[0:1, :]
    scale = m_ref[1:2, :]
    _fill_halo_rows(h_scr, xp_ref, xc_ref, xn_ref,
                    lambda xv: _rms_mod(xv, norm, shift, scale), tq)
    z_scr[...] = _dot(h_scr[...].astype(BF16), w_ref[...])

    lane = lax.broadcasted_iota(jnp.int32, (tq, LANES), 1)
    low_half = lane < HEAD_DIM
    rows = slice(HALO, HALO + tq)

    y = _head_rms(z_scr[rows, 0:QK_WIDTH], hs_ref, hb_ref, gain_ref[...])
    second_half = (lane & AXIS_PAIRS) != 0
    for c in range(QK_WIDTH // LANES):
        yc = y[:, c * LANES:(c + 1) * LANES]
        if rope:
            partner = jnp.where(second_half, pltpu.roll(yc, AXIS_PAIRS, 1),
                                pltpu.roll(yc, LANES - AXIS_PAIRS, 1))
            yc = yc * cos_ref[...] + partner * sin_ref[...]
        if c < Q_WIDTH // LANES:
            q_ref[0, :, c * LANES:(c + 1) * LANES] = yc.astype(BF16)
        else:
            k0, k1 = _dup_heads(yc, low_half)
            k_ref[0, :, 0:LANES] = k0.astype(BF16)
            k_ref[0, :, LANES:2 * LANES] = k1.astype(BF16)
    v0, v1 = _dup_heads(z_scr[rows, OFF_V:OFF_SC], low_half)
    v_ref[0, :, 0:LANES] = v0.astype(BF16)
    v_ref[0, :, LANES:2 * LANES] = v1.astype(BF16)

    def u_at(off):
        r = slice(HALO + off, HALO + off + tq)
        return z_scr[r, OFF_SC + SC_WIDTH:OFF_SC + 2 * SC_WIDTH] * z_scr[r, OFF_SC + 2 * SC_WIDTH:OFF_F]

    conv = u_at(-1) * conv_ref[0:1, :] + u_at(0) * conv_ref[1:2, :] + u_at(1) * conv_ref[2:3, :]
    sc_ref[0] = (z_scr[rows, OFF_SC:OFF_SC + SC_WIDTH] * conv).astype(BF16)

    xf_ref[0] = z_scr[rows, OFF_F:OFF_P].astype(BF16)

    t = i * tq + lax.broadcasted_iota(jnp.int32, (tq, LANES), 0)

    def p_at(off, c):
        return z_scr[HALO + off:HALO + off + tq, OFF_P + c * LANES:OFF_P + (c + 1) * LANES]

    def window_mean(total, w_low, w_high):
        w = jnp.where(low_half, w_low, w_high)
        left, right = (w - 1) // 2, w // 2
        cnt = jnp.minimum(t + right + 1, n) - jnp.maximum(t - left, 0)
        return total / cnt.astype(F32)

    p0 = p_at(0, 0)
    s2 = p0 + p_at(1, 0)
    s4 = s2 + p_at(-1, 0) + p_at(2, 0)
    pooled0 = window_mean(jnp.where(low_half, s2, s4), POOL_WINDOWS[0], POOL_WINDOWS[1]) - p0
    p1 = p_at(0, 1)
    s8 = p1
    for off in (-3, -2, -1, 1, 2, 3, 4):
        s8 = s8 + p_at(off, 1)
    s16 = s8
    for off in (-7, -6, -5, -4, 5, 6, 7, 8):
        s16 = s16 + p_at(off, 1)
    pooled1 = window_mean(jnp.where(low_half, s8, s16), POOL_WINDOWS[2], POOL_WINDOWS[3]) - p1
    pooled = jnp.concatenate([pooled0, pooled1], axis=1).astype(BF16)
    yp_ref[0] = (_dot(pooled, pm_ref[...]) * ps_ref[...]).astype(BF16)


def _in_proj(xs, prm, layer, mod_row, cos_t, sin_t, *, tq, rope):
    b, n, _ = xs.shape
    out_w = (Q_WIDTH, 2 * LANES, 2 * LANES, SC_WIDTH, FOURIER_WIDTH, POOL_WIDTH)
    kern = functools.partial(_in_proj_kernel, n=n, tq=tq, rope=rope)
    return pl.pallas_call(
        kern,
        out_shape=[jax.ShapeDtypeStruct((b, n, w), BF16) for w in out_w],
        grid=(n // tq, b),
        in_specs=_halo_specs(n, tq, D_MODEL) + [
            _mod_spec(layer, mod_row),
            _layer_spec((1, D_MODEL), layer),
            _layer_spec((D_MODEL, OFF_G), layer),
            _fixed_spec((QK_WIDTH, LANES), (0, 0)),
            _fixed_spec((LANES, QK_WIDTH), (0, 0)),
            _layer_spec((1, QK_WIDTH), layer),
            pl.BlockSpec((tq, LANES), lambda i, bb: (i, 0)),
            pl.BlockSpec((tq, LANES), lambda i, bb: (i, 0)),
            _layer_spec((3, SC_WIDTH), layer),
            _layer_spec((POOL_WIDTH, POOL_WIDTH), layer),
            _layer_spec((1, POOL_WIDTH), layer),
        ],
        out_specs=[pl.BlockSpec((1, tq, w), lambda i, bb: (bb, i, 0)) for w in out_w],
        scratch_shapes=[pltpu.VMEM((tq + 2 * HALO, D_MODEL), F32),
                        pltpu.VMEM((tq + 2 * HALO, OFF_G), F32)],
        compiler_params=_params(),
        name="in_proj",
    )(xs, xs, xs, prm["mod"], prm["norm1"], prm["w_in"], prm["head_sum"], prm["head_bcast"], prm["gain"],
      cos_t, sin_t,
      prm["conv_sc"], prm["pool_mat"], prm["pool_scale"])


def _kv_proj_kernel(x_ref, m_ref, norm_ref, w_ref, hs_ref, hb_ref, gain_ref, k_ref, v_ref, *, tq):
    h = _rms_mod(x_ref[0], norm_ref[...], m_ref[0:1, :], m_ref[1:2, :]).astype(BF16)
    z = _dot(h, w_ref[...])
    low_half = lax.broadcasted_iota(jnp.int32, (tq, LANES), 1) < HEAD_DIM
    k0, k1 = _dup_heads(_head_rms(z[:, 0:KV_WIDTH], hs_ref, hb_ref, gain_ref[...]), low_half)
    k_ref[0, :, 0:LANES] = k0.astype(BF16)
    k_ref[0, :, LANES:2 * LANES] = k1.astype(BF16)
    v0, v1 = _dup_heads(z[:, KV_WIDTH:2 * KV_WIDTH], low_half)
    v_ref[0, :, 0:LANES] = v0.astype(BF16)
    v_ref[0, :, LANES:2 * LANES] = v1.astype(BF16)


def _kv_proj(xs, prm, layer, mod_row, *, tq):
    b, n, _ = xs.shape
    kv_cols = 2 * KV_WIDTH
    return pl.pallas_call(
        functools.partial(_kv_proj_kernel, tq=tq),
        out_shape=[jax.ShapeDtypeStruct((b, n, 2 * LANES), BF16)] * 2,
        grid=(n // tq, b),
        in_specs=[pl.BlockSpec((1, tq, D_MODEL), lambda i, bb: (bb, i, 0)),
                  _mod_spec(layer, mod_row),
                  _layer_spec((1, D_MODEL), layer),
                  _layer_spec((D_MODEL, kv_cols), layer, (0, OFF_K // kv_cols)),
                  _fixed_spec((KV_WIDTH, LANES), (0, 0)),
                  _fixed_spec((LANES, KV_WIDTH), (0, 0)),
                  _layer_spec((1, KV_WIDTH), layer, (0, Q_WIDTH // KV_WIDTH))],
        out_specs=[pl.BlockSpec((1, tq, 2 * LANES), lambda i, bb: (bb, i, 0))] * 2,
        compiler_params=_params(),
        name="kv_proj",
    )(xs, prm["mod"], prm["norm1"], prm["w_in"], prm["head_sum"], prm["head_bcast"], prm["gain"])


def _attn_kernel(q_ref, *refs, tq):
    *kv_refs, o_ref = refs
    sub = min(tq, ATTN_SUB)
    lane = lax.broadcasted_iota(jnp.int32, (sub, LANES), 1)
    low_half = lane < HEAD_DIM
    sources = [(kv_refs[2 * a], kv_refs[2 * a + 1]) for a in range(len(kv_refs) // 2)]
    for j in range(Q_WIDTH // LANES):
        g = (2 * j) // (N_HEADS // N_KV_HEADS)
        gl = slice(g * LANES, (g + 1) * LANES)
        for r in range(tq // sub):
            rows = slice(r * sub, (r + 1) * sub)
            qp = q_ref[0, rows, j * LANES:(j + 1) * LANES]
            zero = jnp.zeros_like(qp)
            q2 = jnp.concatenate([jnp.where(low_half, qp, zero), jnp.where(low_half, zero, qp)], axis=0)
            scores = [_dot_nt(q2, k_ref[0, :, gl]) for k_ref, _ in sources]
            row_max = functools.reduce(jnp.maximum, [jnp.max(s, axis=-1, keepdims=True) for s in scores])
            o, denom = None, None
            for s, (_, v_ref) in zip(scores, sources):
                p = jnp.exp(s - row_max)
                part = jnp.sum(p, axis=-1, keepdims=True)
                term = _dot(p.astype(BF16), v_ref[0, :, gl])
                o, denom = (term, part) if o is None else (o + term, denom + part)
            o = o / denom
            o_ref[0, rows, j * LANES:(j + 1) * LANES] = jnp.where(low_half, o[:sub], o[sub:]).astype(BF16)


def _attention(q, kv_sources, *, tq):
    b, n, _ = q.shape
    kv_specs, kv_args = [], []
    for k, v in kv_sources:
        for a in (k, v):
            kv_specs.append(pl.BlockSpec((1, a.shape[1], 2 * LANES), lambda bb, i: (bb, 0, 0)))
            kv_args.append(a)
    return pl.pallas_call(
        functools.partial(_attn_kernel, tq=tq),
        out_shape=jax.ShapeDtypeStruct((b, n, Q_WIDTH), BF16),
        grid=(b, n // tq),
        in_specs=[pl.BlockSpec((1, tq, Q_WIDTH), lambda bb, i: (bb, i, 0))] + kv_specs,
        out_specs=pl.BlockSpec((1, tq, Q_WIDTH), lambda bb, i: (bb, i, 0)),
        compiler_params=_params(),
        name="attention",
    )(q, *kv_args)


def _fourier_kernel(x_ref, wc_ref, cn_ref, nsn_ref, o_ref, t_scr, *, n, tr):
    t_scr[...] = _dot(x_ref[0], wc_ref[...]).astype(BF16)
    scale = 1.0 / math.sqrt(n * FOURIER_GROUP)
    for r in range(n // tr):
        rows = slice(r * tr, (r + 1) * tr)
        acc = (_dot(cn_ref[rows, :], t_scr[:, 0:FOURIER_WIDTH])
               + _dot(nsn_ref[rows, :], t_scr[:, FOURIER_WIDTH:2 * FOURIER_WIDTH]))
        o_ref[0, rows, :] = (acc * scale).astype(BF16)


def _fourier(xf, wc, cn, nsn):
    b, n, _ = xf.shape
    tr = min(n, 512)
    return pl.pallas_call(
        functools.partial(_fourier_kernel, n=n, tr=tr),
        out_shape=jax.ShapeDtypeStruct((b, n, FOURIER_WIDTH), BF16),
        grid=(b, 1),
        in_specs=[pl.BlockSpec((1, n, FOURIER_WIDTH), lambda bb, i: (bb, 0, 0)),
                  _fixed_spec((FOURIER_WIDTH, 2 * FOURIER_WIDTH), (0, 0)),
                  _fixed_spec((n, n), (0, 0)),
                  _fixed_spec((n, n), (0, 0))],
        out_specs=pl.BlockSpec((1, n, FOURIER_WIDTH), lambda bb, i: (bb, 0, 0)),
        scratch_shapes=[pltpu.VMEM((n, 2 * FOURIER_WIDTH), BF16)],
        compiler_params=_params(),
        name="fourier",
    )(xf, wc, cn, nsn)


def _mix_kernel(x_ref, m_ref, norm_ref, oa_ref, sc_ref, xf_ref, yp_ref,
                wg01_ref, wg23_ref, wa_ref, wsc_ref, wf_ref, wp_ref, wo_ref, out_ref, y_scr):
    xv = x_ref[0]
    h = _rms_mod(xv, norm_ref[...], m_ref[0:1, :], m_ref[1:2, :]).astype(BF16)
    branches = ((oa_ref, wa_ref), (sc_ref, wsc_ref), (xf_ref, wf_ref), (yp_ref, wp_ref))
    gate_w = (wg01_ref, wg23_ref)
    for j in range(D_MODEL // FF_CHUNK):
        cols = slice(j * FF_CHUNK, (j + 1) * FF_CHUNK)
        y = None
        for bi, (a_ref, w_ref) in enumerate(branches):
            off = (bi % 2) * D_MODEL + j * FF_CHUNK
            gate = jax.nn.sigmoid(_dot(h, gate_w[bi // 2][:, off:off + FF_CHUNK]))
            term = gate * _dot(a_ref[0], w_ref[:, cols])
            y = term if y is None else y + term
        y_scr[:, cols] = y.astype(BF16)
    out_ref[0] = xv + m_ref[2:3, :] * _dot(y_scr[...], wo_ref[...])


def _mix(xs, prm, layer, mod_row, oa, sc, xf, yp, *, tq):
    b, n, _ = xs.shape
    half_gates = N_BRANCHES * D_MODEL // 2

    def tok(w):
        return pl.BlockSpec((1, tq, w), lambda i, bb: (bb, i, 0))

    return pl.pallas_call(
        _mix_kernel,
        out_shape=jax.ShapeDtypeStruct((b, n, D_MODEL), F32),
        grid=(n // tq, b),
        in_specs=[tok(D_MODEL), _mod_spec(layer, mod_row), _layer_spec((1, D_MODEL), layer),
                  tok(Q_WIDTH), tok(SC_WIDTH), tok(FOURIER_WIDTH), tok(POOL_WIDTH),
                  _layer_spec((D_MODEL, half_gates), layer, (0, OFF_G // half_gates)),
                  _layer_spec((D_MODEL, half_gates), layer, (0, OFF_G // half_gates + 1)),
                  _layer_spec((Q_WIDTH, D_MODEL), layer), _layer_spec((SC_WIDTH, D_MODEL), layer),
                  _layer_spec((FOURIER_WIDTH, D_MODEL), layer), _layer_spec((POOL_WIDTH, D_MODEL), layer),
                  _layer_spec((D_MODEL, D_MODEL), layer)],
        out_specs=tok(D_MODEL),
        scratch_shapes=[pltpu.VMEM((tq, D_MODEL), BF16)],
        compiler_params=_params(),
        name="mix",
    )(xs, prm["mod"], prm["norm1"], oa, sc, xf, yp, prm["w_in"], prm["w_in"],
      prm["w_br_attn"], prm["w_br_sc"], prm["w_br_f"], prm["w_br_p"], prm["w_out"])


def _ffn_kernel(xp_ref, xc_ref, xn_ref, m_ref, norm_ref, wu_ref, cw_ref, wd_ref, fn_ref, out_ref,
                h_scr, hb_scr, u0_scr, u1_scr, acc_scr, *, tq, final):
    norm = norm_ref[...]
    shift = m_ref[3:4, :]
    scale = m_ref[4:5, :]
    _fill_halo_rows(h_scr, xp_ref, xc_ref, xn_ref,
                    lambda xv: _rms_mod(xv, norm, shift, scale), tq)
    hb_scr[...] = h_scr[...].astype(BF16)
    acc_scr[...] = jnp.zeros_like(acc_scr)
    ext = tq + 2 * HALO

    def cols(c, half):
        return slice(half * D_FF + c * FF_CHUNK, half * D_FF + (c + 1) * FF_CHUNK)

    def up(c, u_scr):
        hb = hb_scr[...]
        u_scr[0] = _dot(hb, wu_ref[:, cols(c, 0)])
        u_scr[1] = _dot(hb, wu_ref[:, cols(c, 1)])

    def conv3(u, w):
        full = pltpu.roll(u, 1, 0) * w[0:1, :] + u * w[1:2, :] + pltpu.roll(u, ext - 1, 0) * w[2:3, :]
        return full[HALO:HALO + tq]

    def down(c, u_scr):
        a = conv3(u_scr[0], cw_ref[:, cols(c, 0)])
        bv = conv3(u_scr[1], cw_ref[:, cols(c, 1)])
        act = (a * jax.nn.sigmoid(a) * bv).astype(BF16)
        acc_scr[...] += _dot(act, wd_ref[c * FF_CHUNK:(c + 1) * FF_CHUNK, :])

    bufs = (u0_scr, u1_scr)
    up(0, u0_scr)
    for c in range(N_FF_CHUNKS):
        if c + 1 < N_FF_CHUNKS:
            up(c + 1, bufs[(c + 1) % 2])
        down(c, bufs[c % 2])
    out = xc_ref[0] + m_ref[5:6, :] * acc_scr[...]
    if final:
        ms = jnp.mean(out * out, axis=-1, keepdims=True)
        out = out * lax.rsqrt(ms + NORM_EPS) * fn_ref[...]
    out_ref[0] = out


def _ffn(xs, prm, layer, mod_row, *, tq, final):
    b, n, _ = xs.shape
    return pl.pallas_call(
        functools.partial(_ffn_kernel, tq=tq, final=final),
        out_shape=jax.ShapeDtypeStruct((b, n, D_MODEL), F32),
        grid=(n // tq, b),
        in_specs=_halo_specs(n, tq, D_MODEL) + [
            _mod_spec(layer, mod_row),
            _layer_spec((1, D_MODEL), layer),
            _layer_spec((D_MODEL, 2 * D_FF), layer),
            _layer_spec((3, 2 * D_FF), layer),
            _layer_spec((D_FF, D_MODEL), layer),
            _fixed_spec((1, D_MODEL), (0, 0))],
        out_specs=pl.BlockSpec((1, tq, D_MODEL), lambda i, bb: (bb, i, 0)),
        scratch_shapes=[pltpu.VMEM((tq + 2 * HALO, D_MODEL), F32),
                        pltpu.VMEM((tq + 2 * HALO, D_MODEL), BF16),
                        pltpu.VMEM((2, tq + 2 * HALO, FF_CHUNK), F32),
                        pltpu.VMEM((2, tq + 2 * HALO, FF_CHUNK), F32),
                        pltpu.VMEM((tq, D_MODEL), F32)],
        compiler_params=_params(),
        name="ffn",
    )(xs, xs, xs, prm["mod"], prm["norm2"], prm["w_up"], prm["w_conv_ffn"], prm["w_down"], prm["final_norm"])


def _rope_tables(n):
    d = np.arange(LANES)
    hd = d % HEAD_DIM
    axis = hd // (2 * AXIS_PAIRS)
    half = (hd % (2 * AXIS_PAIRS)) // AXIS_PAIRS
    pair = hd % AXIS_PAIRS
    t = np.arange(n)
    pos = np.where(axis[None, :] == 0, (t // GRID_W)[:, None], (t % GRID_W)[:, None]).astype(np.float64)
    inv = ROPE_THETA ** (-np.arange(AXIS_PAIRS, dtype=np.float64) / AXIS_PAIRS)
    ang = pos * inv[pair][None, :]
    sin = np.sin(ang)
    return (jnp.asarray(np.cos(ang), dtype=F32),
            jnp.asarray(np.where(half[None, :] == 0, -sin, sin), dtype=F32))


def _cos_sin_mod(idx, n):
    ang = (idx & (n - 1)).astype(F32) * (2.0 * math.pi / n)
    return jnp.cos(ang), jnp.sin(ang)


def _dft_tables(n, fine=32):
    k = jnp.arange(n, dtype=jnp.int32)[:, None]
    ca, sa = _cos_sin_mod(k * (jnp.arange(n // fine, dtype=jnp.int32)[None, :] * fine), n)
    cb, sb = _cos_sin_mod(k * jnp.arange(fine, dtype=jnp.int32)[None, :], n)
    ca, sa, cb, sb = ca[:, :, None], sa[:, :, None], cb[:, None, :], sb[:, None, :]
    cos = (ca * cb - sa * sb).reshape(n, n)
    neg_sin = (-(sa * cb + ca * sb)).reshape(n, n)
    return cos.astype(BF16), neg_sin.astype(BF16)


def _channel_dft():
    c = jnp.arange(FOURIER_WIDTH, dtype=jnp.int32)
    same = (c[:, None] // FOURIER_GROUP) == (c[None, :] // FOURIER_GROUP)
    cos, sin = _cos_sin_mod((c[:, None] % FOURIER_GROUP) * (c[None, :] % FOURIER_GROUP), FOURIER_GROUP)
    return jnp.concatenate([jnp.where(same, cos, 0.0), jnp.where(same, sin, 0.0)], axis=1).astype(BF16)


def _head_matrices():
    r = np.arange(QK_WIDTH) // HEAD_DIM
    m = (r[:, None] == np.arange(LANES)[None, :]).astype(np.float32)
    return jnp.asarray(m.astype(BF16)), jnp.asarray(m.T.copy().astype(BF16))


def _block_diag(mats):
    nl, g, a, b = mats.shape
    eye = jnp.eye(g, dtype=mats.dtype)
    return (mats[:, :, :, None, :] * eye[None, :, None, :, None]).reshape(nl, g * a, g * b)


def kernel(x, c, ctx, c_ctx, w_mod, b_mod, norm1, norm2, w_in, conv_sc, qk_gain, pool_mat, pool_scale,
           w_br_attn, w_br_sc, w_br_f, w_br_p, w_out, w_up, w_conv_ffn, w_down, final_norm):
    bsz, n_lat, _ = x.shape
    n_ctx = ctx.shape[1]
    nl = w_in.shape[0]
    tq_lat = min(n_lat, TOKEN_TILE)
    tq_ctx = min(n_ctx, TOKEN_TILE)
    tq_attn = min(n_lat, ATTN_TILE)
    tq_ffn_lat = min(n_lat, FFN_TILE)
    tq_ffn_ctx = min(n_ctx, FFN_TILE)

    mod_rows = 16
    ctx_row = bsz
    cvec = jnp.concatenate([c, c_ctx[None], jnp.zeros((mod_rows - bsz - 1, D_MODEL), F32)], axis=0)
    gain = jnp.concatenate([jnp.tile(qk_gain[:, 0], (1, N_HEADS)) * (HEAD_DIM ** -0.5),
                            jnp.tile(qk_gain[:, 1], (1, N_KV_HEADS))], axis=1)
    head_sum, head_bcast = _head_matrices()
    prm = {
        "mod": _modulation(cvec, w_mod, b_mod),
        "norm1": norm1.reshape(nl, 1, D_MODEL),
        "norm2": norm2.reshape(nl, 1, D_MODEL),
        "w_in": w_in.astype(BF16),
        "head_sum": head_sum,
        "head_bcast": head_bcast,
        "gain": gain.reshape(nl, 1, QK_WIDTH),
        "conv_sc": conv_sc,
        "pool_mat": _block_diag(pool_mat).astype(BF16),
        "pool_scale": pool_scale.reshape(nl, 1, POOL_WIDTH),
        "w_br_attn": w_br_attn.astype(BF16),
        "w_br_sc": w_br_sc.astype(BF16),
        "w_br_f": w_br_f.astype(BF16),
        "w_br_p": w_br_p.astype(BF16),
        "w_out": w_out.astype(BF16),
        "w_up": w_up.astype(BF16),
        "w_conv_ffn": w_conv_ffn,
        "w_down": w_down.astype(BF16),
        "final_norm": final_norm.reshape(1, D_MODEL),
    }
    cos_t, sin_t = _rope_tables(n_lat)
    wc = _channel_dft()
    dft_lat = _dft_tables(n_lat)
    dft_ctx = _dft_tables(n_ctx)

    lat, cx = x, ctx
    for l in range(nl):
        last = l == nl - 1
        if last:
            kc, vc = _kv_proj(cx, prm, l, ctx_row, tq=tq_ctx)
        else:
            qc, kc, vc, scc, xfc, ypc = _in_proj(cx, prm, l, ctx_row, cos_t, sin_t, tq=tq_ctx, rope=False)
        ql, kl, vl, scl, xfl, ypl = _in_proj(lat, prm, l, None, cos_t, sin_t, tq=tq_lat, rope=True)
        oa = _attention(ql, [(kl, vl), (kc, vc)], tq=tq_attn)
        xfour = _fourier(xfl, wc, *dft_lat)
        lat = _mix(lat, prm, l, None, oa, scl, xfour, ypl, tq=tq_lat)
        lat = _ffn(lat, prm, l, None, tq=tq_ffn_lat, final=last)

        if not last:
            oac = _attention(qc, [(kc, vc)], tq=tq_ctx)
            xfourc = _fourier(xfc, wc, *dft_ctx)
            cx = _mix(cx, prm, l, ctx_row, oac, scc, xfourc, ypc, tq=tq_ctx)
            cx = _ffn(cx, prm, l, ctx_row, tq=tq_ffn_ctx, final=False)
    return lat
```

```python
import functools
import math

import numpy as np
import jax
import jax.numpy as jnp
from jax import lax
from jax.experimental import pallas as pl
from jax.experimental.pallas import tpu as pltpu

D_MODEL = 1024
DEPTH = 2
GRID_W = 64
NORM_EPS = 1e-6
N_HEADS = 8
N_KV_HEADS = 2
HEAD_DIM = 64
Q_WIDTH = N_HEADS * HEAD_DIM
KV_WIDTH = N_KV_HEADS * HEAD_DIM
QK_WIDTH = Q_WIDTH + KV_WIDTH
ROPE_THETA = 10000.0
AXIS_PAIRS = HEAD_DIM // 4
SC_WIDTH = 256
FOURIER_GROUPS = 4
FOURIER_WIDTH = 256
FOURIER_GROUP = FOURIER_WIDTH // FOURIER_GROUPS
POOL_WINDOWS = (2, 4, 8, 16)
POOL_WIDTH = 256
POOL_GROUP = POOL_WIDTH // len(POOL_WINDOWS)
N_BRANCHES = 4
D_FF = 2816

OFF_K = Q_WIDTH
OFF_V = OFF_K + KV_WIDTH
OFF_SC = OFF_V + KV_WIDTH
OFF_F = OFF_SC + 3 * SC_WIDTH
OFF_P = OFF_F + FOURIER_WIDTH
OFF_G = OFF_P + POOL_WIDTH
IN_WIDTH = OFF_G + N_BRANCHES * D_MODEL

LANES = 128
HALO = 8
FF_CHUNK = 256
N_FF_CHUNKS = D_FF // FF_CHUNK
ATTN_SUB = 256
TOKEN_TILE = 512
ATTN_TILE = 1024
FFN_TILE = 256
N_MOD = 6
VMEM_LIMIT = 62 * 1024 * 1024

F32 = jnp.float32
BF16 = jnp.bfloat16


def _dot(a, b):
    return jnp.dot(a, b, preferred_element_type=F32)


def _dot_nt(a, b):
    return lax.dot_general(a, b, (((1,), (1,)), ((), ())), preferred_element_type=F32)


def _fixed_spec(block, index):
    return pl.BlockSpec(block, lambda *_: index, pipeline_mode=pl.Buffered(1))


def _layer_spec(tail, layer, tail_index=None):
    tail_index = (0,) * len(tail) if tail_index is None else tail_index
    return _fixed_spec((None,) + tuple(tail), (layer,) + tuple(tail_index))


def _mod_spec(layer, row):
    if row is None:
        return pl.BlockSpec((None, None, N_MOD, D_MODEL), lambda i, b: (layer, b, 0, 0))
    return _fixed_spec((None, None, N_MOD, D_MODEL), (layer, row, 0, 0))


def _params():
    return pltpu.CompilerParams(dimension_semantics=("arbitrary", "arbitrary"),
                                vmem_limit_bytes=VMEM_LIMIT)


def _rms_mod(xv, norm, shift, scale):
    ms = jnp.mean(xv * xv, axis=-1, keepdims=True)
    y = xv * lax.rsqrt(ms + NORM_EPS) * norm
    return y * (1.0 + scale) + shift


def _halo_specs(n, tq, width):
    nblk = n // HALO
    per = tq // HALO
    prev = pl.BlockSpec((1, HALO, width), lambda i, b: (b, jnp.maximum(i * per - 1, 0), 0))
    cur = pl.BlockSpec((1, tq, width), lambda i, b: (b, i, 0))
    nxt = pl.BlockSpec((1, HALO, width), lambda i, b: (b, jnp.minimum((i + 1) * per, nblk - 1), 0))
    return [prev, cur, nxt]


def _fill_halo_rows(h_scr, xp_ref, xc_ref, xn_ref, fn, tq):
    i = pl.program_id(0)
    last = pl.num_programs(0) - 1
    h_scr[HALO:HALO + tq, :] = fn(xc_ref[0])
    h_scr[0:HALO, :] = jnp.where(i > 0, fn(xp_ref[0]), 0.0)
    h_scr[HALO + tq:2 * HALO + tq, :] = jnp.where(i < last, fn(xn_ref[0]), 0.0)


def _split_bf16(a):
    hi = a.astype(BF16)
    return hi, (a - hi.astype(F32)).astype(BF16)


def _head_rms(z, sum_ref, bcast_ref, gain):
    sq_hi, sq_lo = _split_bf16(z * z)
    ss = _dot(sq_hi, sum_ref[...]) + _dot(sq_lo, sum_ref[...])
    r_hi, r_lo = _split_bf16(lax.rsqrt(ss * (1.0 / HEAD_DIM) + NORM_EPS))
    return z * (_dot(r_hi, bcast_ref[...]) + _dot(r_lo, bcast_ref[...])) * gain


def _dup_heads(a, low_half):
    sw = pltpu.roll(a, HEAD_DIM, 1)
    return jnp.where(low_half, a, sw), jnp.where(low_half, sw, a)


def _mod_kernel(c_ref, w_ref, b_ref, o_ref):
    cv = c_ref[...]
    a = (cv * jax.nn.sigmoid(cv)).astype(BF16)
    o_ref[0] = _dot(a, w_ref[0].astype(BF16)) + b_ref[0]


def _modulation(cvec, w_mod, b_mod):
    rows = cvec.shape[0]
    nl = w_mod.shape[0]
    out = pl.pallas_call(
        _mod_kernel,
        out_shape=jax.ShapeDtypeStruct((nl, rows, N_MOD * D_MODEL), F32),
        grid=(nl, N_MOD),
        in_specs=[pl.BlockSpec((rows, D_MODEL), lambda l, j: (0, 0)),
                  pl.BlockSpec((1, D_MODEL, D_MODEL), lambda l, j: (l, 0, j)),
                  pl.BlockSpec((1, 1, D_MODEL), lambda l, j: (l, 0, j))],
        out_specs=pl.BlockSpec((1, rows, D_MODEL), lambda l, j: (l, 0, j)),
        compiler_params=_params(),
        name="adaln_mod",
    )(cvec, w_mod, b_mod.reshape(nl, 1, N_MOD * D_MODEL))
    return out.reshape(nl, rows, N_MOD, D_MODEL)


def _in_proj_kernel(xp_ref, xc_ref, xn_ref, m_ref, norm_ref, w_ref, hs_ref, hb_ref, gain_ref, cos_ref, sin_ref,
                    conv_ref, pm_ref, ps_ref,
                    q_ref, k_ref, v_ref, sc_ref, xf_ref, yp_ref,
                    h_scr, z_scr, *, n, tq, rope):
    i = pl.program_id(0)
    norm = norm_ref[...]
    shift = m_ref[0:1, :]
    scale = m_ref[1:2, :]
    _fill_halo_rows(h_scr, xp_ref, xc_ref, xn_ref,
                    lambda xv: _rms_mod(xv, norm, shift, scale), tq)
    z_scr[...] = _dot(h_scr[...].astype(BF16), w_ref[...])

    lane = lax.broadcasted_iota(jnp.int32, (tq, LANES), 1)
    low_half = lane < HEAD_DIM
    rows = slice(HALO, HALO + tq)

    y = _head_rms(z_scr[rows, 0:QK_WIDTH], hs_ref, hb_ref, gain_ref[...])
    second_half = (lane & AXIS_PAIRS) != 0
    for c in range(QK_WIDTH // LANES):
        yc = y[:, c * LANES:(c + 1) * LANES]
        if rope:
            partner = jnp.where(second_half, pltpu.roll(yc, AXIS_PAIRS, 1),
                                pltpu.roll(yc, LANES - AXIS_PAIRS, 1))
            yc = yc * cos_ref[...] + partner * sin_ref[...]
        if c < Q_WIDTH // LANES:
            q_ref[0, :, c * LANES:(c + 1) * LANES] = yc.astype(BF16)
        else:
            k0, k1 = _dup_heads(yc, low_half)
            k_ref[0, :, 0:LANES] = k0.astype(BF16)
            k_ref[0, :, LANES:2 * LANES] = k1.astype(BF16)
    v0, v1 = _dup_heads(z_scr[rows, OFF_V:OFF_SC], low_half)
    v_ref[0, :, 0:LANES] = v0.astype(BF16)
    v_ref[0, :, LANES:2 * LANES] = v1.astype(BF16)

    def u_at(off):
        r = slice(HALO + off, HALO + off + tq)
        return z_scr[r, OFF_SC + SC_WIDTH:OFF_SC + 2 * SC_WIDTH] * z_scr[r, OFF_SC + 2 * SC_WIDTH:OFF_F]

    conv = u_at(-1) * conv_ref[0:1, :] + u_at(0) * conv_ref[1:2, :] + u_at(1) * conv_ref[2:3, :]
    sc_ref[0] = (z_scr[rows, OFF_SC:OFF_SC + SC_WIDTH] * conv).astype(BF16)

    xf_ref[0] = z_scr[rows, OFF_F:OFF_P].astype(BF16)

    t = i * tq + lax.broadcasted_iota(jnp.int32, (tq, LANES), 0)

    def p_at(off, c):
        return z_scr[HALO + off:HALO + off + tq, OFF_P + c * LANES:OFF_P + (c + 1) * LANES]

    def window_mean(total, w_low, w_high):
        w = jnp.where(low_half, w_low, w_high)
        left, right = (w - 1) // 2, w // 2
        cnt = jnp.minimum(t + right + 1, n) - jnp.maximum(t - left, 0)
        return total / cnt.astype(F32)

    p0 = p_at(0, 0)
    s2 = p0 + p_at(1, 0)
    s4 = s2 + p_at(-1, 0) + p_at(2, 0)
    pooled0 = window_mean(jnp.where(low_half, s2, s4), POOL_WINDOWS[0], POOL_WINDOWS[1]) - p0
    p1 = p_at(0, 1)
    s8 = p1
    for off in (-3, -2, -1, 1, 2, 3, 4):
        s8 = s8 + p_at(off, 1)
    s16 = s8
    for off in (-7, -6, -5, -4, 5, 6, 7, 8):
        s16 = s16 + p_at(off, 1)
    pooled1 = window_mean(jnp.where(low_half, s8, s16), POOL_WINDOWS[2], POOL_WINDOWS[3]) - p1
    pooled = jnp.concatenate([pooled0, pooled1], axis=1).astype(BF16)
    yp_ref[0] = (_dot(pooled, pm_ref[...]) * ps_ref[...]).astype(BF16)


def _in_proj(xs, prm, layer, mod_row, cos_t, sin_t, *, tq, rope):
    b, n, _ = xs.shape
    out_w = (Q_WIDTH, 2 * LANES, 2 * LANES, SC_WIDTH, FOURIER_WIDTH, POOL_WIDTH)
    kern = functools.partial(_in_proj_kernel, n=n, tq=tq, rope=rope)
    return pl.pallas_call(
        kern,
        out_shape=[jax.ShapeDtypeStruct((b, n, w), BF16) for w in out_w],
        grid=(n // tq, b),
        in_specs=_halo_specs(n, tq, D_MODEL) + [
            _mod_spec(layer, mod_row),
            _layer_spec((1, D_MODEL), layer),
            _layer_spec((D_MODEL, OFF_G), layer),
            _fixed_spec((QK_WIDTH, LANES), (0, 0)),
            _fixed_spec((LANES, QK_WIDTH), (0, 0)),
            _layer_spec((1, QK_WIDTH), layer),
            pl.BlockSpec((tq, LANES), lambda i, bb: (i, 0)),
            pl.BlockSpec((tq, LANES), lambda i, bb: (i, 0)),
            _layer_spec((3, SC_WIDTH), layer),
            _layer_spec((POOL_WIDTH, POOL_WIDTH), layer),
            _layer_spec((1, POOL_WIDTH), layer),
        ],
        out_specs=[pl.BlockSpec((1, tq, w), lambda i, bb: (bb, i, 0)) for w in out_w],
        scratch_shapes=[pltpu.VMEM((tq + 2 * HALO, D_MODEL), F32),
                        pltpu.VMEM((tq + 2 * HALO, OFF_G), F32)],
        compiler_params=_params(),
        name="in_proj",
    )(xs, xs, xs, prm["mod"], prm["norm1"], prm["w_in"], prm["head_sum"], prm["head_bcast"], prm["gain"],
      cos_t, sin_t,
      prm["conv_sc"], prm["pool_mat"], prm["pool_scale"])


def _kv_proj_kernel(x_ref, m_ref, norm_ref, w_ref, hs_ref, hb_ref, gain_ref, k_ref, v_ref, *, tq):
    h = _rms_mod(x_ref[0], norm_ref[...], m_ref[0:1, :], m_ref[1:2, :]).astype(BF16)
    z = _dot(h, w_ref[...])
    low_half = lax.broadcasted_iota(jnp.int32, (tq, LANES), 1) < HEAD_DIM
    k0, k1 = _dup_heads(_head_rms(z[:, 0:KV_WIDTH], hs_ref, hb_ref, gain_ref[...]), low_half)
    k_ref[0, :, 0:LANES] = k0.astype(BF16)
    k_ref[0, :, LANES:2 * LANES] = k1.astype(BF16)
    v0, v1 = _dup_heads(z[:, KV_WIDTH:2 * KV_WIDTH], low_half)
    v_ref[0, :, 0:LANES] = v0.astype(BF16)
    v_ref[0, :, LANES:2 * LANES] = v1.astype(BF16)


def _kv_proj(xs, prm, layer, mod_row, *, tq):
    b, n, _ = xs.shape
    kv_cols = 2 * KV_WIDTH
    return pl.pallas_call(
        functools.partial(_kv_proj_kernel, tq=tq),
        out_shape=[jax.ShapeDtypeStruct((b, n, 2 * LANES), BF16)] * 2,
        grid=(n // tq, b),
        in_specs=[pl.BlockSpec((1, tq, D_MODEL), lambda i, bb: (bb, i, 0)),
                  _mod_spec(layer, mod_row),
                  _layer_spec((1, D_MODEL), layer),
                  _layer_spec((D_MODEL, kv_cols), layer, (0, OFF_K // kv_cols)),
                  _fixed_spec((KV_WIDTH, LANES), (0, 0)),
                  _fixed_spec((LANES, KV_WIDTH), (0, 0)),
                  _layer_spec((1, KV_WIDTH), layer, (0, Q_WIDTH // KV_WIDTH))],
        out_specs=[pl.BlockSpec((1, tq, 2 * LANES), lambda i, bb: (bb, i, 0))] * 2,
        compiler_params=_params(),
        name="kv_proj",
    )(xs, prm["mod"], prm["norm1"], prm["w_in"], prm["head_sum"], prm["head_bcast"], prm["gain"])


def _attn_kernel(q_ref, *refs, tq):
    *kv_refs, o_ref = refs
    sub = min(tq, ATTN_SUB)
    lane = lax.broadcasted_iota(jnp.int32, (sub, LANES), 1)
    low_half = lane < HEAD_DIM
    sources = [(kv_refs[2 * a], kv_refs[2 * a + 1]) for a in range(len(kv_refs) // 2)]
    for j in range(Q_WIDTH // LANES):
        g = (2 * j) // (N_HEADS // N_KV_HEADS)
        gl = slice(g * LANES, (g + 1) * LANES)
        for r in range(tq // sub):
            rows = slice(r * sub, (r + 1) * sub)
            qp = q_ref[0, rows, j * LANES:(j + 1) * LANES]
            zero = jnp.zeros_like(qp)
            q2 = jnp.concatenate([jnp.where(low_half, qp, zero), jnp.where(low_half, zero, qp)], axis=0)
            scores = [_dot_nt(q2, k_ref[0, :, gl]) for k_ref, _ in sources]
            row_max = functools.reduce(jnp.maximum, [jnp.max(s, axis=-1, keepdims=True) for s in scores])
            o, denom = None, None
            for s, (_, v_ref) in zip(scores, sources):
                p = jnp.exp(s - row_max)
                part = jnp.sum(p, axis=-1, keepdims=True)
                term = _dot(p.astype(BF16), v_ref[0, :, gl])
                o, denom = (term, part) if o is None else (o + term, denom + part)
            o = o / denom
            o_ref[0, rows, j * LANES:(j + 1) * LANES] = jnp.where(low_half, o[:sub], o[sub:]).astype(BF16)


def _attention(q, kv_sources, *, tq):
    b, n, _ = q.shape
    kv_specs, kv_args = [], []
    for k, v in kv_sources:
        for a in (k, v):
            kv_specs.append(pl.BlockSpec((1, a.shape[1], 2 * LANES), lambda bb, i: (bb, 0, 0)))
            kv_args.append(a)
    return pl.pallas_call(
        functools.partial(_attn_kernel, tq=tq),
        out_shape=jax.ShapeDtypeStruct((b, n, Q_WIDTH), BF16),
        grid=(b, n // tq),
        in_specs=[pl.BlockSpec((1, tq, Q_WIDTH), lambda bb, i: (bb, i, 0))] + kv_specs,
        out_specs=pl.BlockSpec((1, tq, Q_WIDTH), lambda bb, i: (bb, i, 0)),
        compiler_params=_params(),
        name="attention",
    )(q, *kv_args)


def _fourier_kernel(x_ref, wc_ref, cn_ref, nsn_ref, o_ref, t_scr, *, n, tr):
    t_scr[...] = _dot(x_ref[0], wc_ref[...].astype(BF16)).astype(BF16)
    scale = 1.0 / math.sqrt(n * FOURIER_GROUP)
    for r in range(n // tr):
        rows = slice(r * tr, (r + 1) * tr)
        acc = (_dot(cn_ref[rows, :].astype(BF16), t_scr[:, 0:FOURIER_WIDTH])
               + _dot(nsn_ref[rows, :].astype(BF16), t_scr[:, FOURIER_WIDTH:2 * FOURIER_WIDTH]))
        o_ref[0, rows, :] = (acc * scale).astype(BF16)


def _fourier(xf, wc, cn, nsn):
    b, n, _ = xf.shape
    tr = min(n, 512)
    return pl.pallas_call(
        functools.partial(_fourier_kernel, n=n, tr=tr),
        out_shape=jax.ShapeDtypeStruct((b, n, FOURIER_WIDTH), BF16),
        grid=(b, 1),
        in_specs=[pl.BlockSpec((1, n, FOURIER_WIDTH), lambda bb, i: (bb, 0, 0)),
                  _fixed_spec((FOURIER_WIDTH, 2 * FOURIER_WIDTH), (0, 0)),
                  _fixed_spec((n, n), (0, 0)),
                  _fixed_spec((n, n), (0, 0))],
        out_specs=pl.BlockSpec((1, n, FOURIER_WIDTH), lambda bb, i: (bb, 0, 0)),
        scratch_shapes=[pltpu.VMEM((n, 2 * FOURIER_WIDTH), BF16)],
        compiler_params=_params(),
        name="fourier",
    )(xf, wc, cn, nsn)


def _mix_kernel(x_ref, m_ref, norm_ref, oa_ref, sc_ref, xf_ref, yp_ref,
                wg01_ref, wg23_ref, wa_ref, wsc_ref, wf_ref, wp_ref, wo_ref, out_ref, y_scr):
    xv = x_ref[0]
    h = _rms_mod(xv, norm_ref[...], m_ref[0:1, :], m_ref[1:2, :]).astype(BF16)
    branches = ((oa_ref, wa_ref), (sc_ref, wsc_ref), (xf_ref, wf_ref), (yp_ref, wp_ref))
    gate_w = (wg01_ref, wg23_ref)
    for j in range(D_MODEL // FF_CHUNK):
        cols = slice(j * FF_CHUNK, (j + 1) * FF_CHUNK)
        y = None
        for bi, (a_ref, w_ref) in enumerate(branches):
            off = (bi % 2) * D_MODEL + j * FF_CHUNK
            gate = 0.5 + 0.5 * jnp.tanh(0.5 * _dot(h, gate_w[bi // 2][:, off:off + FF_CHUNK]))
            term = gate * _dot(a_ref[0], w_ref[:, cols])
            y = term if y is None else y + term
        y_scr[:, cols] = y.astype(BF16)
    out_ref[0] = xv + m_ref[2:3, :] * _dot(y_scr[...], wo_ref[...])


def _mix(xs, prm, layer, mod_row, oa, sc, xf, yp, *, tq):
    b, n, _ = xs.shape
    half_gates = N_BRANCHES * D_MODEL // 2

    def tok(w):
        return pl.BlockSpec((1, tq, w), lambda i, bb: (bb, i, 0))

    return pl.pallas_call(
        _mix_kernel,
        out_shape=jax.ShapeDtypeStruct((b, n, D_MODEL), F32),
        grid=(n // tq, b),
        in_specs=[tok(D_MODEL), _mod_spec(layer, mod_row), _layer_spec((1, D_MODEL), layer),
                  tok(Q_WIDTH), tok(SC_WIDTH), tok(FOURIER_WIDTH), tok(POOL_WIDTH),
                  _layer_spec((D_MODEL, half_gates), layer, (0, OFF_G // half_gates)),
                  _layer_spec((D_MODEL, half_gates), layer, (0, OFF_G // half_gates + 1)),
                  _layer_spec((Q_WIDTH, D_MODEL), layer), _layer_spec((SC_WIDTH, D_MODEL), layer),
                  _layer_spec((FOURIER_WIDTH, D_MODEL), layer), _layer_spec((POOL_WIDTH, D_MODEL), layer),
                  _layer_spec((D_MODEL, D_MODEL), layer)],
        out_specs=tok(D_MODEL),
        scratch_shapes=[pltpu.VMEM((tq, D_MODEL), BF16)],
        compiler_params=_params(),
        name="mix",
    )(xs, prm["mod"], prm["norm1"], oa, sc, xf, yp, prm["w_in"], prm["w_in"],
      prm["w_br_attn"], prm["w_br_sc"], prm["w_br_f"], prm["w_br_p"], prm["w_out"])


def _ffn_kernel(xp_ref, xc_ref, xn_ref, m_ref, norm_ref, wu_ref, cw_ref, wd_ref, fn_ref, out_ref,
                h_scr, hb_scr, u0_scr, u1_scr, acc_scr, *, tq, final):
    norm = norm_ref[...]
    shift = m_ref[3:4, :]
    scale = m_ref[4:5, :]
    _fill_halo_rows(h_scr, xp_ref, xc_ref, xn_ref,
                    lambda xv: _rms_mod(xv, norm, shift, scale), tq)
    hb_scr[...] = h_scr[...].astype(BF16)
    acc_scr[...] = jnp.zeros_like(acc_scr)
    ext = tq + 2 * HALO

    def cols(c, half):
        return slice(half * D_FF + c * FF_CHUNK, half * D_FF + (c + 1) * FF_CHUNK)

    def up(c, u_scr):
        hb = hb_scr[...]
        u_scr[0] = _dot(hb, wu_ref[:, cols(c, 0)])
        u_scr[1] = _dot(hb, wu_ref[:, cols(c, 1)])

    def conv3(u, w):
        full = pltpu.roll(u, 1, 0) * w[0:1, :] + u * w[1:2, :] + pltpu.roll(u, ext - 1, 0) * w[2:3, :]
        return full[HALO:HALO + tq]

    def down(c, u_scr):
        a = conv3(u_scr[0], cw_ref[:, cols(c, 0)])
        bv = conv3(u_scr[1], cw_ref[:, cols(c, 1)])
        half = 0.5 * a
        act = ((half + half * jnp.tanh(half)) * bv).astype(BF16)
        acc_scr[...] += _dot(act, wd_ref[c * FF_CHUNK:(c + 1) * FF_CHUNK, :])

    bufs = (u0_scr, u1_scr)
    up(0, u0_scr)
    for c in range(N_FF_CHUNKS):
        if c + 1 < N_FF_CHUNKS:
            up(c + 1, bufs[(c + 1) % 2])
        down(c, bufs[c % 2])
    out = xc_ref[0] + m_ref[5:6, :] * acc_scr[...]
    if final:
        ms = jnp.mean(out * out, axis=-1, keepdims=True)
        out = out * lax.rsqrt(ms + NORM_EPS) * fn_ref[...]
    out_ref[0] = out


def _ffn(xs, prm, layer, mod_row, *, tq, final):
    b, n, _ = xs.shape
    return pl.pallas_call(
        functools.partial(_ffn_kernel, tq=tq, final=final),
        out_shape=jax.ShapeDtypeStruct((b, n, D_MODEL), F32),
        grid=(n // tq, b),
        in_specs=_halo_specs(n, tq, D_MODEL) + [
            _mod_spec(layer, mod_row),
            _layer_spec((1, D_MODEL), layer),
            _layer_spec((D_MODEL, 2 * D_FF), layer),
            _layer_spec((3, 2 * D_FF), layer),
            _layer_spec((D_FF, D_MODEL), layer),
            _fixed_spec((1, D_MODEL), (0, 0))],
        out_specs=pl.BlockSpec((1, tq, D_MODEL), lambda i, bb: (bb, i, 0)),
        scratch_shapes=[pltpu.VMEM((tq + 2 * HALO, D_MODEL), F32),
                        pltpu.VMEM((tq + 2 * HALO, D_MODEL), BF16),
                        pltpu.VMEM((2, tq + 2 * HALO, FF_CHUNK), F32),
                        pltpu.VMEM((2, tq + 2 * HALO, FF_CHUNK), F32),
                        pltpu.VMEM((tq, D_MODEL), F32)],
        compiler_params=_params(),
        name="ffn",
    )(xs, xs, xs, prm["mod"], prm["norm2"], prm["w_up"], prm["w_conv_ffn"], prm["w_down"], prm["final_norm"])


def _rope_tables(n):
    d = np.arange(LANES)
    hd = d % HEAD_DIM
    axis = hd // (2 * AXIS_PAIRS)
    half = (hd % (2 * AXIS_PAIRS)) // AXIS_PAIRS
    pair = hd % AXIS_PAIRS
    t = np.arange(n)
    pos = np.where(axis[None, :] == 0, (t // GRID_W)[:, None], (t % GRID_W)[:, None]).astype(np.float64)
    inv = ROPE_THETA ** (-np.arange(AXIS_PAIRS, dtype=np.float64) / AXIS_PAIRS)
    ang = pos * inv[pair][None, :]
    sin = np.sin(ang)
    return (jnp.asarray(np.cos(ang), dtype=F32),
            jnp.asarray(np.where(half[None, :] == 0, -sin, sin), dtype=F32))


def _dft_tables(n):
    k = np.arange(n, dtype=np.int64)
    ang = ((k[:, None] * k[None, :]) % n).astype(np.float64) * (2.0 * math.pi / n)
    return jnp.asarray(np.cos(ang), dtype=F32), jnp.asarray(-np.sin(ang), dtype=F32)


def _channel_dft():
    c = np.arange(FOURIER_WIDTH, dtype=np.int64)
    same = (c[:, None] // FOURIER_GROUP) == (c[None, :] // FOURIER_GROUP)
    mc = ((c[:, None] % FOURIER_GROUP) * (c[None, :] % FOURIER_GROUP)) % FOURIER_GROUP
    ang = mc.astype(np.float64) * (2.0 * math.pi / FOURIER_GROUP)
    both = np.concatenate([np.where(same, np.cos(ang), 0.0), np.where(same, np.sin(ang), 0.0)], axis=1)
    return jnp.asarray(both, dtype=F32)


def _head_matrices():
    r = np.arange(QK_WIDTH) // HEAD_DIM
    m = (r[:, None] == np.arange(LANES)[None, :]).astype(np.float32)
    return jnp.asarray(m.astype(BF16)), jnp.asarray(m.T.copy().astype(BF16))


def _block_diag(mats):
    nl, g, a, b = mats.shape
    eye = jnp.eye(g, dtype=mats.dtype)
    return (mats[:, :, :, None, :] * eye[None, :, None, :, None]).reshape(nl, g * a, g * b)


def kernel(x, c, ctx, c_ctx, w_mod, b_mod, norm1, norm2, w_in, conv_sc, qk_gain, pool_mat, pool_scale,
           w_br_attn, w_br_sc, w_br_f, w_br_p, w_out, w_up, w_conv_ffn, w_down, final_norm):
    bsz, n_lat, _ = x.shape
    n_ctx = ctx.shape[1]
    nl = w_in.shape[0]
    tq_lat = min(n_lat, TOKEN_TILE)
    tq_ctx = min(n_ctx, TOKEN_TILE)
    tq_attn = min(n_lat, ATTN_TILE)
    tq_ffn_lat = min(n_lat, FFN_TILE)
    tq_ffn_ctx = min(n_ctx, FFN_TILE)

    mod_rows = 16
    ctx_row = bsz
    cvec = jnp.concatenate([c, c_ctx[None], jnp.zeros((mod_rows - bsz - 1, D_MODEL), F32)], axis=0)
    gain = jnp.concatenate([jnp.tile(qk_gain[:, 0], (1, N_HEADS)) * (HEAD_DIM ** -0.5),
                            jnp.tile(qk_gain[:, 1], (1, N_KV_HEADS))], axis=1)
    head_sum, head_bcast = _head_matrices()
    prm = {
        "mod": _modulation(cvec, w_mod, b_mod),
        "norm1": norm1.reshape(nl, 1, D_MODEL),
        "norm2": norm2.reshape(nl, 1, D_MODEL),
        "w_in": w_in.astype(BF16),
        "head_sum": head_sum,
        "head_bcast": head_bcast,
        "gain": gain.reshape(nl, 1, QK_WIDTH),
        "conv_sc": conv_sc,
        "pool_mat": _block_diag(pool_mat).astype(BF16),
        "pool_scale": pool_scale.reshape(nl, 1, POOL_WIDTH),
        "w_br_attn": w_br_attn.astype(BF16),
        "w_br_sc": w_br_sc.astype(BF16),
        "w_br_f": w_br_f.astype(BF16),
        "w_br_p": w_br_p.astype(BF16),
        "w_out": w_out.astype(BF16),
        "w_up": w_up.astype(BF16),
        "w_conv_ffn": w_conv_ffn,
        "w_down": w_down.astype(BF16),
        "final_norm": final_norm.reshape(1, D_MODEL),
    }
    cos_t, sin_t = _rope_tables(n_lat)
    wc = _channel_dft()
    dft_lat = _dft_tables(n_lat)
    dft_ctx = _dft_tables(n_ctx)

    lat, cx = x, ctx
    for l in range(nl):
        last = l == nl - 1
        if last:
            kc, vc = _kv_proj(cx, prm, l, ctx_row, tq=tq_ctx)
        else:
            qc, kc, vc, scc, xfc, ypc = _in_proj(cx, prm, l, ctx_row, cos_t, sin_t, tq=tq_ctx, rope=False)
        ql, kl, vl, scl, xfl, ypl = _in_proj(lat, prm, l, None, cos_t, sin_t, tq=tq_lat, rope=True)
        oa = _attention(ql, [(kl, vl), (kc, vc)], tq=tq_attn)
        xfour = _fourier(xfl, wc, *dft_lat)
        lat = _mix(lat, prm, l, None, oa, scl, xfour, ypl, tq=tq_lat)
        lat = _ffn(lat, prm, l, None, tq=tq_ffn_lat, final=last)

        if not last:
            oac = _attention(qc, [(kc, vc)], tq=tq_ctx)
            xfourc = _fourier(xfc, wc, *dft_ctx)
            cx = _mix(cx, prm, l, ctx_row, oac, scc, xfourc, ypc, tq=tq_ctx)
            cx = _ffn(cx, prm, l, ctx_row, tq=tq_ffn_ctx, final=False)
    return lat
```

```python
import functools
import math

import numpy as np
import jax
import jax.numpy as jnp
from jax import lax
from jax.experimental import pallas as pl
from jax.experimental.pallas import tpu as pltpu

D_MODEL = 1024
DEPTH = 2
GRID_W = 64
NORM_EPS = 1e-6
N_HEADS = 8
N_KV_HEADS = 2
HEAD_DIM = 64
Q_WIDTH = N_HEADS * HEAD_DIM
KV_WIDTH = N_KV_HEADS * HEAD_DIM
QK_WIDTH = Q_WIDTH + KV_WIDTH
ROPE_THETA = 10000.0
AXIS_PAIRS = HEAD_DIM // 4
SC_WIDTH = 256
FOURIER_GROUPS = 4
FOURIER_WIDTH = 256
FOURIER_GROUP = FOURIER_WIDTH // FOURIER_GROUPS
POOL_WINDOWS = (2, 4, 8, 16)
POOL_WIDTH = 256
POOL_GROUP = POOL_WIDTH // len(POOL_WINDOWS)
N_BRANCHES = 4
D_FF = 2816

OFF_K = Q_WIDTH
OFF_V = OFF_K + KV_WIDTH
OFF_SC = OFF_V + KV_WIDTH
OFF_F = OFF_SC + 3 * SC_WIDTH
OFF_P = OFF_F + FOURIER_WIDTH
OFF_G = OFF_P + POOL_WIDTH
IN_WIDTH = OFF_G + N_BRANCHES * D_MODEL

LANES = 128
HALO = 8
FF_CHUNK = 256
N_FF_CHUNKS = D_FF // FF_CHUNK
ATTN_SUB = 256
TOKEN_TILE = 512
IN_PROJ_BATCH = 2
ATTN_TILE = 1024
FFN_TILE = 256
N_MOD = 6
VMEM_LIMIT = 62 * 1024 * 1024

F32 = jnp.float32
BF16 = jnp.bfloat16


def _dot(a, b):
    return jnp.dot(a, b, preferred_element_type=F32)


def _dot_nt(a, b):
    return lax.dot_general(a, b, (((1,), (1,)), ((), ())), preferred_element_type=F32)


def _fixed_spec(block, index):
    return pl.BlockSpec(block, lambda *_: index, pipeline_mode=pl.Buffered(1))


def _layer_spec(tail, layer, tail_index=None):
    tail_index = (0,) * len(tail) if tail_index is None else tail_index
    return _fixed_spec((None,) + tuple(tail), (layer,) + tuple(tail_index))


def _mod_spec(layer, row):
    if row is None:
        return pl.BlockSpec((None, None, N_MOD, D_MODEL), lambda i, b: (layer, b, 0, 0))
    return _fixed_spec((None, None, N_MOD, D_MODEL), (layer, row, 0, 0))


def _mod_spec_multi(layer, row, nb):
    if row is None:
        return pl.BlockSpec((None, nb, N_MOD, D_MODEL), lambda i, b: (layer, b, 0, 0))
    return _fixed_spec((None, 1, N_MOD, D_MODEL), (layer, row, 0, 0))


def _params():
    return pltpu.CompilerParams(dimension_semantics=("arbitrary", "arbitrary"),
                                vmem_limit_bytes=VMEM_LIMIT)


def _rms_mod(xv, norm, shift, scale):
    ms = jnp.mean(xv * xv, axis=-1, keepdims=True)
    y = xv * lax.rsqrt(ms + NORM_EPS) * norm
    return y * (1.0 + scale) + shift


def _halo_specs(n, tq, width, nb=1):
    nblk = n // HALO
    per = tq // HALO
    prev = pl.BlockSpec((nb, HALO, width), lambda i, b: (b, jnp.maximum(i * per - 1, 0), 0))
    cur = pl.BlockSpec((nb, tq, width), lambda i, b: (b, i, 0))
    nxt = pl.BlockSpec((nb, HALO, width), lambda i, b: (b, jnp.minimum((i + 1) * per, nblk - 1), 0))
    return [prev, cur, nxt]


def _fill_halo_rows(h_scr, xp_ref, xc_ref, xn_ref, fn, tq):
    i = pl.program_id(0)
    last = pl.num_programs(0) - 1
    h_scr[HALO:HALO + tq, :] = fn(xc_ref[0])
    h_scr[0:HALO, :] = jnp.where(i > 0, fn(xp_ref[0]), 0.0)
    h_scr[HALO + tq:2 * HALO + tq, :] = jnp.where(i < last, fn(xn_ref[0]), 0.0)


def _split_bf16(a):
    hi = a.astype(BF16)
    return hi, (a - hi.astype(F32)).astype(BF16)


def _head_rms(z, sum_ref, bcast_ref, gain):
    sq_hi, sq_lo = _split_bf16(z * z)
    ss = _dot(sq_hi, sum_ref[...]) + _dot(sq_lo, sum_ref[...])
    r_hi, r_lo = _split_bf16(lax.rsqrt(ss * (1.0 / HEAD_DIM) + NORM_EPS))
    return z * (_dot(r_hi, bcast_ref[...]) + _dot(r_lo, bcast_ref[...])) * gain


def _dup_heads(a, low_half):
    sw = pltpu.roll(a, HEAD_DIM, 1)
    return jnp.where(low_half, a, sw), jnp.where(low_half, sw, a)


def _mod_kernel(c_ref, w_ref, b_ref, o_ref):
    cv = c_ref[...]
    a = (cv * jax.nn.sigmoid(cv)).astype(BF16)
    o_ref[0] = _dot(a, w_ref[0].astype(BF16)) + b_ref[0]


def _modulation(cvec, w_mod, b_mod):
    rows = cvec.shape[0]
    nl = w_mod.shape[0]
    out = pl.pallas_call(
        _mod_kernel,
        out_shape=jax.ShapeDtypeStruct((nl, rows, N_MOD * D_MODEL), F32),
        grid=(nl, N_MOD),
        in_specs=[pl.BlockSpec((rows, D_MODEL), lambda l, j: (0, 0)),
                  pl.BlockSpec((1, D_MODEL, D_MODEL), lambda l, j: (l, 0, j)),
                  pl.BlockSpec((1, 1, D_MODEL), lambda l, j: (l, 0, j))],
        out_specs=pl.BlockSpec((1, rows, D_MODEL), lambda l, j: (l, 0, j)),
        compiler_params=_params(),
        name="adaln_mod",
    )(cvec, w_mod, b_mod.reshape(nl, 1, N_MOD * D_MODEL))
    return out.reshape(nl, rows, N_MOD, D_MODEL)


def _in_proj_kernel(xp_ref, xc_ref, xn_ref, m_ref, norm_ref, w_ref, hs_ref, hb_ref, gain_ref, cos_ref, sin_ref,
                    conv_ref, pm_ref, ps_ref,
                    q_ref, k_ref, v_ref, sc_ref, xf_ref, yp_ref,
                    h_scr, z_scr, *, n, tq, rope, nb):
    i = pl.program_id(0)
    norm = norm_ref[...]

    def norm_stage(e):
        m = m_ref.at[min(e, m_ref.shape[0] - 1)]
        shift, scale = m[0:1, :], m[1:2, :]
        _fill_halo_rows(h_scr.at[e], xp_ref.at[e:e + 1], xc_ref.at[e:e + 1], xn_ref.at[e:e + 1],
                        lambda xv: _rms_mod(xv, norm, shift, scale), tq)

    norm_stage(0)
    for e in range(nb):
        z_scr[e] = _dot(h_scr[e].astype(BF16), w_ref[...])
        if e + 1 < nb:
            norm_stage(e + 1)
        _local_mixers(e, z_scr.at[e], hs_ref, hb_ref, gain_ref, cos_ref, sin_ref, conv_ref, pm_ref, ps_ref,
                      q_ref, k_ref, v_ref, sc_ref, xf_ref, yp_ref, i=i, n=n, tq=tq, rope=rope)


def _local_mixers(e, z_scr, hs_ref, hb_ref, gain_ref, cos_ref, sin_ref, conv_ref, pm_ref, ps_ref,
                  q_ref, k_ref, v_ref, sc_ref, xf_ref, yp_ref, *, i, n, tq, rope):
    lane = lax.broadcasted_iota(jnp.int32, (tq, LANES), 1)
    low_half = lane < HEAD_DIM
    rows = slice(HALO, HALO + tq)

    y = _head_rms(z_scr[rows, 0:QK_WIDTH], hs_ref, hb_ref, gain_ref[...])
    second_half = (lane & AXIS_PAIRS) != 0
    for c in range(QK_WIDTH // LANES):
        yc = y[:, c * LANES:(c + 1) * LANES]
        if rope:
            partner = jnp.where(second_half, pltpu.roll(yc, AXIS_PAIRS, 1),
                                pltpu.roll(yc, LANES - AXIS_PAIRS, 1))
            yc = yc * cos_ref[...] + partner * sin_ref[...]
        if c < Q_WIDTH // LANES:
            q_ref[e, :, c * LANES:(c + 1) * LANES] = yc.astype(BF16)
        else:
            k0, k1 = _dup_heads(yc, low_half)
            k_ref[e, :, 0:LANES] = k0.astype(BF16)
            k_ref[e, :, LANES:2 * LANES] = k1.astype(BF16)
    v0, v1 = _dup_heads(z_scr[rows, OFF_V:OFF_SC], low_half)
    v_ref[e, :, 0:LANES] = v0.astype(BF16)
    v_ref[e, :, LANES:2 * LANES] = v1.astype(BF16)

    def u_at(off):
        r = slice(HALO + off, HALO + off + tq)
        return z_scr[r, OFF_SC + SC_WIDTH:OFF_SC + 2 * SC_WIDTH] * z_scr[r, OFF_SC + 2 * SC_WIDTH:OFF_F]

    conv = u_at(-1) * conv_ref[0:1, :] + u_at(0) * conv_ref[1:2, :] + u_at(1) * conv_ref[2:3, :]
    sc_ref[e] = (z_scr[rows, OFF_SC:OFF_SC + SC_WIDTH] * conv).astype(BF16)

    xf_ref[e] = z_scr[rows, OFF_F:OFF_P].astype(BF16)

    t = i * tq + lax.broadcasted_iota(jnp.int32, (tq, LANES), 0)

    def p_at(off, c):
        return z_scr[HALO + off:HALO + off + tq, OFF_P + c * LANES:OFF_P + (c + 1) * LANES]

    def window_mean(total, w_low, w_high):
        w = jnp.where(low_half, w_low, w_high)
        left, right = (w - 1) // 2, w // 2
        cnt = jnp.minimum(t + right + 1, n) - jnp.maximum(t - left, 0)
        return total / cnt.astype(F32)

    p0 = p_at(0, 0)
    s2 = p0 + p_at(1, 0)
    s4 = s2 + p_at(-1, 0) + p_at(2, 0)
    pooled0 = window_mean(jnp.where(low_half, s2, s4), POOL_WINDOWS[0], POOL_WINDOWS[1]) - p0
    p1 = p_at(0, 1)
    s8 = p1
    for off in (-3, -2, -1, 1, 2, 3, 4):
        s8 = s8 + p_at(off, 1)
    s16 = s8
    for off in (-7, -6, -5, -4, 5, 6, 7, 8):
        s16 = s16 + p_at(off, 1)
    pooled1 = window_mean(jnp.where(low_half, s8, s16), POOL_WINDOWS[2], POOL_WINDOWS[3]) - p1
    pooled = jnp.concatenate([pooled0, pooled1], axis=1).astype(BF16)
    yp_ref[e] = (_dot(pooled, pm_ref[...]) * ps_ref[...]).astype(BF16)


def _in_proj(xs, prm, layer, mod_row, cos_t, sin_t, *, tq, rope):
    b, n, _ = xs.shape
    out_w = (Q_WIDTH, 2 * LANES, 2 * LANES, SC_WIDTH, FOURIER_WIDTH, POOL_WIDTH)
    nb = IN_PROJ_BATCH
    kern = functools.partial(_in_proj_kernel, n=n, tq=tq, rope=rope, nb=nb)
    return pl.pallas_call(
        kern,
        out_shape=[jax.ShapeDtypeStruct((b, n, w), BF16) for w in out_w],
        grid=(n // tq, b // nb),
        in_specs=_halo_specs(n, tq, D_MODEL, nb) + [
            _mod_spec_multi(layer, mod_row, nb),
            _layer_spec((1, D_MODEL), layer),
            _layer_spec((D_MODEL, OFF_G), layer),
            _fixed_spec((QK_WIDTH, LANES), (0, 0)),
            _fixed_spec((LANES, QK_WIDTH), (0, 0)),
            _layer_spec((1, QK_WIDTH), layer),
            pl.BlockSpec((tq, LANES), lambda i, bb: (i, 0)),
            pl.BlockSpec((tq, LANES), lambda i, bb: (i, 0)),
            _layer_spec((3, SC_WIDTH), layer),
            _layer_spec((POOL_WIDTH, POOL_WIDTH), layer),
            _layer_spec((1, POOL_WIDTH), layer),
        ],
        out_specs=[pl.BlockSpec((nb, tq, w), lambda i, bb: (bb, i, 0)) for w in out_w],
        scratch_shapes=[pltpu.VMEM((nb, tq + 2 * HALO, D_MODEL), F32),
                        pltpu.VMEM((nb, tq + 2 * HALO, OFF_G), F32)],
        compiler_params=_params(),
        name="in_proj",
    )(xs, xs, xs, prm["mod"], prm["norm1"], prm["w_in"], prm["head_sum"], prm["head_bcast"], prm["gain"],
      cos_t, sin_t,
      prm["conv_sc"], prm["pool_mat"], prm["pool_scale"])


def _kv_proj_kernel(x_ref, m_ref, norm_ref, w_ref, hs_ref, hb_ref, gain_ref, k_ref, v_ref, *, tq):
    h = _rms_mod(x_ref[0], norm_ref[...], m_ref[0:1, :], m_ref[1:2, :]).astype(BF16)
    z = _dot(h, w_ref[...])
    low_half = lax.broadcasted_iota(jnp.int32, (tq, LANES), 1) < HEAD_DIM
    k0, k1 = _dup_heads(_head_rms(z[:, 0:KV_WIDTH], hs_ref, hb_ref, gain_ref[...]), low_half)
    k_ref[0, :, 0:LANES] = k0.astype(BF16)
    k_ref[0, :, LANES:2 * LANES] = k1.astype(BF16)
    v0, v1 = _dup_heads(z[:, KV_WIDTH:2 * KV_WIDTH], low_half)
    v_ref[0, :, 0:LANES] = v0.astype(BF16)
    v_ref[0, :, LANES:2 * LANES] = v1.astype(BF16)


def _kv_proj(xs, prm, layer, mod_row, *, tq):
    b, n, _ = xs.shape
    kv_cols = 2 * KV_WIDTH
    return pl.pallas_call(
        functools.partial(_kv_proj_kernel, tq=tq),
        out_shape=[jax.ShapeDtypeStruct((b, n, 2 * LANES), BF16)] * 2,
        grid=(n // tq, b),
        in_specs=[pl.BlockSpec((1, tq, D_MODEL), lambda i, bb: (bb, i, 0)),
                  _mod_spec(layer, mod_row),
                  _layer_spec((1, D_MODEL), layer),
                  _layer_spec((D_MODEL, kv_cols), layer, (0, OFF_K // kv_cols)),
                  _fixed_spec((KV_WIDTH, LANES), (0, 0)),
                  _fixed_spec((LANES, KV_WIDTH), (0, 0)),
                  _layer_spec((1, KV_WIDTH), layer, (0, Q_WIDTH // KV_WIDTH))],
        out_specs=[pl.BlockSpec((1, tq, 2 * LANES), lambda i, bb: (bb, i, 0))] * 2,
        compiler_params=_params(),
        name="kv_proj",
    )(xs, prm["mod"], prm["norm1"], prm["w_in"], prm["head_sum"], prm["head_bcast"], prm["gain"])


def _attn_kernel(q_ref, *refs, tq):
    *kv_refs, o_ref = refs
    sub = min(tq, ATTN_SUB)
    lane = lax.broadcasted_iota(jnp.int32, (sub, LANES), 1)
    low_half = lane < HEAD_DIM
    sources = [(kv_refs[2 * a], kv_refs[2 * a + 1]) for a in range(len(kv_refs) // 2)]
    for j in range(Q_WIDTH // LANES):
        g = (2 * j) // (N_HEADS // N_KV_HEADS)
        gl = slice(g * LANES, (g + 1) * LANES)
        for r in range(tq // sub):
            rows = slice(r * sub, (r + 1) * sub)
            qp = q_ref[0, rows, j * LANES:(j + 1) * LANES]
            zero = jnp.zeros_like(qp)
            q2 = jnp.concatenate([jnp.where(low_half, qp, zero), jnp.where(low_half, zero, qp)], axis=0)
            scores = [_dot_nt(q2, k_ref[0, :, gl]) for k_ref, _ in sources]
            row_max = functools.reduce(jnp.maximum, [jnp.max(s, axis=-1, keepdims=True) for s in scores])
            o, denom = None, None
            for s, (_, v_ref) in zip(scores, sources):
                p = jnp.exp(s - row_max)
                part = jnp.sum(p, axis=-1, keepdims=True)
                term = _dot(p.astype(BF16), v_ref[0, :, gl])
                o, denom = (term, part) if o is None else (o + term, denom + part)
            o = o / denom
            o_ref[0, rows, j * LANES:(j + 1) * LANES] = jnp.where(low_half, o[:sub], o[sub:]).astype(BF16)


def _attention(q, kv_sources, *, tq):
    b, n, _ = q.shape
    kv_specs, kv_args = [], []
    for k, v in kv_sources:
        for a in (k, v):
            kv_specs.append(pl.BlockSpec((1, a.shape[1], 2 * LANES), lambda bb, i: (bb, 0, 0)))
            kv_args.append(a)
    return pl.pallas_call(
        functools.partial(_attn_kernel, tq=tq),
        out_shape=jax.ShapeDtypeStruct((b, n, Q_WIDTH), BF16),
        grid=(b, n // tq),
        in_specs=[pl.BlockSpec((1, tq, Q_WIDTH), lambda bb, i: (bb, i, 0))] + kv_specs,
        out_specs=pl.BlockSpec((1, tq, Q_WIDTH), lambda bb, i: (bb, i, 0)),
        compiler_params=_params(),
        name="attention",
    )(q, *kv_args)


def _fourier_kernel(x_ref, wc_ref, cn_ref, nsn_ref, o_ref, t_scr, *, n, tr):
    t_scr[...] = _dot(x_ref[0], wc_ref[...].astype(BF16)).astype(BF16)
    scale = 1.0 / math.sqrt(n * FOURIER_GROUP)
    for r in range(n // tr):
        rows = slice(r * tr, (r + 1) * tr)
        acc = (_dot(cn_ref[rows, :].astype(BF16), t_scr[:, 0:FOURIER_WIDTH])
               + _dot(nsn_ref[rows, :].astype(BF16), t_scr[:, FOURIER_WIDTH:2 * FOURIER_WIDTH]))
        o_ref[0, rows, :] = (acc * scale).astype(BF16)


def _fourier(xf, wc, cn, nsn):
    b, n, _ = xf.shape
    tr = min(n, 512)
    return pl.pallas_call(
        functools.partial(_fourier_kernel, n=n, tr=tr),
        out_shape=jax.ShapeDtypeStruct((b, n, FOURIER_WIDTH), BF16),
        grid=(b, 1),
        in_specs=[pl.BlockSpec((1, n, FOURIER_WIDTH), lambda bb, i: (bb, 0, 0)),
                  _fixed_spec((FOURIER_WIDTH, 2 * FOURIER_WIDTH), (0, 0)),
                  _fixed_spec((n, n), (0, 0)),
                  _fixed_spec((n, n), (0, 0))],
        out_specs=pl.BlockSpec((1, n, FOURIER_WIDTH), lambda bb, i: (bb, 0, 0)),
        scratch_shapes=[pltpu.VMEM((n, 2 * FOURIER_WIDTH), BF16)],
        compiler_params=_params(),
        name="fourier",
    )(xf, wc, cn, nsn)


def _mix_kernel(x_ref, m_ref, norm_ref, oa_ref, sc_ref, xf_ref, yp_ref,
                wg01_ref, wg23_ref, wa_ref, wsc_ref, wf_ref, wp_ref, wo_ref, out_ref, y_scr):
    xv = x_ref[0]
    h = _rms_mod(xv, norm_ref[...], m_ref[0:1, :], m_ref[1:2, :]).astype(BF16)
    branches = ((oa_ref, wa_ref), (sc_ref, wsc_ref), (xf_ref, wf_ref), (yp_ref, wp_ref))
    gate_w = (wg01_ref, wg23_ref)
    for j in range(D_MODEL // FF_CHUNK):
        cols = slice(j * FF_CHUNK, (j + 1) * FF_CHUNK)
        y = None
        for bi, (a_ref, w_ref) in enumerate(branches):
            off = (bi % 2) * D_MODEL + j * FF_CHUNK
            gate = 0.5 + 0.5 * jnp.tanh(0.5 * _dot(h, gate_w[bi // 2][:, off:off + FF_CHUNK]))
            term = gate * _dot(a_ref[0], w_ref[:, cols])
            y = term if y is None else y + term
        y_scr[:, cols] = y.astype(BF16)
    out_ref[0] = xv + m_ref[2:3, :] * _dot(y_scr[...], wo_ref[...])


def _mix(xs, prm, layer, mod_row, oa, sc, xf, yp, *, tq):
    b, n, _ = xs.shape
    half_gates = N_BRANCHES * D_MODEL // 2

    def tok(w):
        return pl.BlockSpec((1, tq, w), lambda i, bb: (bb, i, 0))

    return pl.pallas_call(
        _mix_kernel,
        out_shape=jax.ShapeDtypeStruct((b, n, D_MODEL), F32),
        grid=(n // tq, b),
        in_specs=[tok(D_MODEL), _mod_spec(layer, mod_row), _layer_spec((1, D_MODEL), layer),
                  tok(Q_WIDTH), tok(SC_WIDTH), tok(FOURIER_WIDTH), tok(POOL_WIDTH),
                  _layer_spec((D_MODEL, half_gates), layer, (0, OFF_G // half_gates)),
                  _layer_spec((D_MODEL, half_gates), layer, (0, OFF_G // half_gates + 1)),
                  _layer_spec((Q_WIDTH, D_MODEL), layer), _layer_spec((SC_WIDTH, D_MODEL), layer),
                  _layer_spec((FOURIER_WIDTH, D_MODEL), layer), _layer_spec((POOL_WIDTH, D_MODEL), layer),
                  _layer_spec((D_MODEL, D_MODEL), layer)],
        out_specs=tok(D_MODEL),
        scratch_shapes=[pltpu.VMEM((tq, D_MODEL), BF16)],
        compiler_params=_params(),
        name="mix",
    )(xs, prm["mod"], prm["norm1"], oa, sc, xf, yp, prm["w_in"], prm["w_in"],
      prm["w_br_attn"], prm["w_br_sc"], prm["w_br_f"], prm["w_br_p"], prm["w_out"])


def _ffn_kernel(xp_ref, xc_ref, xn_ref, m_ref, norm_ref, wu_ref, cw_ref, wd_ref, fn_ref, out_ref,
                h_scr, hb_scr, u0_scr, u1_scr, acc_scr, *, tq, final):
    norm = norm_ref[...]
    shift = m_ref[3:4, :]
    scale = m_ref[4:5, :]
    _fill_halo_rows(h_scr, xp_ref, xc_ref, xn_ref,
                    lambda xv: _rms_mod(xv, norm, shift, scale), tq)
    hb_scr[...] = h_scr[...].astype(BF16)
    acc_scr[...] = jnp.zeros_like(acc_scr)
    ext = tq + 2 * HALO

    def cols(c, half):
        return slice(half * D_FF + c * FF_CHUNK, half * D_FF + (c + 1) * FF_CHUNK)

    def up(c, u_scr):
        hb = hb_scr[...]
        u_scr[0] = _dot(hb, wu_ref[:, cols(c, 0)])
        u_scr[1] = _dot(hb, wu_ref[:, cols(c, 1)])

    def conv3(u, w):
        full = pltpu.roll(u, 1, 0) * w[0:1, :] + u * w[1:2, :] + pltpu.roll(u, ext - 1, 0) * w[2:3, :]
        return full[HALO:HALO + tq]

    def down(c, u_scr):
        a = conv3(u_scr[0], cw_ref[:, cols(c, 0)])
        bv = conv3(u_scr[1], cw_ref[:, cols(c, 1)])
        half = 0.5 * a
        act = ((half + half * jnp.tanh(half)) * bv).astype(BF16)
        acc_scr[...] += _dot(act, wd_ref[c * FF_CHUNK:(c + 1) * FF_CHUNK, :])

    bufs = (u0_scr, u1_scr)
    up(0, u0_scr)
    for c in range(N_FF_CHUNKS):
        if c + 1 < N_FF_CHUNKS:
            up(c + 1, bufs[(c + 1) % 2])
        down(c, bufs[c % 2])
    out = xc_ref[0] + m_ref[5:6, :] * acc_scr[...]
    if final:
        ms = jnp.mean(out * out, axis=-1, keepdims=True)
        out = out * lax.rsqrt(ms + NORM_EPS) * fn_ref[...]
    out_ref[0] = out


def _ffn(xs, prm, layer, mod_row, *, tq, final):
    b, n, _ = xs.shape
    return pl.pallas_call(
        functools.partial(_ffn_kernel, tq=tq, final=final),
        out_shape=jax.ShapeDtypeStruct((b, n, D_MODEL), F32),
        grid=(n // tq, b),
        in_specs=_halo_specs(n, tq, D_MODEL) + [
            _mod_spec(layer, mod_row),
            _layer_spec((1, D_MODEL), layer),
            _layer_spec((D_MODEL, 2 * D_FF), layer),
            _layer_spec((3, 2 * D_FF), layer),
            _layer_spec((D_FF, D_MODEL), layer),
            _fixed_spec((1, D_MODEL), (0, 0))],
        out_specs=pl.BlockSpec((1, tq, D_MODEL), lambda i, bb: (bb, i, 0)),
        scratch_shapes=[pltpu.VMEM((tq + 2 * HALO, D_MODEL), F32),
                        pltpu.VMEM((tq + 2 * HALO, D_MODEL), BF16),
                        pltpu.VMEM((2, tq + 2 * HALO, FF_CHUNK), F32),
                        pltpu.VMEM((2, tq + 2 * HALO, FF_CHUNK), F32),
                        pltpu.VMEM((tq, D_MODEL), F32)],
        compiler_params=_params(),
        name="ffn",
    )(xs, xs, xs, prm["mod"], prm["norm2"], prm["w_up"], prm["w_conv_ffn"], prm["w_down"], prm["final_norm"])


def _rope_tables(n):
    d = np.arange(LANES)
    hd = d % HEAD_DIM
    axis = hd // (2 * AXIS_PAIRS)
    half = (hd % (2 * AXIS_PAIRS)) // AXIS_PAIRS
    pair = hd % AXIS_PAIRS
    t = np.arange(n)
    pos = np.where(axis[None, :] == 0, (t // GRID_W)[:, None], (t % GRID_W)[:, None]).astype(np.float64)
    inv = ROPE_THETA ** (-np.arange(AXIS_PAIRS, dtype=np.float64) / AXIS_PAIRS)
    ang = pos * inv[pair][None, :]
    sin = np.sin(ang)
    return (jnp.asarray(np.cos(ang), dtype=F32),
            jnp.asarray(np.where(half[None, :] == 0, -sin, sin), dtype=F32))


def _dft_tables(n):
    k = np.arange(n, dtype=np.int64)
    ang = ((k[:, None] * k[None, :]) % n).astype(np.float64) * (2.0 * math.pi / n)
    return jnp.asarray(np.cos(ang), dtype=F32), jnp.asarray(-np.sin(ang), dtype=F32)


def _channel_dft():
    c = np.arange(FOURIER_WIDTH, dtype=np.int64)
    same = (c[:, None] // FOURIER_GROUP) == (c[None, :] // FOURIER_GROUP)
    mc = ((c[:, None] % FOURIER_GROUP) * (c[None, :] % FOURIER_GROUP)) % FOURIER_GROUP
    ang = mc.astype(np.float64) * (2.0 * math.pi / FOURIER_GROUP)
    both = np.concatenate([np.where(same, np.cos(ang), 0.0), np.where(same, np.sin(ang), 0.0)], axis=1)
    return jnp.asarray(both, dtype=F32)


def _head_matrices():
    r = np.arange(QK_WIDTH) // HEAD_DIM
    m = (r[:, None] == np.arange(LANES)[None, :]).astype(np.float32)
    return jnp.asarray(m.astype(BF16)), jnp.asarray(m.T.copy().astype(BF16))


def _block_diag(mats):
    nl, g, a, b = mats.shape
    eye = jnp.eye(g, dtype=mats.dtype)
    return (mats[:, :, :, None, :] * eye[None, :, None, :, None]).reshape(nl, g * a, g * b)


def kernel(x, c, ctx, c_ctx, w_mod, b_mod, norm1, norm2, w_in, conv_sc, qk_gain, pool_mat, pool_scale,
           w_br_attn, w_br_sc, w_br_f, w_br_p, w_out, w_up, w_conv_ffn, w_down, final_norm):
    bsz, n_lat, _ = x.shape
    n_ctx = ctx.shape[1]
    nl = w_in.shape[0]
    tq_lat = min(n_lat, TOKEN_TILE)
    tq_ctx = min(n_ctx, TOKEN_TILE)
    tq_attn = min(n_lat, ATTN_TILE)
    tq_ffn_lat = min(n_lat, FFN_TILE)
    tq_ffn_ctx = min(n_ctx, FFN_TILE)

    mod_rows = 16
    ctx_row = bsz
    cvec = jnp.concatenate([c, c_ctx[None], jnp.zeros((mod_rows - bsz - 1, D_MODEL), F32)], axis=0)
    gain = jnp.concatenate([jnp.tile(qk_gain[:, 0], (1, N_HEADS)) * (HEAD_DIM ** -0.5),
                            jnp.tile(qk_gain[:, 1], (1, N_KV_HEADS))], axis=1)
    head_sum, head_bcast = _head_matrices()
    prm = {
        "mod": _modulation(cvec, w_mod, b_mod),
        "norm1": norm1.reshape(nl, 1, D_MODEL),
        "norm2": norm2.reshape(nl, 1, D_MODEL),
        "w_in": w_in.astype(BF16),
        "head_sum": head_sum,
        "head_bcast": head_bcast,
        "gain": gain.reshape(nl, 1, QK_WIDTH),
        "conv_sc": conv_sc,
        "pool_mat": _block_diag(pool_mat).astype(BF16),
        "pool_scale": pool_scale.reshape(nl, 1, POOL_WIDTH),
        "w_br_attn": w_br_attn.astype(BF16),
        "w_br_sc": w_br_sc.astype(BF16),
        "w_br_f": w_br_f.astype(BF16),
        "w_br_p": w_br_p.astype(BF16),
        "w_out": w_out.astype(BF16),
        "w_up": w_up.astype(BF16),
        "w_conv_ffn": w_conv_ffn,
        "w_down": w_down.astype(BF16),
        "final_norm": final_norm.reshape(1, D_MODEL),
    }
    cos_t, sin_t = _rope_tables(n_lat)
    wc = _channel_dft()
    dft_lat = _dft_tables(n_lat)
    dft_ctx = _dft_tables(n_ctx)

    lat, cx = x, ctx
    for l in range(nl):
        last = l == nl - 1
        if last:
            kc, vc = _kv_proj(cx, prm, l, ctx_row, tq=tq_ctx)
        else:
            qc, kc, vc, scc, xfc, ypc = _in_proj(cx, prm, l, ctx_row, cos_t, sin_t, tq=tq_ctx, rope=False)
        ql, kl, vl, scl, xfl, ypl = _in_proj(lat, prm, l, None, cos_t, sin_t, tq=tq_lat, rope=True)
        oa = _attention(ql, [(kl, vl), (kc, vc)], tq=tq_attn)
        xfour = _fourier(xfl, wc, *dft_lat)
        lat = _mix(lat, prm, l, None, oa, scl, xfour, ypl, tq=tq_lat)
        lat = _ffn(lat, prm, l, None, tq=tq_ffn_lat, final=last)

        if not last:
            oac = _attention(qc, [(kc, vc)], tq=tq_ctx)
            xfourc = _fourier(xfc, wc, *dft_ctx)
            cx = _mix(cx, prm, l, ctx_row, oac, scc, xfourc, ypc, tq=tq_ctx)
            cx = _ffn(cx, prm, l, ctx_row, tq=tq_ffn_ctx, final=False)
    return lat
```

```python
import functools
import math

import numpy as np
import jax
import jax.numpy as jnp
from jax import lax
from jax.experimental import pallas as pl
from jax.experimental.pallas import tpu as pltpu

D_MODEL = 1024
DEPTH = 2
GRID_W = 64
NORM_EPS = 1e-6
N_HEADS = 8
N_KV_HEADS = 2
HEAD_DIM = 64
Q_WIDTH = N_HEADS * HEAD_DIM
KV_WIDTH = N_KV_HEADS * HEAD_DIM
QK_WIDTH = Q_WIDTH + KV_WIDTH
ROPE_THETA = 10000.0
AXIS_PAIRS = HEAD_DIM // 4
SC_WIDTH = 256
FOURIER_GROUPS = 4
FOURIER_WIDTH = 256
FOURIER_GROUP = FOURIER_WIDTH // FOURIER_GROUPS
POOL_WINDOWS = (2, 4, 8, 16)
POOL_WIDTH = 256
POOL_GROUP = POOL_WIDTH // len(POOL_WINDOWS)
N_BRANCHES = 4
D_FF = 2816

OFF_K = Q_WIDTH
OFF_V = OFF_K + KV_WIDTH
OFF_SC = OFF_V + KV_WIDTH
OFF_F = OFF_SC + 3 * SC_WIDTH
OFF_P = OFF_F + FOURIER_WIDTH
OFF_G = OFF_P + POOL_WIDTH
IN_WIDTH = OFF_G + N_BRANCHES * D_MODEL

LANES = 128
HALO = 8
FF_CHUNK = 256
N_FF_CHUNKS = D_FF // FF_CHUNK
ATTN_SUB = 256
TOKEN_TILE = 512
IN_PROJ_BATCH = 2
MIX_BATCH = 2
FFN_BATCH = 4
ATTN_TILE = 1024
FFN_TILE = 256
N_MOD = 6
VMEM_LIMIT = 62 * 1024 * 1024

F32 = jnp.float32
BF16 = jnp.bfloat16


def _dot(a, b):
    return jnp.dot(a, b, preferred_element_type=F32)


def _dot_nt(a, b):
    return lax.dot_general(a, b, (((1,), (1,)), ((), ())), preferred_element_type=F32)


def _fixed_spec(block, index):
    return pl.BlockSpec(block, lambda *_: index, pipeline_mode=pl.Buffered(1))


def _layer_spec(tail, layer, tail_index=None):
    tail_index = (0,) * len(tail) if tail_index is None else tail_index
    return _fixed_spec((None,) + tuple(tail), (layer,) + tuple(tail_index))


def _mod_spec(layer, row):
    if row is None:
        return pl.BlockSpec((None, None, N_MOD, D_MODEL), lambda i, b: (layer, b, 0, 0))
    return _fixed_spec((None, None, N_MOD, D_MODEL), (layer, row, 0, 0))


def _mod_spec_multi(layer, row, nb):
    if row is None:
        return pl.BlockSpec((None, nb, N_MOD, D_MODEL), lambda i, b: (layer, b, 0, 0))
    return _fixed_spec((None, 1, N_MOD, D_MODEL), (layer, row, 0, 0))


def _params():
    return pltpu.CompilerParams(dimension_semantics=("arbitrary", "arbitrary"),
                                vmem_limit_bytes=VMEM_LIMIT)


def _rms_mod(xv, norm, shift, scale):
    ms = jnp.mean(xv * xv, axis=-1, keepdims=True)
    y = xv * lax.rsqrt(ms + NORM_EPS) * norm
    return y * (1.0 + scale) + shift


def _halo_specs(n, tq, width, nb=1):
    nblk = n // HALO
    per = tq // HALO
    prev = pl.BlockSpec((nb, HALO, width), lambda i, b: (b, jnp.maximum(i * per - 1, 0), 0))
    cur = pl.BlockSpec((nb, tq, width), lambda i, b: (b, i, 0))
    nxt = pl.BlockSpec((nb, HALO, width), lambda i, b: (b, jnp.minimum((i + 1) * per, nblk - 1), 0))
    return [prev, cur, nxt]


def _fill_halo_rows(h_scr, xp_ref, xc_ref, xn_ref, fn, tq):
    i = pl.program_id(0)
    last = pl.num_programs(0) - 1
    h_scr[HALO:HALO + tq, :] = fn(xc_ref[0])
    h_scr[0:HALO, :] = jnp.where(i > 0, fn(xp_ref[0]), 0.0)
    h_scr[HALO + tq:2 * HALO + tq, :] = jnp.where(i < last, fn(xn_ref[0]), 0.0)


def _split_bf16(a):
    hi = a.astype(BF16)
    return hi, (a - hi.astype(F32)).astype(BF16)


def _head_rms(z, sum_ref, bcast_ref, gain):
    sq_hi, sq_lo = _split_bf16(z * z)
    ss = _dot(sq_hi, sum_ref[...]) + _dot(sq_lo, sum_ref[...])
    r_hi, r_lo = _split_bf16(lax.rsqrt(ss * (1.0 / HEAD_DIM) + NORM_EPS))
    return z * (_dot(r_hi, bcast_ref[...]) + _dot(r_lo, bcast_ref[...])) * gain


def _dup_heads(a, low_half):
    sw = pltpu.roll(a, HEAD_DIM, 1)
    return jnp.where(low_half, a, sw), jnp.where(low_half, sw, a)


def _mod_kernel(c_ref, w_ref, b_ref, o_ref):
    cv = c_ref[...]
    a = (cv * jax.nn.sigmoid(cv)).astype(BF16)
    o_ref[0] = _dot(a, w_ref[0].astype(BF16)) + b_ref[0]


def _modulation(cvec, w_mod, b_mod):
    rows = cvec.shape[0]
    nl = w_mod.shape[0]
    out = pl.pallas_call(
        _mod_kernel,
        out_shape=jax.ShapeDtypeStruct((nl, rows, N_MOD * D_MODEL), F32),
        grid=(nl, N_MOD),
        in_specs=[pl.BlockSpec((rows, D_MODEL), lambda l, j: (0, 0)),
                  pl.BlockSpec((1, D_MODEL, D_MODEL), lambda l, j: (l, 0, j)),
                  pl.BlockSpec((1, 1, D_MODEL), lambda l, j: (l, 0, j))],
        out_specs=pl.BlockSpec((1, rows, D_MODEL), lambda l, j: (l, 0, j)),
        compiler_params=_params(),
        name="adaln_mod",
    )(cvec, w_mod, b_mod.reshape(nl, 1, N_MOD * D_MODEL))
    return out.reshape(nl, rows, N_MOD, D_MODEL)


def _in_proj_kernel(xp_ref, xc_ref, xn_ref, m_ref, norm_ref, w_ref, hs_ref, hb_ref, gain_ref, cos_ref, sin_ref,
                    conv_ref, pm_ref, ps_ref,
                    q_ref, k_ref, v_ref, sc_ref, xf_ref, yp_ref,
                    h_scr, z_scr, *, n, tq, rope, nb):
    i = pl.program_id(0)
    norm = norm_ref[...]

    def norm_stage(e):
        m = m_ref.at[min(e, m_ref.shape[0] - 1)]
        shift, scale = m[0:1, :], m[1:2, :]
        _fill_halo_rows(h_scr.at[e], xp_ref.at[e:e + 1], xc_ref.at[e:e + 1], xn_ref.at[e:e + 1],
                        lambda xv: _rms_mod(xv, norm, shift, scale), tq)

    norm_stage(0)
    for e in range(nb):
        z_scr[e] = _dot(h_scr[e].astype(BF16), w_ref[...])
        if e + 1 < nb:
            norm_stage(e + 1)
        _local_mixers(e, z_scr.at[e], hs_ref, hb_ref, gain_ref, cos_ref, sin_ref, conv_ref, pm_ref, ps_ref,
                      q_ref, k_ref, v_ref, sc_ref, xf_ref, yp_ref, i=i, n=n, tq=tq, rope=rope)


def _local_mixers(e, z_scr, hs_ref, hb_ref, gain_ref, cos_ref, sin_ref, conv_ref, pm_ref, ps_ref,
                  q_ref, k_ref, v_ref, sc_ref, xf_ref, yp_ref, *, i, n, tq, rope):
    lane = lax.broadcasted_iota(jnp.int32, (tq, LANES), 1)
    low_half = lane < HEAD_DIM
    rows = slice(HALO, HALO + tq)

    y = _head_rms(z_scr[rows, 0:QK_WIDTH], hs_ref, hb_ref, gain_ref[...])
    second_half = (lane & AXIS_PAIRS) != 0
    for c in range(QK_WIDTH // LANES):
        yc = y[:, c * LANES:(c + 1) * LANES]
        if rope:
            partner = jnp.where(second_half, pltpu.roll(yc, AXIS_PAIRS, 1),
                                pltpu.roll(yc, LANES - AXIS_PAIRS, 1))
            yc = yc * cos_ref[...] + partner * sin_ref[...]
        if c < Q_WIDTH // LANES:
            q_ref[e, :, c * LANES:(c + 1) * LANES] = yc.astype(BF16)
        else:
            k0, k1 = _dup_heads(yc, low_half)
            k_ref[e, :, 0:LANES] = k0.astype(BF16)
            k_ref[e, :, LANES:2 * LANES] = k1.astype(BF16)
    v0, v1 = _dup_heads(z_scr[rows, OFF_V:OFF_SC], low_half)
    v_ref[e, :, 0:LANES] = v0.astype(BF16)
    v_ref[e, :, LANES:2 * LANES] = v1.astype(BF16)

    def u_at(off):
        r = slice(HALO + off, HALO + off + tq)
        return z_scr[r, OFF_SC + SC_WIDTH:OFF_SC + 2 * SC_WIDTH] * z_scr[r, OFF_SC + 2 * SC_WIDTH:OFF_F]

    conv = u_at(-1) * conv_ref[0:1, :] + u_at(0) * conv_ref[1:2, :] + u_at(1) * conv_ref[2:3, :]
    sc_ref[e] = (z_scr[rows, OFF_SC:OFF_SC + SC_WIDTH] * conv).astype(BF16)

    xf_ref[e] = z_scr[rows, OFF_F:OFF_P].astype(BF16)

    t = i * tq + lax.broadcasted_iota(jnp.int32, (tq, LANES), 0)

    def p_at(off, c):
        return z_scr[HALO + off:HALO + off + tq, OFF_P + c * LANES:OFF_P + (c + 1) * LANES]

    def window_mean(total, w_low, w_high):
        w = jnp.where(low_half, w_low, w_high)
        left, right = (w - 1) // 2, w // 2
        cnt = jnp.minimum(t + right + 1, n) - jnp.maximum(t - left, 0)
        return total / cnt.astype(F32)

    p0 = p_at(0, 0)
    s2 = p0 + p_at(1, 0)
    s4 = s2 + p_at(-1, 0) + p_at(2, 0)
    pooled0 = window_mean(jnp.where(low_half, s2, s4), POOL_WINDOWS[0], POOL_WINDOWS[1]) - p0
    p1 = p_at(0, 1)
    s8 = p1
    for off in (-3, -2, -1, 1, 2, 3, 4):
        s8 = s8 + p_at(off, 1)
    s16 = s8
    for off in (-7, -6, -5, -4, 5, 6, 7, 8):
        s16 = s16 + p_at(off, 1)
    pooled1 = window_mean(jnp.where(low_half, s8, s16), POOL_WINDOWS[2], POOL_WINDOWS[3]) - p1
    pooled = jnp.concatenate([pooled0, pooled1], axis=1).astype(BF16)
    yp_ref[e] = (_dot(pooled, pm_ref[...]) * ps_ref[...]).astype(BF16)


def _in_proj(xs, prm, layer, mod_row, cos_t, sin_t, *, tq, rope):
    b, n, _ = xs.shape
    out_w = (Q_WIDTH, 2 * LANES, 2 * LANES, SC_WIDTH, FOURIER_WIDTH, POOL_WIDTH)
    nb = IN_PROJ_BATCH
    kern = functools.partial(_in_proj_kernel, n=n, tq=tq, rope=rope, nb=nb)
    return pl.pallas_call(
        kern,
        out_shape=[jax.ShapeDtypeStruct((b, n, w), BF16) for w in out_w],
        grid=(n // tq, b // nb),
        in_specs=_halo_specs(n, tq, D_MODEL, nb) + [
            _mod_spec_multi(layer, mod_row, nb),
            _layer_spec((1, D_MODEL), layer),
            _layer_spec((D_MODEL, OFF_G), layer),
            _fixed_spec((QK_WIDTH, LANES), (0, 0)),
            _fixed_spec((LANES, QK_WIDTH), (0, 0)),
            _layer_spec((1, QK_WIDTH), layer),
            pl.BlockSpec((tq, LANES), lambda i, bb: (i, 0)),
            pl.BlockSpec((tq, LANES), lambda i, bb: (i, 0)),
            _layer_spec((3, SC_WIDTH), layer),
            _layer_spec((POOL_WIDTH, POOL_WIDTH), layer),
            _layer_spec((1, POOL_WIDTH), layer),
        ],
        out_specs=[pl.BlockSpec((nb, tq, w), lambda i, bb: (bb, i, 0)) for w in out_w],
        scratch_shapes=[pltpu.VMEM((nb, tq + 2 * HALO, D_MODEL), F32),
                        pltpu.VMEM((nb, tq + 2 * HALO, OFF_G), F32)],
        compiler_params=_params(),
        name="in_proj",
    )(xs, xs, xs, prm["mod"], prm["norm1"], prm["w_in"], prm["head_sum"], prm["head_bcast"], prm["gain"],
      cos_t, sin_t,
      prm["conv_sc"], prm["pool_mat"], prm["pool_scale"])


def _kv_proj_kernel(x_ref, m_ref, norm_ref, w_ref, hs_ref, hb_ref, gain_ref, k_ref, v_ref, *, tq):
    h = _rms_mod(x_ref[0], norm_ref[...], m_ref[0:1, :], m_ref[1:2, :]).astype(BF16)
    z = _dot(h, w_ref[...])
    low_half = lax.broadcasted_iota(jnp.int32, (tq, LANES), 1) < HEAD_DIM
    k0, k1 = _dup_heads(_head_rms(z[:, 0:KV_WIDTH], hs_ref, hb_ref, gain_ref[...]), low_half)
    k_ref[0, :, 0:LANES] = k0.astype(BF16)
    k_ref[0, :, LANES:2 * LANES] = k1.astype(BF16)
    v0, v1 = _dup_heads(z[:, KV_WIDTH:2 * KV_WIDTH], low_half)
    v_ref[0, :, 0:LANES] = v0.astype(BF16)
    v_ref[0, :, LANES:2 * LANES] = v1.astype(BF16)


def _kv_proj(xs, prm, layer, mod_row, *, tq):
    b, n, _ = xs.shape
    kv_cols = 2 * KV_WIDTH
    return pl.pallas_call(
        functools.partial(_kv_proj_kernel, tq=tq),
        out_shape=[jax.ShapeDtypeStruct((b, n, 2 * LANES), BF16)] * 2,
        grid=(n // tq, b),
        in_specs=[pl.BlockSpec((1, tq, D_MODEL), lambda i, bb: (bb, i, 0)),
                  _mod_spec(layer, mod_row),
                  _layer_spec((1, D_MODEL), layer),
                  _layer_spec((D_MODEL, kv_cols), layer, (0, OFF_K // kv_cols)),
                  _fixed_spec((KV_WIDTH, LANES), (0, 0)),
                  _fixed_spec((LANES, KV_WIDTH), (0, 0)),
                  _layer_spec((1, KV_WIDTH), layer, (0, Q_WIDTH // KV_WIDTH))],
        out_specs=[pl.BlockSpec((1, tq, 2 * LANES), lambda i, bb: (bb, i, 0))] * 2,
        compiler_params=_params(),
        name="kv_proj",
    )(xs, prm["mod"], prm["norm1"], prm["w_in"], prm["head_sum"], prm["head_bcast"], prm["gain"])


def _attn_kernel(q_ref, *refs, tq):
    *kv_refs, o_ref = refs
    sub = min(tq, ATTN_SUB)
    lane = lax.broadcasted_iota(jnp.int32, (sub, LANES), 1)
    low_half = lane < HEAD_DIM
    sources = [(kv_refs[2 * a], kv_refs[2 * a + 1]) for a in range(len(kv_refs) // 2)]
    for j in range(Q_WIDTH // LANES):
        g = (2 * j) // (N_HEADS // N_KV_HEADS)
        gl = slice(g * LANES, (g + 1) * LANES)
        for r in range(tq // sub):
            rows = slice(r * sub, (r + 1) * sub)
            qp = q_ref[0, rows, j * LANES:(j + 1) * LANES]
            zero = jnp.zeros_like(qp)
            q2 = jnp.concatenate([jnp.where(low_half, qp, zero), jnp.where(low_half, zero, qp)], axis=0)
            scores = [_dot_nt(q2, k_ref[0, :, gl]) for k_ref, _ in sources]
            row_max = functools.reduce(jnp.maximum, [jnp.max(s, axis=-1, keepdims=True) for s in scores])
            o, denom = None, None
            for s, (_, v_ref) in zip(scores, sources):
                p = jnp.exp(s - row_max)
                part = jnp.sum(p, axis=-1, keepdims=True)
                term = _dot(p.astype(BF16), v_ref[0, :, gl])
                o, denom = (term, part) if o is None else (o + term, denom + part)
            o = o / denom
            o_ref[0, rows, j * LANES:(j + 1) * LANES] = jnp.where(low_half, o[:sub], o[sub:]).astype(BF16)


def _attention(q, kv_sources, *, tq):
    b, n, _ = q.shape
    kv_specs, kv_args = [], []
    for k, v in kv_sources:
        for a in (k, v):
            kv_specs.append(pl.BlockSpec((1, a.shape[1], 2 * LANES), lambda bb, i: (bb, 0, 0)))
            kv_args.append(a)
    return pl.pallas_call(
        functools.partial(_attn_kernel, tq=tq),
        out_shape=jax.ShapeDtypeStruct((b, n, Q_WIDTH), BF16),
        grid=(b, n // tq),
        in_specs=[pl.BlockSpec((1, tq, Q_WIDTH), lambda bb, i: (bb, i, 0))] + kv_specs,
        out_specs=pl.BlockSpec((1, tq, Q_WIDTH), lambda bb, i: (bb, i, 0)),
        compiler_params=_params(),
        name="attention",
    )(q, *kv_args)


def _fourier_kernel(x_ref, wc_ref, cn_ref, nsn_ref, o_ref, t_scr, *, n, tr):
    t_scr[...] = _dot(x_ref[0], wc_ref[...].astype(BF16)).astype(BF16)
    scale = 1.0 / math.sqrt(n * FOURIER_GROUP)
    for r in range(n // tr):
        rows = slice(r * tr, (r + 1) * tr)
        acc = (_dot(cn_ref[rows, :].astype(BF16), t_scr[:, 0:FOURIER_WIDTH])
               + _dot(nsn_ref[rows, :].astype(BF16), t_scr[:, FOURIER_WIDTH:2 * FOURIER_WIDTH]))
        o_ref[0, rows, :] = (acc * scale).astype(BF16)


def _fourier(xf, wc, cn, nsn):
    b, n, _ = xf.shape
    tr = min(n, 512)
    return pl.pallas_call(
        functools.partial(_fourier_kernel, n=n, tr=tr),
        out_shape=jax.ShapeDtypeStruct((b, n, FOURIER_WIDTH), BF16),
        grid=(b, 1),
        in_specs=[pl.BlockSpec((1, n, FOURIER_WIDTH), lambda bb, i: (bb, 0, 0)),
                  _fixed_spec((FOURIER_WIDTH, 2 * FOURIER_WIDTH), (0, 0)),
                  _fixed_spec((n, n), (0, 0)),
                  _fixed_spec((n, n), (0, 0))],
        out_specs=pl.BlockSpec((1, n, FOURIER_WIDTH), lambda bb, i: (bb, 0, 0)),
        scratch_shapes=[pltpu.VMEM((n, 2 * FOURIER_WIDTH), BF16)],
        compiler_params=_params(),
        name="fourier",
    )(xf, wc, cn, nsn)


def _mix_kernel(x_ref, m_ref, norm_ref, oa_ref, sc_ref, xf_ref, yp_ref,
                wg01_ref, wg23_ref, wa_ref, wsc_ref, wf_ref, wp_ref, wo_ref, out_ref, h_scr, y_scr, *, nb):
    branches = ((oa_ref, wa_ref), (sc_ref, wsc_ref), (xf_ref, wf_ref), (yp_ref, wp_ref))
    gate_w = (wg01_ref, wg23_ref)
    norm = norm_ref[...]

    def mod(e):
        return m_ref.at[min(e, m_ref.shape[0] - 1)]

    def norm_stage(e):
        m = mod(e)
        h_scr[e] = _rms_mod(x_ref[e], norm, m[0:1, :], m[1:2, :]).astype(BF16)

    norm_stage(0)
    for e in range(nb):
        h = h_scr[e]
        for j in range(D_MODEL // FF_CHUNK):
            cols = slice(j * FF_CHUNK, (j + 1) * FF_CHUNK)
            y = None
            for bi, (a_ref, w_ref) in enumerate(branches):
                off = (bi % 2) * D_MODEL + j * FF_CHUNK
                gate = 0.5 + 0.5 * jnp.tanh(0.5 * _dot(h, gate_w[bi // 2][:, off:off + FF_CHUNK]))
                term = gate * _dot(a_ref[e], w_ref[:, cols])
                y = term if y is None else y + term
            y_scr[e, :, cols] = y.astype(BF16)
            if j == 0 and e + 1 < nb:
                norm_stage(e + 1)
        out_ref[e] = x_ref[e] + mod(e)[2:3, :] * _dot(y_scr[e], wo_ref[...])


def _mix(xs, prm, layer, mod_row, oa, sc, xf, yp, *, tq):
    b, n, _ = xs.shape
    nb = MIX_BATCH
    half_gates = N_BRANCHES * D_MODEL // 2

    def tok(w):
        return pl.BlockSpec((nb, tq, w), lambda i, bb: (bb, i, 0))

    return pl.pallas_call(
        functools.partial(_mix_kernel, nb=nb),
        out_shape=jax.ShapeDtypeStruct((b, n, D_MODEL), F32),
        grid=(n // tq, b // nb),
        in_specs=[tok(D_MODEL), _mod_spec_multi(layer, mod_row, nb), _layer_spec((1, D_MODEL), layer),
                  tok(Q_WIDTH), tok(SC_WIDTH), tok(FOURIER_WIDTH), tok(POOL_WIDTH),
                  _layer_spec((D_MODEL, half_gates), layer, (0, OFF_G // half_gates)),
                  _layer_spec((D_MODEL, half_gates), layer, (0, OFF_G // half_gates + 1)),
                  _layer_spec((Q_WIDTH, D_MODEL), layer), _layer_spec((SC_WIDTH, D_MODEL), layer),
                  _layer_spec((FOURIER_WIDTH, D_MODEL), layer), _layer_spec((POOL_WIDTH, D_MODEL), layer),
                  _layer_spec((D_MODEL, D_MODEL), layer)],
        out_specs=tok(D_MODEL),
        scratch_shapes=[pltpu.VMEM((nb, tq, D_MODEL), BF16), pltpu.VMEM((nb, tq, D_MODEL), BF16)],
        compiler_params=_params(),
        name="mix",
    )(xs, prm["mod"], prm["norm1"], oa, sc, xf, yp, prm["w_in"], prm["w_in"],
      prm["w_br_attn"], prm["w_br_sc"], prm["w_br_f"], prm["w_br_p"], prm["w_out"])


def _ffn_kernel(xp_ref, xc_ref, xn_ref, m_ref, norm_ref, wu_ref, cw_ref, wd_ref, fn_ref, out_ref,
                h_scr, hb_scr, u0_scr, u1_scr, *, tq, final, nb):
    norm = norm_ref[...]
    ext = tq + 2 * HALO

    def mod(e):
        return m_ref.at[min(e, m_ref.shape[0] - 1)]

    def cols(c, half):
        return slice(half * D_FF + c * FF_CHUNK, half * D_FF + (c + 1) * FF_CHUNK)

    def norm_stage(e):
        m = mod(e)
        shift, scale = m[3:4, :], m[4:5, :]
        _fill_halo_rows(h_scr.at[e], xp_ref.at[e:e + 1], xc_ref.at[e:e + 1], xn_ref.at[e:e + 1],
                        lambda xv: _rms_mod(xv, norm, shift, scale), tq)
        hb_scr[e] = h_scr[e].astype(BF16)

    def up(e, c, u_scr):
        hb = hb_scr[e]
        u_scr[0] = _dot(hb, wu_ref[:, cols(c, 0)])
        u_scr[1] = _dot(hb, wu_ref[:, cols(c, 1)])

    def conv3(u, w):
        full = pltpu.roll(u, 1, 0) * w[0:1, :] + u * w[1:2, :] + pltpu.roll(u, ext - 1, 0) * w[2:3, :]
        return full[HALO:HALO + tq]

    def down(c, u_scr):
        a = conv3(u_scr[0], cw_ref[:, cols(c, 0)])
        bv = conv3(u_scr[1], cw_ref[:, cols(c, 1)])
        half = 0.5 * a
        act = ((half + half * jnp.tanh(half)) * bv).astype(BF16)
        return _dot(act, wd_ref[c * FF_CHUNK:(c + 1) * FF_CHUNK, :])

    bufs = (u0_scr, u1_scr)
    units = [(e, c) for e in range(nb) for c in range(N_FF_CHUNKS)]
    norm_stage(0)
    up(0, 0, u0_scr)
    acc = None
    for k, (e, c) in enumerate(units):
        if k + 1 < len(units):
            e_next, c_next = units[k + 1]
            if c_next == 0:
                norm_stage(e_next)
            up(e_next, c_next, bufs[(k + 1) % 2])
        d = down(c, bufs[k % 2])
        acc = d if c == 0 else acc + d
        if c == N_FF_CHUNKS - 1:
            out = xc_ref[e] + mod(e)[5:6, :] * acc
            if final:
                ms = jnp.mean(out * out, axis=-1, keepdims=True)
                out = out * lax.rsqrt(ms + NORM_EPS) * fn_ref[...]
            out_ref[e] = out


def _ffn(xs, prm, layer, mod_row, *, tq, final):
    b, n, _ = xs.shape
    nb = FFN_BATCH
    return pl.pallas_call(
        functools.partial(_ffn_kernel, tq=tq, final=final, nb=nb),
        out_shape=jax.ShapeDtypeStruct((b, n, D_MODEL), F32),
        grid=(n // tq, b // nb),
        in_specs=_halo_specs(n, tq, D_MODEL, nb) + [
            _mod_spec_multi(layer, mod_row, nb),
            _layer_spec((1, D_MODEL), layer),
            _layer_spec((D_MODEL, 2 * D_FF), layer),
            _layer_spec((3, 2 * D_FF), layer),
            _layer_spec((D_FF, D_MODEL), layer),
            _fixed_spec((1, D_MODEL), (0, 0))],
        out_specs=pl.BlockSpec((nb, tq, D_MODEL), lambda i, bb: (bb, i, 0)),
        scratch_shapes=[pltpu.VMEM((nb, tq + 2 * HALO, D_MODEL), F32),
                        pltpu.VMEM((nb, tq + 2 * HALO, D_MODEL), BF16),
                        pltpu.VMEM((2, tq + 2 * HALO, FF_CHUNK), F32),
                        pltpu.VMEM((2, tq + 2 * HALO, FF_CHUNK), F32)],
        compiler_params=_params(),
        name="ffn",
    )(xs, xs, xs, prm["mod"], prm["norm2"], prm["w_up"], prm["w_conv_ffn"], prm["w_down"], prm["final_norm"])


def _rope_tables(n):
    d = np.arange(LANES)
    hd = d % HEAD_DIM
    axis = hd // (2 * AXIS_PAIRS)
    half = (hd % (2 * AXIS_PAIRS)) // AXIS_PAIRS
    pair = hd % AXIS_PAIRS
    t = np.arange(n)
    pos = np.where(axis[None, :] == 0, (t // GRID_W)[:, None], (t % GRID_W)[:, None]).astype(np.float64)
    inv = ROPE_THETA ** (-np.arange(AXIS_PAIRS, dtype=np.float64) / AXIS_PAIRS)
    ang = pos * inv[pair][None, :]
    sin = np.sin(ang)
    return (jnp.asarray(np.cos(ang), dtype=F32),
            jnp.asarray(np.where(half[None, :] == 0, -sin, sin), dtype=F32))


def _dft_tables(n):
    k = np.arange(n, dtype=np.int64)
    ang = ((k[:, None] * k[None, :]) % n).astype(np.float64) * (2.0 * math.pi / n)
    return jnp.asarray(np.cos(ang), dtype=F32), jnp.asarray(-np.sin(ang), dtype=F32)


def _channel_dft():
    c = np.arange(FOURIER_WIDTH, dtype=np.int64)
    same = (c[:, None] // FOURIER_GROUP) == (c[None, :] // FOURIER_GROUP)
    mc = ((c[:, None] % FOURIER_GROUP) * (c[None, :] % FOURIER_GROUP)) % FOURIER_GROUP
    ang = mc.astype(np.float64) * (2.0 * math.pi / FOURIER_GROUP)
    both = np.concatenate([np.where(same, np.cos(ang), 0.0), np.where(same, np.sin(ang), 0.0)], axis=1)
    return jnp.asarray(both, dtype=F32)


def _head_matrices():
    r = np.arange(QK_WIDTH) // HEAD_DIM
    m = (r[:, None] == np.arange(LANES)[None, :]).astype(np.float32)
    return jnp.asarray(m.astype(BF16)), jnp.asarray(m.T.copy().astype(BF16))


def _block_diag(mats):
    nl, g, a, b = mats.shape
    eye = jnp.eye(g, dtype=mats.dtype)
    return (mats[:, :, :, None, :] * eye[None, :, None, :, None]).reshape(nl, g * a, g * b)


def kernel(x, c, ctx, c_ctx, w_mod, b_mod, norm1, norm2, w_in, conv_sc, qk_gain, pool_mat, pool_scale,
           w_br_attn, w_br_sc, w_br_f, w_br_p, w_out, w_up, w_conv_ffn, w_down, final_norm):
    bsz, n_lat, _ = x.shape
    n_ctx = ctx.shape[1]
    nl = w_in.shape[0]
    tq_lat = min(n_lat, TOKEN_TILE)
    tq_ctx = min(n_ctx, TOKEN_TILE)
    tq_attn = min(n_lat, ATTN_TILE)
    tq_ffn_lat = min(n_lat, FFN_TILE)
    tq_ffn_ctx = min(n_ctx, FFN_TILE)

    mod_rows = 16
    ctx_row = bsz
    cvec = jnp.concatenate([c, c_ctx[None], jnp.zeros((mod_rows - bsz - 1, D_MODEL), F32)], axis=0)
    gain = jnp.concatenate([jnp.tile(qk_gain[:, 0], (1, N_HEADS)) * (HEAD_DIM ** -0.5),
                            jnp.tile(qk_gain[:, 1], (1, N_KV_HEADS))], axis=1)
    head_sum, head_bcast = _head_matrices()
    prm = {
        "mod": _modulation(cvec, w_mod, b_mod),
        "norm1": norm1.reshape(nl, 1, D_MODEL),
        "norm2": norm2.reshape(nl, 1, D_MODEL),
        "w_in": w_in.astype(BF16),
        "head_sum": head_sum,
        "head_bcast": head_bcast,
        "gain": gain.reshape(nl, 1, QK_WIDTH),
        "conv_sc": conv_sc,
        "pool_mat": _block_diag(pool_mat).astype(BF16),
        "pool_scale": pool_scale.reshape(nl, 1, POOL_WIDTH),
        "w_br_attn": w_br_attn.astype(BF16),
        "w_br_sc": w_br_sc.astype(BF16),
        "w_br_f": w_br_f.astype(BF16),
        "w_br_p": w_br_p.astype(BF16),
        "w_out": w_out.astype(BF16),
        "w_up": w_up.astype(BF16),
        "w_conv_ffn": w_conv_ffn,
        "w_down": w_down.astype(BF16),
        "final_norm": final_norm.reshape(1, D_MODEL),
    }
    cos_t, sin_t = _rope_tables(n_lat)
    wc = _channel_dft()
    dft_lat = _dft_tables(n_lat)
    dft_ctx = _dft_tables(n_ctx)

    lat, cx = x, ctx
    for l in range(nl):
        last = l == nl - 1
        if last:
            kc, vc = _kv_proj(cx, prm, l, ctx_row, tq=tq_ctx)
        else:
            qc, kc, vc, scc, xfc, ypc = _in_proj(cx, prm, l, ctx_row, cos_t, sin_t, tq=tq_ctx, rope=False)
        ql, kl, vl, scl, xfl, ypl = _in_proj(lat, prm, l, None, cos_t, sin_t, tq=tq_lat, rope=True)
        oa = _attention(ql, [(kl, vl), (kc, vc)], tq=tq_attn)
        xfour = _fourier(xfl, wc, *dft_lat)
        lat = _mix(lat, prm, l, None, oa, scl, xfour, ypl, tq=tq_lat)
        lat = _ffn(lat, prm, l, None, tq=tq_ffn_lat, final=last)

        if not last:
            oac = _attention(qc, [(kc, vc)], tq=tq_ctx)
            xfourc = _fourier(xfc, wc, *dft_ctx)
            cx = _mix(cx, prm, l, ctx_row, oac, scc, xfourc, ypc, tq=tq_ctx)
            cx = _ffn(cx, prm, l, ctx_row, tq=tq_ffn_ctx, final=False)
    return lat
```

```python
import functools
import math

import numpy as np
import jax
import jax.numpy as jnp
from jax import lax
from jax.experimental import pallas as pl
from jax.experimental.pallas import tpu as pltpu

D_MODEL = 1024
DEPTH = 2
GRID_W = 64
NORM_EPS = 1e-6
N_HEADS = 8
N_KV_HEADS = 2
HEAD_DIM = 64
Q_WIDTH = N_HEADS * HEAD_DIM
KV_WIDTH = N_KV_HEADS * HEAD_DIM
QK_WIDTH = Q_WIDTH + KV_WIDTH
ROPE_THETA = 10000.0
AXIS_PAIRS = HEAD_DIM // 4
SC_WIDTH = 256
FOURIER_GROUPS = 4
FOURIER_WIDTH = 256
FOURIER_GROUP = FOURIER_WIDTH // FOURIER_GROUPS
POOL_WINDOWS = (2, 4, 8, 16)
POOL_WIDTH = 256
POOL_GROUP = POOL_WIDTH // len(POOL_WINDOWS)
N_BRANCHES = 4
D_FF = 2816

OFF_K = Q_WIDTH
OFF_V = OFF_K + KV_WIDTH
OFF_SC = OFF_V + KV_WIDTH
OFF_F = OFF_SC + 3 * SC_WIDTH
OFF_P = OFF_F + FOURIER_WIDTH
OFF_G = OFF_P + POOL_WIDTH
IN_WIDTH = OFF_G + N_BRANCHES * D_MODEL

LANES = 128
HALO = 8
FF_CHUNK = 256
N_FF_CHUNKS = D_FF // FF_CHUNK
ATTN_SUB = 256
TOKEN_TILE = 512
IN_PROJ_BATCH = 2
MIX_BATCH = 2
FFN_BATCH = 4
ATTN_TILE = 1024
FFN_TILE = 256
N_MOD = 6
VMEM_LIMIT = 62 * 1024 * 1024

F32 = jnp.float32
BF16 = jnp.bfloat16


def _dot(a, b):
    return jnp.dot(a, b, preferred_element_type=F32)


def _dot_nt(a, b):
    return lax.dot_general(a, b, (((1,), (1,)), ((), ())), preferred_element_type=F32)


def _fixed_spec(block, index):
    return pl.BlockSpec(block, lambda *_: index, pipeline_mode=pl.Buffered(1))


def _layer_spec(tail, layer, tail_index=None):
    tail_index = (0,) * len(tail) if tail_index is None else tail_index
    return _fixed_spec((None,) + tuple(tail), (layer,) + tuple(tail_index))


def _pair_spec(pair, tail, tail_index=None):
    return _layer_spec(tail, pair[1], tail_index)


def _cast_plumbing(casts, steps, flat_index):
    in_specs, out_specs, out_shapes = [], [], []
    for arr, layer in casts:
        _, r, c = arr.shape
        rows = r // steps
        assert rows * steps == r and rows % 16 == 0, (arr.shape, steps)
        in_specs.append(pl.BlockSpec((None, rows, c), lambda i, b, layer=layer: (layer, flat_index(i, b), 0)))
        out_specs.append(pl.BlockSpec((None, rows, c), lambda i, b: (0, flat_index(i, b), 0)))
        out_shapes.append(jax.ShapeDtypeStruct((1, r, c), BF16))
    return in_specs, out_specs, out_shapes


def _run_casts(cast_in, cast_out):
    for src, dst in zip(cast_in, cast_out):
        dst[...] = src[...].astype(BF16)


def _mod_spec(layer, row):
    if row is None:
        return pl.BlockSpec((None, None, N_MOD, D_MODEL), lambda i, b: (layer, b, 0, 0))
    return _fixed_spec((None, None, N_MOD, D_MODEL), (layer, row, 0, 0))


def _mod_spec_multi(layer, row, nb):
    if row is None:
        return pl.BlockSpec((None, nb, N_MOD, D_MODEL), lambda i, b: (layer, b, 0, 0))
    return _fixed_spec((None, 1, N_MOD, D_MODEL), (layer, row, 0, 0))


def _params():
    return pltpu.CompilerParams(dimension_semantics=("arbitrary", "arbitrary"),
                                vmem_limit_bytes=VMEM_LIMIT)


def _rms_mod(xv, norm, shift, scale):
    ms = jnp.mean(xv * xv, axis=-1, keepdims=True)
    y = xv * lax.rsqrt(ms + NORM_EPS) * norm
    return y * (1.0 + scale) + shift


def _halo_specs(n, tq, width, nb=1):
    nblk = n // HALO
    per = tq // HALO
    prev = pl.BlockSpec((nb, HALO, width), lambda i, b: (b, jnp.maximum(i * per - 1, 0), 0))
    cur = pl.BlockSpec((nb, tq, width), lambda i, b: (b, i, 0))
    nxt = pl.BlockSpec((nb, HALO, width), lambda i, b: (b, jnp.minimum((i + 1) * per, nblk - 1), 0))
    return [prev, cur, nxt]


def _fill_halo_rows(h_scr, xp_ref, xc_ref, xn_ref, fn, tq):
    i = pl.program_id(0)
    last = pl.num_programs(0) - 1
    h_scr[HALO:HALO + tq, :] = fn(xc_ref[0])
    h_scr[0:HALO, :] = jnp.where(i > 0, fn(xp_ref[0]), 0.0)
    h_scr[HALO + tq:2 * HALO + tq, :] = jnp.where(i < last, fn(xn_ref[0]), 0.0)


def _split_bf16(a):
    hi = a.astype(BF16)
    return hi, (a - hi.astype(F32)).astype(BF16)


def _head_rms(z, sum_ref, bcast_ref, gain):
    sq_hi, sq_lo = _split_bf16(z * z)
    ss = _dot(sq_hi, sum_ref[...]) + _dot(sq_lo, sum_ref[...])
    r_hi, r_lo = _split_bf16(lax.rsqrt(ss * (1.0 / HEAD_DIM) + NORM_EPS))
    return z * (_dot(r_hi, bcast_ref[...]) + _dot(r_lo, bcast_ref[...])) * gain


def _dup_heads(a, low_half):
    sw = pltpu.roll(a, HEAD_DIM, 1)
    return jnp.where(low_half, a, sw), jnp.where(low_half, sw, a)


def _mod_kernel(c_ref, w_ref, b_ref, o_ref):
    cv = c_ref[...]
    a = (cv * jax.nn.sigmoid(cv)).astype(BF16)
    o_ref[0] = _dot(a, w_ref[0].astype(BF16)) + b_ref[0]


def _modulation(cvec, w_mod, b_mod):
    rows = cvec.shape[0]
    nl = w_mod.shape[0]
    out = pl.pallas_call(
        _mod_kernel,
        out_shape=jax.ShapeDtypeStruct((nl, rows, N_MOD * D_MODEL), F32),
        grid=(nl, N_MOD),
        in_specs=[pl.BlockSpec((rows, D_MODEL), lambda l, j: (0, 0)),
                  pl.BlockSpec((1, D_MODEL, D_MODEL), lambda l, j: (l, 0, j)),
                  pl.BlockSpec((1, 1, D_MODEL), lambda l, j: (l, 0, j))],
        out_specs=pl.BlockSpec((1, rows, D_MODEL), lambda l, j: (l, 0, j)),
        compiler_params=_params(),
        name="adaln_mod",
    )(cvec, w_mod, b_mod.reshape(nl, 1, N_MOD * D_MODEL))
    return out.reshape(nl, rows, N_MOD, D_MODEL)


def _in_proj_kernel(xp_ref, xc_ref, xn_ref, m_ref, norm_ref, w_ref, hs_ref, hb_ref, gain_ref, cos_ref, sin_ref,
                    conv_ref, pm_ref, ps_ref,
                    q_ref, k_ref, v_ref, sc_ref, xf_ref, yp_ref,
                    h_scr, z_scr, *, n, tq, rope, nb):
    i = pl.program_id(0)
    norm = norm_ref[...]

    def norm_stage(e):
        m = m_ref.at[min(e, m_ref.shape[0] - 1)]
        shift, scale = m[0:1, :], m[1:2, :]
        _fill_halo_rows(h_scr.at[e], xp_ref.at[e:e + 1], xc_ref.at[e:e + 1], xn_ref.at[e:e + 1],
                        lambda xv: _rms_mod(xv, norm, shift, scale), tq)

    norm_stage(0)
    for e in range(nb):
        z_scr[e] = _dot(h_scr[e].astype(BF16), w_ref[...])
        if e + 1 < nb:
            norm_stage(e + 1)
        _local_mixers(e, z_scr.at[e], hs_ref, hb_ref, gain_ref, cos_ref, sin_ref, conv_ref, pm_ref, ps_ref,
                      q_ref, k_ref, v_ref, sc_ref, xf_ref, yp_ref, i=i, n=n, tq=tq, rope=rope)


def _local_mixers(e, z_scr, hs_ref, hb_ref, gain_ref, cos_ref, sin_ref, conv_ref, pm_ref, ps_ref,
                  q_ref, k_ref, v_ref, sc_ref, xf_ref, yp_ref, *, i, n, tq, rope):
    lane = lax.broadcasted_iota(jnp.int32, (tq, LANES), 1)
    low_half = lane < HEAD_DIM
    rows = slice(HALO, HALO + tq)

    y = _head_rms(z_scr[rows, 0:QK_WIDTH], hs_ref, hb_ref, gain_ref[...])
    second_half = (lane & AXIS_PAIRS) != 0
    for c in range(QK_WIDTH // LANES):
        yc = y[:, c * LANES:(c + 1) * LANES]
        if rope:
            partner = jnp.where(second_half, pltpu.roll(yc, AXIS_PAIRS, 1),
                                pltpu.roll(yc, LANES - AXIS_PAIRS, 1))
            yc = yc * cos_ref[...] + partner * sin_ref[...]
        if c < Q_WIDTH // LANES:
            q_ref[e, :, c * LANES:(c + 1) * LANES] = yc.astype(BF16)
        else:
            k0, k1 = _dup_heads(yc, low_half)
            k_ref[e, :, 0:LANES] = k0.astype(BF16)
            k_ref[e, :, LANES:2 * LANES] = k1.astype(BF16)
    v0, v1 = _dup_heads(z_scr[rows, OFF_V:OFF_SC], low_half)
    v_ref[e, :, 0:LANES] = v0.astype(BF16)
    v_ref[e, :, LANES:2 * LANES] = v1.astype(BF16)

    def u_at(off):
        r = slice(HALO + off, HALO + off + tq)
        return z_scr[r, OFF_SC + SC_WIDTH:OFF_SC + 2 * SC_WIDTH] * z_scr[r, OFF_SC + 2 * SC_WIDTH:OFF_F]

    conv = u_at(-1) * conv_ref[0:1, :] + u_at(0) * conv_ref[1:2, :] + u_at(1) * conv_ref[2:3, :]
    sc_ref[e] = (z_scr[rows, OFF_SC:OFF_SC + SC_WIDTH] * conv).astype(BF16)

    xf_ref[e] = z_scr[rows, OFF_F:OFF_P].astype(BF16)

    t = i * tq + lax.broadcasted_iota(jnp.int32, (tq, LANES), 0)

    def p_at(off, c):
        return z_scr[HALO + off:HALO + off + tq, OFF_P + c * LANES:OFF_P + (c + 1) * LANES]

    def window_mean(total, w_low, w_high):
        w = jnp.where(low_half, w_low, w_high)
        left, right = (w - 1) // 2, w // 2
        cnt = jnp.minimum(t + right + 1, n) - jnp.maximum(t - left, 0)
        return total / cnt.astype(F32)

    p0 = p_at(0, 0)
    s2 = p0 + p_at(1, 0)
    s4 = s2 + p_at(-1, 0) + p_at(2, 0)
    pooled0 = window_mean(jnp.where(low_half, s2, s4), POOL_WINDOWS[0], POOL_WINDOWS[1]) - p0
    p1 = p_at(0, 1)
    s8 = p1
    for off in (-3, -2, -1, 1, 2, 3, 4):
        s8 = s8 + p_at(off, 1)
    s16 = s8
    for off in (-7, -6, -5, -4, 5, 6, 7, 8):
        s16 = s16 + p_at(off, 1)
    pooled1 = window_mean(jnp.where(low_half, s8, s16), POOL_WINDOWS[2], POOL_WINDOWS[3]) - p1
    pooled = jnp.concatenate([pooled0, pooled1], axis=1).astype(BF16)
    yp_ref[e] = (_dot(pooled, pm_ref[...]) * ps_ref[...]).astype(BF16)


def _in_proj(xs, prm, layer, mod_row, cos_t, sin_t, *, tq, rope):
    b, n, _ = xs.shape
    out_w = (Q_WIDTH, 2 * LANES, 2 * LANES, SC_WIDTH, FOURIER_WIDTH, POOL_WIDTH)
    nb = IN_PROJ_BATCH
    kern = functools.partial(_in_proj_kernel, n=n, tq=tq, rope=rope, nb=nb)
    return pl.pallas_call(
        kern,
        out_shape=[jax.ShapeDtypeStruct((b, n, w), BF16) for w in out_w],
        grid=(n // tq, b // nb),
        in_specs=_halo_specs(n, tq, D_MODEL, nb) + [
            _mod_spec_multi(layer, mod_row, nb),
            _layer_spec((1, D_MODEL), layer),
            _pair_spec(prm["w_in"], (D_MODEL, OFF_G)),
            _fixed_spec((QK_WIDTH, LANES), (0, 0)),
            _fixed_spec((LANES, QK_WIDTH), (0, 0)),
            _layer_spec((1, QK_WIDTH), layer),
            pl.BlockSpec((tq, LANES), lambda i, bb: (i, 0)),
            pl.BlockSpec((tq, LANES), lambda i, bb: (i, 0)),
            _layer_spec((3, SC_WIDTH), layer),
            _layer_spec((POOL_WIDTH, POOL_WIDTH), layer),
            _layer_spec((1, POOL_WIDTH), layer),
        ],
        out_specs=[pl.BlockSpec((nb, tq, w), lambda i, bb: (bb, i, 0)) for w in out_w],
        scratch_shapes=[pltpu.VMEM((nb, tq + 2 * HALO, D_MODEL), F32),
                        pltpu.VMEM((nb, tq + 2 * HALO, OFF_G), F32)],
        compiler_params=_params(),
        name="in_proj",
    )(xs, xs, xs, prm["mod"], prm["norm1"], prm["w_in"][0], prm["head_sum"], prm["head_bcast"], prm["gain"],
      cos_t, sin_t,
      prm["conv_sc"], prm["pool_mat"], prm["pool_scale"])


def _kv_proj_kernel(x_ref, m_ref, norm_ref, w_ref, hs_ref, hb_ref, gain_ref, k_ref, v_ref, *, tq):
    h = _rms_mod(x_ref[0], norm_ref[...], m_ref[0:1, :], m_ref[1:2, :]).astype(BF16)
    z = _dot(h, w_ref[...])
    low_half = lax.broadcasted_iota(jnp.int32, (tq, LANES), 1) < HEAD_DIM
    k0, k1 = _dup_heads(_head_rms(z[:, 0:KV_WIDTH], hs_ref, hb_ref, gain_ref[...]), low_half)
    k_ref[0, :, 0:LANES] = k0.astype(BF16)
    k_ref[0, :, LANES:2 * LANES] = k1.astype(BF16)
    v0, v1 = _dup_heads(z[:, KV_WIDTH:2 * KV_WIDTH], low_half)
    v_ref[0, :, 0:LANES] = v0.astype(BF16)
    v_ref[0, :, LANES:2 * LANES] = v1.astype(BF16)


def _kv_proj(xs, prm, layer, mod_row, *, tq):
    b, n, _ = xs.shape
    kv_cols = 2 * KV_WIDTH
    return pl.pallas_call(
        functools.partial(_kv_proj_kernel, tq=tq),
        out_shape=[jax.ShapeDtypeStruct((b, n, 2 * LANES), BF16)] * 2,
        grid=(n // tq, b),
        in_specs=[pl.BlockSpec((1, tq, D_MODEL), lambda i, bb: (bb, i, 0)),
                  _mod_spec(layer, mod_row),
                  _layer_spec((1, D_MODEL), layer),
                  _pair_spec(prm["w_in"], (D_MODEL, kv_cols), (0, OFF_K // kv_cols)),
                  _fixed_spec((KV_WIDTH, LANES), (0, 0)),
                  _fixed_spec((LANES, KV_WIDTH), (0, 0)),
                  _layer_spec((1, KV_WIDTH), layer, (0, Q_WIDTH // KV_WIDTH))],
        out_specs=[pl.BlockSpec((1, tq, 2 * LANES), lambda i, bb: (bb, i, 0))] * 2,
        compiler_params=_params(),
        name="kv_proj",
    )(xs, prm["mod"], prm["norm1"], prm["w_in"][0], prm["head_sum"], prm["head_bcast"], prm["gain"])


def _attn_kernel(q_ref, *refs, tq):
    *kv_refs, o_ref = refs
    sub = min(tq, ATTN_SUB)
    lane = lax.broadcasted_iota(jnp.int32, (sub, LANES), 1)
    low_half = lane < HEAD_DIM
    sources = [(kv_refs[2 * a], kv_refs[2 * a + 1]) for a in range(len(kv_refs) // 2)]
    for j in range(Q_WIDTH // LANES):
        g = (2 * j) // (N_HEADS // N_KV_HEADS)
        gl = slice(g * LANES, (g + 1) * LANES)
        for r in range(tq // sub):
            rows = slice(r * sub, (r + 1) * sub)
            qp = q_ref[0, rows, j * LANES:(j + 1) * LANES]
            zero = jnp.zeros_like(qp)
            q2 = jnp.concatenate([jnp.where(low_half, qp, zero), jnp.where(low_half, zero, qp)], axis=0)
            scores = [_dot_nt(q2, k_ref[0, :, gl]) for k_ref, _ in sources]
            row_max = functools.reduce(jnp.maximum, [jnp.max(s, axis=-1, keepdims=True) for s in scores])
            o, denom = None, None
            for s, (_, v_ref) in zip(scores, sources):
                p = jnp.exp(s - row_max)
                part = jnp.sum(p, axis=-1, keepdims=True)
                term = _dot(p.astype(BF16), v_ref[0, :, gl])
                o, denom = (term, part) if o is None else (o + term, denom + part)
            o = o / denom
            o_ref[0, rows, j * LANES:(j + 1) * LANES] = jnp.where(low_half, o[:sub], o[sub:]).astype(BF16)


def _attention(q, kv_sources, *, tq):
    b, n, _ = q.shape
    kv_specs, kv_args = [], []
    for k, v in kv_sources:
        for a in (k, v):
            kv_specs.append(pl.BlockSpec((1, a.shape[1], 2 * LANES), lambda bb, i: (bb, 0, 0)))
            kv_args.append(a)
    return pl.pallas_call(
        functools.partial(_attn_kernel, tq=tq),
        out_shape=jax.ShapeDtypeStruct((b, n, Q_WIDTH), BF16),
        grid=(b, n // tq),
        in_specs=[pl.BlockSpec((1, tq, Q_WIDTH), lambda bb, i: (bb, i, 0))] + kv_specs,
        out_specs=pl.BlockSpec((1, tq, Q_WIDTH), lambda bb, i: (bb, i, 0)),
        compiler_params=_params(),
        name="attention",
    )(q, *kv_args)


def _fourier_kernel(x_ref, wc_ref, cn_ref, nsn_ref, o_ref, t_scr, *, n, tr):
    t_scr[...] = _dot(x_ref[0], wc_ref[...].astype(BF16)).astype(BF16)
    scale = 1.0 / math.sqrt(n * FOURIER_GROUP)
    for r in range(n // tr):
        rows = slice(r * tr, (r + 1) * tr)
        acc = (_dot(cn_ref[rows, :].astype(BF16), t_scr[:, 0:FOURIER_WIDTH])
               + _dot(nsn_ref[rows, :].astype(BF16), t_scr[:, FOURIER_WIDTH:2 * FOURIER_WIDTH]))
        o_ref[0, rows, :] = (acc * scale).astype(BF16)


def _fourier(xf, wc, cn, nsn):
    b, n, _ = xf.shape
    tr = min(n, 512)
    return pl.pallas_call(
        functools.partial(_fourier_kernel, n=n, tr=tr),
        out_shape=jax.ShapeDtypeStruct((b, n, FOURIER_WIDTH), BF16),
        grid=(b, 1),
        in_specs=[pl.BlockSpec((1, n, FOURIER_WIDTH), lambda bb, i: (bb, 0, 0)),
                  _fixed_spec((FOURIER_WIDTH, 2 * FOURIER_WIDTH), (0, 0)),
                  _fixed_spec((n, n), (0, 0)),
                  _fixed_spec((n, n), (0, 0))],
        out_specs=pl.BlockSpec((1, n, FOURIER_WIDTH), lambda bb, i: (bb, 0, 0)),
        scratch_shapes=[pltpu.VMEM((n, 2 * FOURIER_WIDTH), BF16)],
        compiler_params=_params(),
        name="fourier",
    )(xf, wc, cn, nsn)


N_MIX_INPUTS = 14


def _mix_kernel(*refs, nb, n_cast):
    (x_ref, m_ref, norm_ref, oa_ref, sc_ref, xf_ref, yp_ref,
     wg01_ref, wg23_ref, wa_ref, wsc_ref, wf_ref, wp_ref, wo_ref) = refs[:N_MIX_INPUTS]
    cast_in = refs[N_MIX_INPUTS:N_MIX_INPUTS + n_cast]
    out_ref = refs[N_MIX_INPUTS + n_cast]
    cast_out = refs[N_MIX_INPUTS + n_cast + 1:N_MIX_INPUTS + 2 * n_cast + 1]
    h_scr, y_scr = refs[-2:]
    _run_casts(cast_in, cast_out)
    branches = ((oa_ref, wa_ref), (sc_ref, wsc_ref), (xf_ref, wf_ref), (yp_ref, wp_ref))
    gate_w = (wg01_ref, wg23_ref)
    norm = norm_ref[...]

    def mod(e):
        return m_ref.at[min(e, m_ref.shape[0] - 1)]

    def norm_stage(e):
        m = mod(e)
        h_scr[e] = _rms_mod(x_ref[e], norm, m[0:1, :], m[1:2, :]).astype(BF16)

    norm_stage(0)
    for e in range(nb):
        h = h_scr[e]
        for j in range(D_MODEL // FF_CHUNK):
            cols = slice(j * FF_CHUNK, (j + 1) * FF_CHUNK)
            y = None
            for bi, (a_ref, w_ref) in enumerate(branches):
                off = (bi % 2) * D_MODEL + j * FF_CHUNK
                gate = 0.5 + 0.5 * jnp.tanh(0.5 * _dot(h, gate_w[bi // 2][:, off:off + FF_CHUNK]))
                term = gate * _dot(a_ref[e], w_ref[:, cols])
                y = term if y is None else y + term
            y_scr[e, :, cols] = y.astype(BF16)
            if j == 0 and e + 1 < nb:
                norm_stage(e + 1)
        out_ref[e] = x_ref[e] + mod(e)[2:3, :] * _dot(y_scr[e], wo_ref[...])


def _mix(xs, prm, layer, mod_row, oa, sc, xf, yp, *, tq, casts=()):
    b, n, _ = xs.shape
    nb = MIX_BATCH
    half_gates = N_BRANCHES * D_MODEL // 2
    grid = (n // tq, b // nb)
    cast_in, cast_out, cast_shapes = _cast_plumbing(casts, grid[0] * grid[1], lambda i, bb: i * grid[1] + bb)

    def tok(w):
        return pl.BlockSpec((nb, tq, w), lambda i, bb: (bb, i, 0))

    in_specs = [tok(D_MODEL), _mod_spec_multi(layer, mod_row, nb), _layer_spec((1, D_MODEL), layer),
                tok(Q_WIDTH), tok(SC_WIDTH), tok(FOURIER_WIDTH), tok(POOL_WIDTH),
                _pair_spec(prm["w_in"], (D_MODEL, half_gates), (0, OFF_G // half_gates)),
                _pair_spec(prm["w_in"], (D_MODEL, half_gates), (0, OFF_G // half_gates + 1)),
                _layer_spec((Q_WIDTH, D_MODEL), layer), _layer_spec((SC_WIDTH, D_MODEL), layer),
                _layer_spec((FOURIER_WIDTH, D_MODEL), layer), _layer_spec((POOL_WIDTH, D_MODEL), layer),
                _layer_spec((D_MODEL, D_MODEL), layer)]
    assert len(in_specs) == N_MIX_INPUTS
    return pl.pallas_call(
        functools.partial(_mix_kernel, nb=nb, n_cast=len(casts)),
        out_shape=[jax.ShapeDtypeStruct((b, n, D_MODEL), F32)] + cast_shapes,
        grid=grid,
        in_specs=in_specs + cast_in,
        out_specs=[tok(D_MODEL)] + cast_out,
        scratch_shapes=[pltpu.VMEM((nb, tq, D_MODEL), BF16), pltpu.VMEM((nb, tq, D_MODEL), BF16)],
        compiler_params=_params(),
        name="mix",
    )(xs, prm["mod"], prm["norm1"], oa, sc, xf, yp, prm["w_in"][0], prm["w_in"][0],
      prm["w_br_attn"], prm["w_br_sc"], prm["w_br_f"], prm["w_br_p"], prm["w_out"], *[a for a, _ in casts])


N_FFN_INPUTS = 9


def _ffn_kernel(*refs, tq, final, nb, n_cast):
    xp_ref, xc_ref, xn_ref, m_ref, norm_ref, wu_ref, cw_ref, wd_ref, fn_ref = refs[:N_FFN_INPUTS]
    cast_in = refs[N_FFN_INPUTS:N_FFN_INPUTS + n_cast]
    out_ref = refs[N_FFN_INPUTS + n_cast]
    cast_out = refs[N_FFN_INPUTS + n_cast + 1:N_FFN_INPUTS + 2 * n_cast + 1]
    h_scr, hb_scr, u0_scr, u1_scr = refs[-4:]
    _run_casts(cast_in, cast_out)
    norm = norm_ref[...]
    ext = tq + 2 * HALO

    def mod(e):
        return m_ref.at[min(e, m_ref.shape[0] - 1)]

    def cols(c, half):
        return slice(half * D_FF + c * FF_CHUNK, half * D_FF + (c + 1) * FF_CHUNK)

    def norm_stage(e):
        m = mod(e)
        shift, scale = m[3:4, :], m[4:5, :]
        _fill_halo_rows(h_scr.at[e], xp_ref.at[e:e + 1], xc_ref.at[e:e + 1], xn_ref.at[e:e + 1],
                        lambda xv: _rms_mod(xv, norm, shift, scale), tq)
        hb_scr[e] = h_scr[e].astype(BF16)

    def up(e, c, u_scr):
        hb = hb_scr[e]
        u_scr[0] = _dot(hb, wu_ref[:, cols(c, 0)])
        u_scr[1] = _dot(hb, wu_ref[:, cols(c, 1)])

    def conv3(u, w):
        full = pltpu.roll(u, 1, 0) * w[0:1, :] + u * w[1:2, :] + pltpu.roll(u, ext - 1, 0) * w[2:3, :]
        return full[HALO:HALO + tq]

    def down(c, u_scr):
        a = conv3(u_scr[0], cw_ref[:, cols(c, 0)])
        bv = conv3(u_scr[1], cw_ref[:, cols(c, 1)])
        half = 0.5 * a
        act = ((half + half * jnp.tanh(half)) * bv).astype(BF16)
        return _dot(act, wd_ref[c * FF_CHUNK:(c + 1) * FF_CHUNK, :])

    bufs = (u0_scr, u1_scr)
    units = [(e, c) for e in range(nb) for c in range(N_FF_CHUNKS)]
    norm_stage(0)
    up(0, 0, u0_scr)
    acc = None
    for k, (e, c) in enumerate(units):
        if k + 1 < len(units):
            e_next, c_next = units[k + 1]
            if c_next == 0:
                norm_stage(e_next)
            up(e_next, c_next, bufs[(k + 1) % 2])
        d = down(c, bufs[k % 2])
        acc = d if c == 0 else acc + d
        if c == N_FF_CHUNKS - 1:
            out = xc_ref[e] + mod(e)[5:6, :] * acc
            if final:
                ms = jnp.mean(out * out, axis=-1, keepdims=True)
                out = out * lax.rsqrt(ms + NORM_EPS) * fn_ref[...]
            out_ref[e] = out


def _ffn(xs, prm, layer, mod_row, *, tq, final, casts=()):
    b, n, _ = xs.shape
    nb = FFN_BATCH
    grid = (n // tq, b // nb)
    cast_in, cast_out, cast_shapes = _cast_plumbing(casts, grid[0] * grid[1], lambda i, bb: i * grid[1] + bb)
    in_specs = _halo_specs(n, tq, D_MODEL, nb) + [
        _mod_spec_multi(layer, mod_row, nb),
        _layer_spec((1, D_MODEL), layer),
        _pair_spec(prm["w_up"], (D_MODEL, 2 * D_FF)),
        _layer_spec((3, 2 * D_FF), layer),
        _pair_spec(prm["w_down"], (D_FF, D_MODEL)),
        _fixed_spec((1, D_MODEL), (0, 0))]
    assert len(in_specs) == N_FFN_INPUTS
    return pl.pallas_call(
        functools.partial(_ffn_kernel, tq=tq, final=final, nb=nb, n_cast=len(casts)),
        out_shape=[jax.ShapeDtypeStruct((b, n, D_MODEL), F32)] + cast_shapes,
        grid=grid,
        in_specs=in_specs + cast_in,
        out_specs=[pl.BlockSpec((nb, tq, D_MODEL), lambda i, bb: (bb, i, 0))] + cast_out,
        scratch_shapes=[pltpu.VMEM((nb, tq + 2 * HALO, D_MODEL), F32),
                        pltpu.VMEM((nb, tq + 2 * HALO, D_MODEL), BF16),
                        pltpu.VMEM((2, tq + 2 * HALO, FF_CHUNK), F32),
                        pltpu.VMEM((2, tq + 2 * HALO, FF_CHUNK), F32)],
        compiler_params=_params(),
        name="ffn",
    )(xs, xs, xs, prm["mod"], prm["norm2"], prm["w_up"][0], prm["w_conv_ffn"], prm["w_down"][0],
      prm["final_norm"], *[a for a, _ in casts])


def _rope_tables(n):
    d = np.arange(LANES)
    hd = d % HEAD_DIM
    axis = hd // (2 * AXIS_PAIRS)
    half = (hd % (2 * AXIS_PAIRS)) // AXIS_PAIRS
    pair = hd % AXIS_PAIRS
    t = np.arange(n)
    pos = np.where(axis[None, :] == 0, (t // GRID_W)[:, None], (t % GRID_W)[:, None]).astype(np.float64)
    inv = ROPE_THETA ** (-np.arange(AXIS_PAIRS, dtype=np.float64) / AXIS_PAIRS)
    ang = pos * inv[pair][None, :]
    sin = np.sin(ang)
    return (jnp.asarray(np.cos(ang), dtype=F32),
            jnp.asarray(np.where(half[None, :] == 0, -sin, sin), dtype=F32))


def _dft_tables(n):
    k = np.arange(n, dtype=np.int64)
    ang = ((k[:, None] * k[None, :]) % n).astype(np.float64) * (2.0 * math.pi / n)
    return jnp.asarray(np.cos(ang), dtype=F32), jnp.asarray(-np.sin(ang), dtype=F32)


def _channel_dft():
    c = np.arange(FOURIER_WIDTH, dtype=np.int64)
    same = (c[:, None] // FOURIER_GROUP) == (c[None, :] // FOURIER_GROUP)
    mc = ((c[:, None] % FOURIER_GROUP) * (c[None, :] % FOURIER_GROUP)) % FOURIER_GROUP
    ang = mc.astype(np.float64) * (2.0 * math.pi / FOURIER_GROUP)
    both = np.concatenate([np.where(same, np.cos(ang), 0.0), np.where(same, np.sin(ang), 0.0)], axis=1)
    return jnp.asarray(both, dtype=F32)


def _head_matrices():
    r = np.arange(QK_WIDTH) // HEAD_DIM
    m = (r[:, None] == np.arange(LANES)[None, :]).astype(np.float32)
    return jnp.asarray(m.astype(BF16)), jnp.asarray(m.T.copy().astype(BF16))


def _block_diag(mats):
    nl, g, a, b = mats.shape
    eye = jnp.eye(g, dtype=mats.dtype)
    return (mats[:, :, :, None, :] * eye[None, :, None, :, None]).reshape(nl, g * a, g * b)


def kernel(x, c, ctx, c_ctx, w_mod, b_mod, norm1, norm2, w_in, conv_sc, qk_gain, pool_mat, pool_scale,
           w_br_attn, w_br_sc, w_br_f, w_br_p, w_out, w_up, w_conv_ffn, w_down, final_norm):
    bsz, n_lat, _ = x.shape
    n_ctx = ctx.shape[1]
    nl = w_in.shape[0]
    tq_lat = min(n_lat, TOKEN_TILE)
    tq_ctx = min(n_ctx, TOKEN_TILE)
    tq_attn = min(n_lat, ATTN_TILE)
    tq_ffn_lat = min(n_lat, FFN_TILE)
    tq_ffn_ctx = min(n_ctx, FFN_TILE)

    mod_rows = 16
    ctx_row = bsz
    cvec = jnp.concatenate([c, c_ctx[None], jnp.zeros((mod_rows - bsz - 1, D_MODEL), F32)], axis=0)
    gain = jnp.concatenate([jnp.tile(qk_gain[:, 0], (1, N_HEADS)) * (HEAD_DIM ** -0.5),
                            jnp.tile(qk_gain[:, 1], (1, N_KV_HEADS))], axis=1)
    head_sum, head_bcast = _head_matrices()
    prm = {
        "mod": _modulation(cvec, w_mod, b_mod),
        "norm1": norm1.reshape(nl, 1, D_MODEL),
        "norm2": norm2.reshape(nl, 1, D_MODEL),
        "w_in": (w_in[:1].astype(BF16), 0),
        "head_sum": head_sum,
        "head_bcast": head_bcast,
        "gain": gain.reshape(nl, 1, QK_WIDTH),
        "conv_sc": conv_sc,
        "pool_mat": _block_diag(pool_mat).astype(BF16),
        "pool_scale": pool_scale.reshape(nl, 1, POOL_WIDTH),
        "w_br_attn": w_br_attn.astype(BF16),
        "w_br_sc": w_br_sc.astype(BF16),
        "w_br_f": w_br_f.astype(BF16),
        "w_br_p": w_br_p.astype(BF16),
        "w_out": w_out.astype(BF16),
        "w_conv_ffn": w_conv_ffn,
        "final_norm": final_norm.reshape(1, D_MODEL),
    }
    cos_t, sin_t = _rope_tables(n_lat)
    wc = _channel_dft()
    dft_lat = _dft_tables(n_lat)
    dft_ctx = _dft_tables(n_ctx)

    lat, cx = x, ctx
    for l in range(nl):
        last = l == nl - 1
        if last:
            kc, vc = _kv_proj(cx, prm, l, ctx_row, tq=tq_ctx)
        else:
            qc, kc, vc, scc, xfc, ypc = _in_proj(cx, prm, l, ctx_row, cos_t, sin_t, tq=tq_ctx, rope=False)
        ql, kl, vl, scl, xfl, ypl = _in_proj(lat, prm, l, None, cos_t, sin_t, tq=tq_lat, rope=True)
        oa = _attention(ql, [(kl, vl), (kc, vc)], tq=tq_attn)
        xfour = _fourier(xfl, wc, *dft_lat)
        lat, w_up_bf, w_down_bf = _mix(lat, prm, l, None, oa, scl, xfour, ypl, tq=tq_lat,
                                       casts=[(w_up, l), (w_down, l)])
        prm["w_up"], prm["w_down"] = (w_up_bf, 0), (w_down_bf, 0)
        lat, *next_w_in = _ffn(lat, prm, l, None, tq=tq_ffn_lat, final=last,
                               casts=[] if last else [(w_in, l + 1)])

        if not last:
            oac = _attention(qc, [(kc, vc)], tq=tq_ctx)
            xfourc = _fourier(xfc, wc, *dft_ctx)
            cx, = _mix(cx, prm, l, ctx_row, oac, scc, xfourc, ypc, tq=tq_ctx)
            cx, = _ffn(cx, prm, l, ctx_row, tq=tq_ffn_ctx, final=False)
            prm["w_in"] = (next_w_in[0], 0)
    return lat
```

```python
import functools
import math

import numpy as np
import jax
import jax.numpy as jnp
from jax import lax
from jax.experimental import pallas as pl
from jax.experimental.pallas import tpu as pltpu

D_MODEL = 1024
DEPTH = 2
GRID_W = 64
NORM_EPS = 1e-6
N_HEADS = 8
N_KV_HEADS = 2
HEAD_DIM = 64
Q_WIDTH = N_HEADS * HEAD_DIM
KV_WIDTH = N_KV_HEADS * HEAD_DIM
QK_WIDTH = Q_WIDTH + KV_WIDTH
ROPE_THETA = 10000.0
AXIS_PAIRS = HEAD_DIM // 4
SC_WIDTH = 256
FOURIER_GROUPS = 4
FOURIER_WIDTH = 256
FOURIER_GROUP = FOURIER_WIDTH // FOURIER_GROUPS
POOL_WINDOWS = (2, 4, 8, 16)
POOL_WIDTH = 256
POOL_GROUP = POOL_WIDTH // len(POOL_WINDOWS)
N_BRANCHES = 4
D_FF = 2816

OFF_K = Q_WIDTH
OFF_V = OFF_K + KV_WIDTH
OFF_SC = OFF_V + KV_WIDTH
OFF_F = OFF_SC + 3 * SC_WIDTH
OFF_P = OFF_F + FOURIER_WIDTH
OFF_G = OFF_P + POOL_WIDTH
IN_WIDTH = OFF_G + N_BRANCHES * D_MODEL

LANES = 128
HALO = 8
FF_CHUNK = 256
N_FF_CHUNKS = D_FF // FF_CHUNK
ATTN_SUB = 256
TOKEN_TILE = 512
IN_PROJ_BATCH = 2
MIX_BATCH = 2
FFN_BATCH = 4
ATTN_TILE = 1024
FFN_TILE = 256
N_MOD = 6
VMEM_LIMIT = 62 * 1024 * 1024

F32 = jnp.float32
BF16 = jnp.bfloat16


def _dot(a, b):
    return jnp.dot(a, b, preferred_element_type=F32)


def _dot_nt(a, b):
    return lax.dot_general(a, b, (((1,), (1,)), ((), ())), preferred_element_type=F32)


def _fixed_spec(block, index):
    return pl.BlockSpec(block, lambda *_: index, pipeline_mode=pl.Buffered(1))


def _layer_spec(tail, layer, tail_index=None):
    tail_index = (0,) * len(tail) if tail_index is None else tail_index
    return _fixed_spec((None,) + tuple(tail), (layer,) + tuple(tail_index))


def _pair_spec(pair, tail, tail_index=None):
    return _layer_spec(tail, pair[1], tail_index)


def _cast_plumbing(casts, steps, flat_index):
    in_specs, out_specs, out_shapes = [], [], []
    for arr, layer in casts:
        _, r, c = arr.shape
        rows = r // steps
        assert rows * steps == r and rows % 16 == 0, (arr.shape, steps)
        in_specs.append(pl.BlockSpec((None, rows, c), lambda i, b, layer=layer: (layer, flat_index(i, b), 0)))
        out_specs.append(pl.BlockSpec((None, rows, c), lambda i, b: (0, flat_index(i, b), 0)))
        out_shapes.append(jax.ShapeDtypeStruct((1, r, c), BF16))
    return in_specs, out_specs, out_shapes


def _run_casts(cast_in, cast_out):
    for src, dst in zip(cast_in, cast_out):
        dst[...] = src[...].astype(BF16)


def _mod_spec(layer, row):
    if row is None:
        return pl.BlockSpec((None, None, N_MOD, D_MODEL), lambda i, b: (layer, b, 0, 0))
    return _fixed_spec((None, None, N_MOD, D_MODEL), (layer, row, 0, 0))


def _mod_spec_multi(layer, row, nb):
    if row is None:
        return pl.BlockSpec((None, nb, N_MOD, D_MODEL), lambda i, b: (layer, b, 0, 0))
    return _fixed_spec((None, 1, N_MOD, D_MODEL), (layer, row, 0, 0))


def _params():
    return pltpu.CompilerParams(dimension_semantics=("arbitrary", "arbitrary"),
                                vmem_limit_bytes=VMEM_LIMIT)


def _rms_mod(xv, norm, shift, scale):
    ms = jnp.mean(xv * xv, axis=-1, keepdims=True)
    y = xv * lax.rsqrt(ms + NORM_EPS) * norm
    return y * (1.0 + scale) + shift


def _halo_specs(n, tq, width, nb=1):
    nblk = n // HALO
    per = tq // HALO
    prev = pl.BlockSpec((nb, HALO, width), lambda i, b: (b, jnp.maximum(i * per - 1, 0), 0))
    cur = pl.BlockSpec((nb, tq, width), lambda i, b: (b, i, 0))
    nxt = pl.BlockSpec((nb, HALO, width), lambda i, b: (b, jnp.minimum((i + 1) * per, nblk - 1), 0))
    return [prev, cur, nxt]


def _fill_halo_rows(h_scr, xp_ref, xc_ref, xn_ref, fn, tq):
    i = pl.program_id(0)
    last = pl.num_programs(0) - 1
    h_scr[HALO:HALO + tq, :] = fn(xc_ref[0])
    h_scr[0:HALO, :] = jnp.where(i > 0, fn(xp_ref[0]), 0.0)
    h_scr[HALO + tq:2 * HALO + tq, :] = jnp.where(i < last, fn(xn_ref[0]), 0.0)


def _split_bf16(a):
    hi = a.astype(BF16)
    return hi, (a - hi.astype(F32)).astype(BF16)


def _head_rms(z, sum_ref, bcast_ref, gain):
    sq_hi, sq_lo = _split_bf16(z * z)
    ss = _dot(sq_hi, sum_ref[...]) + _dot(sq_lo, sum_ref[...])
    r_hi, r_lo = _split_bf16(lax.rsqrt(ss * (1.0 / HEAD_DIM) + NORM_EPS))
    return z * (_dot(r_hi, bcast_ref[...]) + _dot(r_lo, bcast_ref[...])) * gain


def _dup_heads(a, low_half):
    sw = pltpu.roll(a, HEAD_DIM, 1)
    return jnp.where(low_half, a, sw), jnp.where(low_half, sw, a)


def _mod_kernel(c_ref, w_ref, b_ref, o_ref):
    cv = c_ref[...]
    a = (cv * jax.nn.sigmoid(cv)).astype(BF16)
    o_ref[0] = _dot(a, w_ref[0].astype(BF16)) + b_ref[0]


def _modulation(cvec, w_mod, b_mod):
    rows = cvec.shape[0]
    nl = w_mod.shape[0]
    out = pl.pallas_call(
        _mod_kernel,
        out_shape=jax.ShapeDtypeStruct((nl, rows, N_MOD * D_MODEL), F32),
        grid=(nl, N_MOD),
        in_specs=[pl.BlockSpec((rows, D_MODEL), lambda l, j: (0, 0)),
                  pl.BlockSpec((1, D_MODEL, D_MODEL), lambda l, j: (l, 0, j)),
                  pl.BlockSpec((1, 1, D_MODEL), lambda l, j: (l, 0, j))],
        out_specs=pl.BlockSpec((1, rows, D_MODEL), lambda l, j: (l, 0, j)),
        compiler_params=_params(),
        name="adaln_mod",
    )(cvec, w_mod, b_mod.reshape(nl, 1, N_MOD * D_MODEL))
    return out.reshape(nl, rows, N_MOD, D_MODEL)


def _in_proj_kernel(xp_ref, xc_ref, xn_ref, m_ref, norm_ref, w_ref, hs_ref, hb_ref, gain_ref, cos_ref, sin_ref,
                    conv_ref, pm_ref, ps_ref,
                    q_ref, k_ref, v_ref, sc_ref, xf_ref, yp_ref,
                    h_scr, z_scr, *, n, tq, rope, nb):
    i = pl.program_id(0)
    norm = norm_ref[...]

    def norm_stage(e):
        m = m_ref.at[min(e, m_ref.shape[0] - 1)]
        shift, scale = m[0:1, :], m[1:2, :]
        _fill_halo_rows(h_scr.at[e], xp_ref.at[e:e + 1], xc_ref.at[e:e + 1], xn_ref.at[e:e + 1],
                        lambda xv: _rms_mod(xv, norm, shift, scale), tq)

    norm_stage(0)
    for e in range(nb):
        z_scr[e] = _dot(h_scr[e].astype(BF16), w_ref[...])
        if e + 1 < nb:
            norm_stage(e + 1)
        _local_mixers(e, z_scr.at[e], hs_ref, hb_ref, gain_ref, cos_ref, sin_ref, conv_ref, pm_ref, ps_ref,
                      q_ref, k_ref, v_ref, sc_ref, xf_ref, yp_ref, i=i, n=n, tq=tq, rope=rope)


def _local_mixers(e, z_scr, hs_ref, hb_ref, gain_ref, cos_ref, sin_ref, conv_ref, pm_ref, ps_ref,
                  q_ref, k_ref, v_ref, sc_ref, xf_ref, yp_ref, *, i, n, tq, rope):
    lane = lax.broadcasted_iota(jnp.int32, (tq, LANES), 1)
    low_half = lane < HEAD_DIM
    rows = slice(HALO, HALO + tq)

    y = _head_rms(z_scr[rows, 0:QK_WIDTH], hs_ref, hb_ref, gain_ref[...])
    second_half = (lane & AXIS_PAIRS) != 0
    for c in range(QK_WIDTH // LANES):
        yc = y[:, c * LANES:(c + 1) * LANES]
        if rope:
            partner = jnp.where(second_half, pltpu.roll(yc, AXIS_PAIRS, 1),
                                pltpu.roll(yc, LANES - AXIS_PAIRS, 1))
            yc = yc * cos_ref[...] + partner * sin_ref[...]
        if c < Q_WIDTH // LANES:
            q_ref[e, :, c * LANES:(c + 1) * LANES] = yc.astype(BF16)
        else:
            k0, k1 = _dup_heads(yc, low_half)
            k_ref[e, :, 0:LANES] = k0.astype(BF16)
            k_ref[e, :, LANES:2 * LANES] = k1.astype(BF16)
    v0, v1 = _dup_heads(z_scr[rows, OFF_V:OFF_SC], low_half)
    v_ref[e, :, 0:LANES] = v0.astype(BF16)
    v_ref[e, :, LANES:2 * LANES] = v1.astype(BF16)

    def u_at(off):
        r = slice(HALO + off, HALO + off + tq)
        return z_scr[r, OFF_SC + SC_WIDTH:OFF_SC + 2 * SC_WIDTH] * z_scr[r, OFF_SC + 2 * SC_WIDTH:OFF_F]

    conv = u_at(-1) * conv_ref[0:1, :] + u_at(0) * conv_ref[1:2, :] + u_at(1) * conv_ref[2:3, :]
    sc_ref[e] = (z_scr[rows, OFF_SC:OFF_SC + SC_WIDTH] * conv).astype(BF16)

    xf_ref[e] = z_scr[rows, OFF_F:OFF_P].astype(BF16)

    t = i * tq + lax.broadcasted_iota(jnp.int32, (tq, LANES), 0)

    def p_at(off, c):
        return z_scr[HALO + off:HALO + off + tq, OFF_P + c * LANES:OFF_P + (c + 1) * LANES]

    def window_mean(total, w_low, w_high):
        w = jnp.where(low_half, w_low, w_high)
        left, right = (w - 1) // 2, w // 2
        cnt = jnp.minimum(t + right + 1, n) - jnp.maximum(t - left, 0)
        return total / cnt.astype(F32)

    p0 = p_at(0, 0)
    s2 = p0 + p_at(1, 0)
    s4 = s2 + p_at(-1, 0) + p_at(2, 0)
    pooled0 = window_mean(jnp.where(low_half, s2, s4), POOL_WINDOWS[0], POOL_WINDOWS[1]) - p0
    p1 = p_at(0, 1)
    s8 = p1
    for off in (-3, -2, -1, 1, 2, 3, 4):
        s8 = s8 + p_at(off, 1)
    s16 = s8
    for off in (-7, -6, -5, -4, 5, 6, 7, 8):
        s16 = s16 + p_at(off, 1)
    pooled1 = window_mean(jnp.where(low_half, s8, s16), POOL_WINDOWS[2], POOL_WINDOWS[3]) - p1
    pooled = jnp.concatenate([pooled0, pooled1], axis=1).astype(BF16)
    yp_ref[e] = (_dot(pooled, pm_ref[...]) * ps_ref[...]).astype(BF16)


def _in_proj(xs, prm, layer, mod_row, cos_t, sin_t, *, tq, rope):
    b, n, _ = xs.shape
    out_w = (Q_WIDTH, 2 * LANES, 2 * LANES, SC_WIDTH, FOURIER_WIDTH, POOL_WIDTH)
    nb = IN_PROJ_BATCH
    kern = functools.partial(_in_proj_kernel, n=n, tq=tq, rope=rope, nb=nb)
    return pl.pallas_call(
        kern,
        out_shape=[jax.ShapeDtypeStruct((b, n, w), BF16) for w in out_w],
        grid=(n // tq, b // nb),
        in_specs=_halo_specs(n, tq, D_MODEL, nb) + [
            _mod_spec_multi(layer, mod_row, nb),
            _layer_spec((1, D_MODEL), layer),
            _pair_spec(prm["w_in"], (D_MODEL, OFF_G)),
            _fixed_spec((QK_WIDTH, LANES), (0, 0)),
            _fixed_spec((LANES, QK_WIDTH), (0, 0)),
            _layer_spec((1, QK_WIDTH), layer),
            pl.BlockSpec((tq, LANES), lambda i, bb: (i, 0)),
            pl.BlockSpec((tq, LANES), lambda i, bb: (i, 0)),
            _layer_spec((3, SC_WIDTH), layer),
            _layer_spec((POOL_WIDTH, POOL_WIDTH), layer),
            _layer_spec((1, POOL_WIDTH), layer),
        ],
        out_specs=[pl.BlockSpec((nb, tq, w), lambda i, bb: (bb, i, 0)) for w in out_w],
        scratch_shapes=[pltpu.VMEM((nb, tq + 2 * HALO, D_MODEL), F32),
                        pltpu.VMEM((nb, tq + 2 * HALO, OFF_G), F32)],
        compiler_params=_params(),
        name="in_proj",
    )(xs, xs, xs, prm["mod"], prm["norm1"], prm["w_in"][0], prm["head_sum"], prm["head_bcast"], prm["gain"],
      cos_t, sin_t,
      prm["conv_sc"], prm["pool_mat"], prm["pool_scale"])


def _kv_proj_kernel(x_ref, m_ref, norm_ref, w_ref, hs_ref, hb_ref, gain_ref, k_ref, v_ref, *, tq):
    h = _rms_mod(x_ref[0], norm_ref[...], m_ref[0:1, :], m_ref[1:2, :]).astype(BF16)
    z = _dot(h, w_ref[...])
    low_half = lax.broadcasted_iota(jnp.int32, (tq, LANES), 1) < HEAD_DIM
    k0, k1 = _dup_heads(_head_rms(z[:, 0:KV_WIDTH], hs_ref, hb_ref, gain_ref[...]), low_half)
    k_ref[0, :, 0:LANES] = k0.astype(BF16)
    k_ref[0, :, LANES:2 * LANES] = k1.astype(BF16)
    v0, v1 = _dup_heads(z[:, KV_WIDTH:2 * KV_WIDTH], low_half)
    v_ref[0, :, 0:LANES] = v0.astype(BF16)
    v_ref[0, :, LANES:2 * LANES] = v1.astype(BF16)


def _kv_proj(xs, prm, layer, mod_row, *, tq):
    b, n, _ = xs.shape
    kv_cols = 2 * KV_WIDTH
    return pl.pallas_call(
        functools.partial(_kv_proj_kernel, tq=tq),
        out_shape=[jax.ShapeDtypeStruct((b, n, 2 * LANES), BF16)] * 2,
        grid=(n // tq, b),
        in_specs=[pl.BlockSpec((1, tq, D_MODEL), lambda i, bb: (bb, i, 0)),
                  _mod_spec(layer, mod_row),
                  _layer_spec((1, D_MODEL), layer),
                  _pair_spec(prm["w_in"], (D_MODEL, kv_cols), (0, OFF_K // kv_cols)),
                  _fixed_spec((KV_WIDTH, LANES), (0, 0)),
                  _fixed_spec((LANES, KV_WIDTH), (0, 0)),
                  _layer_spec((1, KV_WIDTH), layer, (0, Q_WIDTH // KV_WIDTH))],
        out_specs=[pl.BlockSpec((1, tq, 2 * LANES), lambda i, bb: (bb, i, 0))] * 2,
        compiler_params=_params(),
        name="kv_proj",
    )(xs, prm["mod"], prm["norm1"], prm["w_in"][0], prm["head_sum"], prm["head_bcast"], prm["gain"])


def _attn_kernel(q_ref, *refs, tq):
    *kv_refs, o_ref = refs
    sub = min(tq, ATTN_SUB)
    lane = lax.broadcasted_iota(jnp.int32, (sub, LANES), 1)
    low_half = lane < HEAD_DIM
    sources = [(kv_refs[2 * a], kv_refs[2 * a + 1]) for a in range(len(kv_refs) // 2)]
    for j in range(Q_WIDTH // LANES):
        g = (2 * j) // (N_HEADS // N_KV_HEADS)
        gl = slice(g * LANES, (g + 1) * LANES)
        for r in range(tq // sub):
            rows = slice(r * sub, (r + 1) * sub)
            qp = q_ref[0, rows, j * LANES:(j + 1) * LANES]
            zero = jnp.zeros_like(qp)
            q2 = jnp.concatenate([jnp.where(low_half, qp, zero), jnp.where(low_half, zero, qp)], axis=0)
            scores = [_dot_nt(q2, k_ref[0, :, gl]) for k_ref, _ in sources]
            row_max = functools.reduce(jnp.maximum, [jnp.max(s, axis=-1, keepdims=True) for s in scores])
            o, denom = None, None
            for s, (_, v_ref) in zip(scores, sources):
                p = jnp.exp(s - row_max)
                part = jnp.sum(p, axis=-1, keepdims=True)
                term = _dot(p.astype(BF16), v_ref[0, :, gl])
                o, denom = (term, part) if o is None else (o + term, denom + part)
            o = o / denom
            o_ref[0, rows, j * LANES:(j + 1) * LANES] = jnp.where(low_half, o[:sub], o[sub:]).astype(BF16)


def _attention(q, kv_sources, *, tq):
    b, n, _ = q.shape
    kv_specs, kv_args = [], []
    for k, v in kv_sources:
        for a in (k, v):
            kv_specs.append(pl.BlockSpec((1, a.shape[1], 2 * LANES), lambda bb, i: (bb, 0, 0)))
            kv_args.append(a)
    return pl.pallas_call(
        functools.partial(_attn_kernel, tq=tq),
        out_shape=jax.ShapeDtypeStruct((b, n, Q_WIDTH), BF16),
        grid=(b, n // tq),
        in_specs=[pl.BlockSpec((1, tq, Q_WIDTH), lambda bb, i: (bb, i, 0))] + kv_specs,
        out_specs=pl.BlockSpec((1, tq, Q_WIDTH), lambda bb, i: (bb, i, 0)),
        compiler_params=_params(),
        name="attention",
    )(q, *kv_args)


def _fourier_kernel(x_ref, wc_ref, cn_ref, nsn_ref, o_ref, t_scr, *, n, tr):
    t_scr[...] = _dot(x_ref[0], wc_ref[...].astype(BF16)).astype(BF16)
    scale = 1.0 / math.sqrt(n * FOURIER_GROUP)
    for r in range(n // tr):
        rows = slice(r * tr, (r + 1) * tr)
        acc = (_dot(cn_ref[rows, :].astype(BF16), t_scr[:, 0:FOURIER_WIDTH])
               + _dot(nsn_ref[rows, :].astype(BF16), t_scr[:, FOURIER_WIDTH:2 * FOURIER_WIDTH]))
        o_ref[0, rows, :] = (acc * scale).astype(BF16)


def _fourier(xf, wc, cn, nsn):
    b, n, _ = xf.shape
    tr = min(n, 512)
    return pl.pallas_call(
        functools.partial(_fourier_kernel, n=n, tr=tr),
        out_shape=jax.ShapeDtypeStruct((b, n, FOURIER_WIDTH), BF16),
        grid=(b, 1),
        in_specs=[pl.BlockSpec((1, n, FOURIER_WIDTH), lambda bb, i: (bb, 0, 0)),
                  _fixed_spec((FOURIER_WIDTH, 2 * FOURIER_WIDTH), (0, 0)),
                  _fixed_spec((n, n), (0, 0)),
                  _fixed_spec((n, n), (0, 0))],
        out_specs=pl.BlockSpec((1, n, FOURIER_WIDTH), lambda bb, i: (bb, 0, 0)),
        scratch_shapes=[pltpu.VMEM((n, 2 * FOURIER_WIDTH), BF16)],
        compiler_params=_params(),
        name="fourier",
    )(xf, wc, cn, nsn)


N_MIX_INPUTS = 14


def _mix_kernel(*refs, nb, n_cast):
    (x_ref, m_ref, norm_ref, oa_ref, sc_ref, xf_ref, yp_ref,
     wg01_ref, wg23_ref, wa_ref, wsc_ref, wf_ref, wp_ref, wo_ref) = refs[:N_MIX_INPUTS]
    cast_in = refs[N_MIX_INPUTS:N_MIX_INPUTS + n_cast]
    out_ref = refs[N_MIX_INPUTS + n_cast]
    cast_out = refs[N_MIX_INPUTS + n_cast + 1:N_MIX_INPUTS + 2 * n_cast + 1]
    h_scr, y_scr = refs[-2:]
    _run_casts(cast_in, cast_out)
    branches = ((oa_ref, wa_ref), (sc_ref, wsc_ref), (xf_ref, wf_ref), (yp_ref, wp_ref))
    gate_w = (wg01_ref, wg23_ref)
    norm = norm_ref[...]

    def mod(e):
        return m_ref.at[min(e, m_ref.shape[0] - 1)]

    def norm_stage(e):
        m = mod(e)
        h_scr[e] = _rms_mod(x_ref[e], norm, m[0:1, :], m[1:2, :]).astype(BF16)

    norm_stage(0)
    for e in range(nb):
        h = h_scr[e]
        for j in range(D_MODEL // FF_CHUNK):
            cols = slice(j * FF_CHUNK, (j + 1) * FF_CHUNK)
            y = None
            for bi, (a_ref, w_ref) in enumerate(branches):
                off = (bi % 2) * D_MODEL + j * FF_CHUNK
                gate = 0.5 + 0.5 * jnp.tanh(0.5 * _dot(h, gate_w[bi // 2][:, off:off + FF_CHUNK]))
                term = gate * _dot(a_ref[e], w_ref[:, cols])
                y = term if y is None else y + term
            y_scr[e, :, cols] = y.astype(BF16)
            if j == 0 and e + 1 < nb:
                norm_stage(e + 1)
        out_ref[e] = x_ref[e] + mod(e)[2:3, :] * _dot(y_scr[e], wo_ref[...])


def _mix(xs, prm, layer, mod_row, oa, sc, xf, yp, *, tq, casts=()):
    b, n, _ = xs.shape
    nb = MIX_BATCH
    half_gates = N_BRANCHES * D_MODEL // 2
    grid = (n // tq, b // nb)
    cast_in, cast_out, cast_shapes = _cast_plumbing(casts, grid[0] * grid[1], lambda i, bb: i * grid[1] + bb)

    def tok(w):
        return pl.BlockSpec((nb, tq, w), lambda i, bb: (bb, i, 0))

    in_specs = [tok(D_MODEL), _mod_spec_multi(layer, mod_row, nb), _layer_spec((1, D_MODEL), layer),
                tok(Q_WIDTH), tok(SC_WIDTH), tok(FOURIER_WIDTH), tok(POOL_WIDTH),
                _pair_spec(prm["w_in"], (D_MODEL, half_gates), (0, OFF_G // half_gates)),
                _pair_spec(prm["w_in"], (D_MODEL, half_gates), (0, OFF_G // half_gates + 1)),
                _layer_spec((Q_WIDTH, D_MODEL), layer), _layer_spec((SC_WIDTH, D_MODEL), layer),
                _layer_spec((FOURIER_WIDTH, D_MODEL), layer), _layer_spec((POOL_WIDTH, D_MODEL), layer),
                _layer_spec((D_MODEL, D_MODEL), layer)]
    assert len(in_specs) == N_MIX_INPUTS
    return pl.pallas_call(
        functools.partial(_mix_kernel, nb=nb, n_cast=len(casts)),
        out_shape=[jax.ShapeDtypeStruct((b, n, D_MODEL), F32)] + cast_shapes,
        grid=grid,
        in_specs=in_specs + cast_in,
        out_specs=[tok(D_MODEL)] + cast_out,
        scratch_shapes=[pltpu.VMEM((nb, tq, D_MODEL), BF16), pltpu.VMEM((nb, tq, D_MODEL), BF16)],
        compiler_params=_params(),
        name="mix",
    )(xs, prm["mod"], prm["norm1"], oa, sc, xf, yp, prm["w_in"][0], prm["w_in"][0],
      prm["w_br_attn"], prm["w_br_sc"], prm["w_br_f"], prm["w_br_p"], prm["w_out"], *[a for a, _ in casts])


def _ffn_kernel(xp_ref, xc_ref, xn_ref, m_ref, norm_ref, wu_ref, cw_ref, wd_ref, fn_ref, out_ref,
                h_scr, hb_scr, u0_scr, u1_scr, *, tq, final, nb):
    norm = norm_ref[...]
    ext = tq + 2 * HALO

    def mod(e):
        return m_ref.at[min(e, m_ref.shape[0] - 1)]

    def cols(c, half):
        return slice(half * D_FF + c * FF_CHUNK, half * D_FF + (c + 1) * FF_CHUNK)

    def norm_stage(e):
        m = mod(e)
        shift, scale = m[3:4, :], m[4:5, :]
        _fill_halo_rows(h_scr.at[e], xp_ref.at[e:e + 1], xc_ref.at[e:e + 1], xn_ref.at[e:e + 1],
                        lambda xv: _rms_mod(xv, norm, shift, scale), tq)
        hb_scr[e] = h_scr[e].astype(BF16)

    def up(e, c, u_scr):
        hb = hb_scr[e]
        u_scr[0] = _dot(hb, wu_ref[:, cols(c, 0)])
        u_scr[1] = _dot(hb, wu_ref[:, cols(c, 1)])

    def conv3(u, w):
        full = pltpu.roll(u, 1, 0) * w[0:1, :] + u * w[1:2, :] + pltpu.roll(u, ext - 1, 0) * w[2:3, :]
        return full[HALO:HALO + tq]

    def down(c, u_scr):
        a = conv3(u_scr[0], cw_ref[:, cols(c, 0)])
        bv = conv3(u_scr[1], cw_ref[:, cols(c, 1)])
        half = 0.5 * a
        act = ((half + half * jnp.tanh(half)) * bv).astype(BF16)
        return _dot(act, wd_ref[c * FF_CHUNK:(c + 1) * FF_CHUNK, :])

    bufs = (u0_scr, u1_scr)
    units = [(e, c) for e in range(nb) for c in range(N_FF_CHUNKS)]
    norm_stage(0)
    up(0, 0, u0_scr)
    acc = None
    for k, (e, c) in enumerate(units):
        if k + 1 < len(units):
            e_next, c_next = units[k + 1]
            if c_next == 0:
                norm_stage(e_next)
            up(e_next, c_next, bufs[(k + 1) % 2])
        d = down(c, bufs[k % 2])
        acc = d if c == 0 else acc + d
        if c == N_FF_CHUNKS - 1:
            out = xc_ref[e] + mod(e)[5:6, :] * acc
            if final:
                ms = jnp.mean(out * out, axis=-1, keepdims=True)
                out = out * lax.rsqrt(ms + NORM_EPS) * fn_ref[...]
            out_ref[e] = out


def _ffn(xs, prm, layer, mod_row, *, tq, final):
    b, n, _ = xs.shape
    nb = FFN_BATCH
    return pl.pallas_call(
        functools.partial(_ffn_kernel, tq=tq, final=final, nb=nb),
        out_shape=jax.ShapeDtypeStruct((b, n, D_MODEL), F32),
        grid=(n // tq, b // nb),
        in_specs=_halo_specs(n, tq, D_MODEL, nb) + [
            _mod_spec_multi(layer, mod_row, nb),
            _layer_spec((1, D_MODEL), layer),
            _pair_spec(prm["w_up"], (D_MODEL, 2 * D_FF)),
            _layer_spec((3, 2 * D_FF), layer),
            _pair_spec(prm["w_down"], (D_FF, D_MODEL)),
            _fixed_spec((1, D_MODEL), (0, 0))],
        out_specs=pl.BlockSpec((nb, tq, D_MODEL), lambda i, bb: (bb, i, 0)),
        scratch_shapes=[pltpu.VMEM((nb, tq + 2 * HALO, D_MODEL), F32),
                        pltpu.VMEM((nb, tq + 2 * HALO, D_MODEL), BF16),
                        pltpu.VMEM((2, tq + 2 * HALO, FF_CHUNK), F32),
                        pltpu.VMEM((2, tq + 2 * HALO, FF_CHUNK), F32)],
        compiler_params=_params(),
        name="ffn",
    )(xs, xs, xs, prm["mod"], prm["norm2"], prm["w_up"][0], prm["w_conv_ffn"], prm["w_down"][0],
      prm["final_norm"])


def _rope_tables(n):
    d = np.arange(LANES)
    hd = d % HEAD_DIM
    axis = hd // (2 * AXIS_PAIRS)
    half = (hd % (2 * AXIS_PAIRS)) // AXIS_PAIRS
    pair = hd % AXIS_PAIRS
    t = np.arange(n)
    pos = np.where(axis[None, :] == 0, (t // GRID_W)[:, None], (t % GRID_W)[:, None]).astype(np.float64)
    inv = ROPE_THETA ** (-np.arange(AXIS_PAIRS, dtype=np.float64) / AXIS_PAIRS)
    ang = pos * inv[pair][None, :]
    sin = np.sin(ang)
    return (jnp.asarray(np.cos(ang), dtype=F32),
            jnp.asarray(np.where(half[None, :] == 0, -sin, sin), dtype=F32))


def _dft_tables(n):
    k = np.arange(n, dtype=np.int64)
    ang = ((k[:, None] * k[None, :]) % n).astype(np.float64) * (2.0 * math.pi / n)
    return jnp.asarray(np.cos(ang), dtype=F32), jnp.asarray(-np.sin(ang), dtype=F32)


def _channel_dft():
    c = np.arange(FOURIER_WIDTH, dtype=np.int64)
    same = (c[:, None] // FOURIER_GROUP) == (c[None, :] // FOURIER_GROUP)
    mc = ((c[:, None] % FOURIER_GROUP) * (c[None, :] % FOURIER_GROUP)) % FOURIER_GROUP
    ang = mc.astype(np.float64) * (2.0 * math.pi / FOURIER_GROUP)
    both = np.concatenate([np.where(same, np.cos(ang), 0.0), np.where(same, np.sin(ang), 0.0)], axis=1)
    return jnp.asarray(both, dtype=F32)


def _head_matrices():
    r = np.arange(QK_WIDTH) // HEAD_DIM
    m = (r[:, None] == np.arange(LANES)[None, :]).astype(np.float32)
    return jnp.asarray(m.astype(BF16)), jnp.asarray(m.T.copy().astype(BF16))


def _block_diag(mats):
    nl, g, a, b = mats.shape
    eye = jnp.eye(g, dtype=mats.dtype)
    return (mats[:, :, :, None, :] * eye[None, :, None, :, None]).reshape(nl, g * a, g * b)


def kernel(x, c, ctx, c_ctx, w_mod, b_mod, norm1, norm2, w_in, conv_sc, qk_gain, pool_mat, pool_scale,
           w_br_attn, w_br_sc, w_br_f, w_br_p, w_out, w_up, w_conv_ffn, w_down, final_norm):
    bsz, n_lat, _ = x.shape
    n_ctx = ctx.shape[1]
    nl = w_in.shape[0]
    tq_lat = min(n_lat, TOKEN_TILE)
    tq_ctx = min(n_ctx, TOKEN_TILE)
    tq_attn = min(n_lat, ATTN_TILE)
    tq_ffn_lat = min(n_lat, FFN_TILE)
    tq_ffn_ctx = min(n_ctx, FFN_TILE)

    mod_rows = 16
    ctx_row = bsz
    cvec = jnp.concatenate([c, c_ctx[None], jnp.zeros((mod_rows - bsz - 1, D_MODEL), F32)], axis=0)
    gain = jnp.concatenate([jnp.tile(qk_gain[:, 0], (1, N_HEADS)) * (HEAD_DIM ** -0.5),
                            jnp.tile(qk_gain[:, 1], (1, N_KV_HEADS))], axis=1)
    head_sum, head_bcast = _head_matrices()
    prm = {
        "mod": _modulation(cvec, w_mod, b_mod),
        "norm1": norm1.reshape(nl, 1, D_MODEL),
        "norm2": norm2.reshape(nl, 1, D_MODEL),
        "w_in": (w_in[:1].astype(BF16), 0),
        "head_sum": head_sum,
        "head_bcast": head_bcast,
        "gain": gain.reshape(nl, 1, QK_WIDTH),
        "conv_sc": conv_sc,
        "pool_mat": _block_diag(pool_mat).astype(BF16),
        "pool_scale": pool_scale.reshape(nl, 1, POOL_WIDTH),
        "w_br_attn": w_br_attn.astype(BF16),
        "w_br_sc": w_br_sc.astype(BF16),
        "w_br_f": w_br_f.astype(BF16),
        "w_br_p": w_br_p.astype(BF16),
        "w_out": w_out.astype(BF16),
        "w_conv_ffn": w_conv_ffn,
        "final_norm": final_norm.reshape(1, D_MODEL),
    }
    cos_t, sin_t = _rope_tables(n_lat)
    wc = _channel_dft()
    dft_lat = _dft_tables(n_lat)
    dft_ctx = _dft_tables(n_ctx)

    lat, cx = x, ctx
    for l in range(nl):
        last = l == nl - 1
        if last:
            kc, vc = _kv_proj(cx, prm, l, ctx_row, tq=tq_ctx)
        else:
            qc, kc, vc, scc, xfc, ypc = _in_proj(cx, prm, l, ctx_row, cos_t, sin_t, tq=tq_ctx, rope=False)
        ql, kl, vl, scl, xfl, ypl = _in_proj(lat, prm, l, None, cos_t, sin_t, tq=tq_lat, rope=True)
        oa = _attention(ql, [(kl, vl), (kc, vc)], tq=tq_attn)
        xfour = _fourier(xfl, wc, *dft_lat)
        lat, w_up_bf, w_down_bf, *next_w_in = _mix(
            lat, prm, l, None, oa, scl, xfour, ypl, tq=tq_lat,
            casts=[(w_up, l), (w_down, l)] + ([] if last else [(w_in, l + 1)]))
        prm["w_up"], prm["w_down"] = (w_up_bf, 0), (w_down_bf, 0)
        lat = _ffn(lat, prm, l, None, tq=tq_ffn_lat, final=last)

        if not last:
            oac = _attention(qc, [(kc, vc)], tq=tq_ctx)
            xfourc = _fourier(xfc, wc, *dft_ctx)
            cx, = _mix(cx, prm, l, ctx_row, oac, scc, xfourc, ypc, tq=tq_ctx)
            cx = _ffn(cx, prm, l, ctx_row, tq=tq_ffn_ctx, final=False)
            prm["w_in"] = (next_w_in[0], 0)
    return lat
```

```python
import functools
import math

import numpy as np
import jax
import jax.numpy as jnp
from jax import lax
from jax.experimental import pallas as pl
from jax.experimental.pallas import tpu as pltpu

D_MODEL = 1024
DEPTH = 2
GRID_W = 64
NORM_EPS = 1e-6
N_HEADS = 8
N_KV_HEADS = 2
HEAD_DIM = 64
Q_WIDTH = N_HEADS * HEAD_DIM
KV_WIDTH = N_KV_HEADS * HEAD_DIM
QK_WIDTH = Q_WIDTH + KV_WIDTH
ROPE_THETA = 10000.0
AXIS_PAIRS = HEAD_DIM // 4
SC_WIDTH = 256
FOURIER_GROUPS = 4
FOURIER_WIDTH = 256
FOURIER_GROUP = FOURIER_WIDTH // FOURIER_GROUPS
POOL_WINDOWS = (2, 4, 8, 16)
POOL_WIDTH = 256
POOL_GROUP = POOL_WIDTH // len(POOL_WINDOWS)
N_BRANCHES = 4
D_FF = 2816

OFF_K = Q_WIDTH
OFF_V = OFF_K + KV_WIDTH
OFF_SC = OFF_V + KV_WIDTH
OFF_F = OFF_SC + 3 * SC_WIDTH
OFF_P = OFF_F + FOURIER_WIDTH
OFF_G = OFF_P + POOL_WIDTH
IN_WIDTH = OFF_G + N_BRANCHES * D_MODEL

LANES = 128
HALO = 8
FF_CHUNK = 256
N_FF_CHUNKS = D_FF // FF_CHUNK
ATTN_SUB = 256
TOKEN_TILE = 512
IN_PROJ_BATCH = 4
MOD_COLS = 2048
MIX_BATCH = 2
FFN_BATCH = 4
ATTN_TILE = 1024
FFN_TILE = 256
N_MOD = 6
VMEM_LIMIT = 62 * 1024 * 1024

F32 = jnp.float32
BF16 = jnp.bfloat16


def _dot(a, b):
    return jnp.dot(a, b, preferred_element_type=F32)


def _dot_nt(a, b):
    return lax.dot_general(a, b, (((1,), (1,)), ((), ())), preferred_element_type=F32)


def _fixed_spec(block, index):
    return pl.BlockSpec(block, lambda *_: index, pipeline_mode=pl.Buffered(1))


def _layer_spec(tail, layer, tail_index=None):
    tail_index = (0,) * len(tail) if tail_index is None else tail_index
    return _fixed_spec((None,) + tuple(tail), (layer,) + tuple(tail_index))


def _pair_spec(pair, tail, tail_index=None):
    return _layer_spec(tail, pair[1], tail_index)


def _cast_plumbing(casts, steps, flat_index):
    in_specs, out_specs, out_shapes = [], [], []
    for arr, layer in casts:
        _, r, c = arr.shape
        rows = r // steps
        assert rows * steps == r and rows % 16 == 0, (arr.shape, steps)
        in_specs.append(pl.BlockSpec((None, rows, c), lambda i, b, layer=layer: (layer, flat_index(i, b), 0)))
        out_specs.append(pl.BlockSpec((None, rows, c), lambda i, b: (0, flat_index(i, b), 0)))
        out_shapes.append(jax.ShapeDtypeStruct((1, r, c), BF16))
    return in_specs, out_specs, out_shapes


def _run_casts(cast_in, cast_out):
    for src, dst in zip(cast_in, cast_out):
        dst[...] = src[...].astype(BF16)


def _mod_spec(layer, row):
    if row is None:
        return pl.BlockSpec((None, None, N_MOD, D_MODEL), lambda i, b: (layer, b, 0, 0))
    return _fixed_spec((None, None, N_MOD, D_MODEL), (layer, row, 0, 0))


def _mod_spec_multi(layer, row, nb):
    if row is None:
        return pl.BlockSpec((None, nb, N_MOD, D_MODEL), lambda i, b: (layer, b, 0, 0))
    return _fixed_spec((None, 1, N_MOD, D_MODEL), (layer, row, 0, 0))


def _params():
    return pltpu.CompilerParams(dimension_semantics=("arbitrary", "arbitrary"),
                                vmem_limit_bytes=VMEM_LIMIT)


def _rms_mod(xv, norm, shift, scale):
    ms = jnp.mean(xv * xv, axis=-1, keepdims=True)
    y = xv * lax.rsqrt(ms + NORM_EPS) * norm
    return y * (1.0 + scale) + shift


def _halo_specs(n, tq, width, nb=1):
    nblk = n // HALO
    per = tq // HALO
    prev = pl.BlockSpec((nb, HALO, width), lambda i, b: (b, jnp.maximum(i * per - 1, 0), 0))
    cur = pl.BlockSpec((nb, tq, width), lambda i, b: (b, i, 0))
    nxt = pl.BlockSpec((nb, HALO, width), lambda i, b: (b, jnp.minimum((i + 1) * per, nblk - 1), 0))
    return [prev, cur, nxt]


def _fill_halo_rows(h_scr, xp_ref, xc_ref, xn_ref, fn, tq):
    i = pl.program_id(0)
    last = pl.num_programs(0) - 1
    h_scr[HALO:HALO + tq, :] = fn(xc_ref[0])
    h_scr[0:HALO, :] = jnp.where(i > 0, fn(xp_ref[0]), 0.0)
    h_scr[HALO + tq:2 * HALO + tq, :] = jnp.where(i < last, fn(xn_ref[0]), 0.0)


def _split_bf16(a):
    hi = a.astype(BF16)
    return hi, (a - hi.astype(F32)).astype(BF16)


def _head_rms(z, sum_ref, bcast_ref, gain):
    sq_hi, sq_lo = _split_bf16(z * z)
    ss = _dot(sq_hi, sum_ref[...]) + _dot(sq_lo, sum_ref[...])
    r_hi, r_lo = _split_bf16(lax.rsqrt(ss * (1.0 / HEAD_DIM) + NORM_EPS))
    return z * (_dot(r_hi, bcast_ref[...]) + _dot(r_lo, bcast_ref[...])) * gain


def _dup_heads(a, low_half):
    sw = pltpu.roll(a, HEAD_DIM, 1)
    return jnp.where(low_half, a, sw), jnp.where(low_half, sw, a)


def _mod_kernel(c_ref, w_ref, b_ref, o_ref):
    cv = c_ref[...]
    a = (cv * jax.nn.sigmoid(cv)).astype(BF16)
    o_ref[0] = _dot(a, w_ref[0].astype(BF16)) + b_ref[0]


def _modulation(cvec, w_mod, b_mod):
    rows = cvec.shape[0]
    nl = w_mod.shape[0]
    out = pl.pallas_call(
        _mod_kernel,
        out_shape=jax.ShapeDtypeStruct((nl, rows, N_MOD * D_MODEL), F32),
        grid=(nl, N_MOD * D_MODEL // MOD_COLS),
        in_specs=[pl.BlockSpec((rows, D_MODEL), lambda l, j: (0, 0)),
                  pl.BlockSpec((1, D_MODEL, MOD_COLS), lambda l, j: (l, 0, j)),
                  pl.BlockSpec((1, 1, MOD_COLS), lambda l, j: (l, 0, j))],
        out_specs=pl.BlockSpec((1, rows, MOD_COLS), lambda l, j: (l, 0, j)),
        compiler_params=_params(),
        name="adaln_mod",
    )(cvec, w_mod, b_mod.reshape(nl, 1, N_MOD * D_MODEL))
    return out.reshape(nl, rows, N_MOD, D_MODEL)


def _in_proj_kernel(xp_ref, xc_ref, xn_ref, m_ref, norm_ref, w_ref, hs_ref, hb_ref, gain_ref, cos_ref, sin_ref,
                    conv_ref, pm_ref, ps_ref,
                    q_ref, k_ref, v_ref, sc_ref, xf_ref, yp_ref,
                    h_scr, z_scr, *, n, tq, rope, nb):
    i = pl.program_id(0)
    norm = norm_ref[...]

    def norm_stage(e):
        m = m_ref.at[min(e, m_ref.shape[0] - 1)]
        shift, scale = m[0:1, :], m[1:2, :]
        _fill_halo_rows(h_scr.at[e % 2], xp_ref.at[e:e + 1], xc_ref.at[e:e + 1], xn_ref.at[e:e + 1],
                        lambda xv: _rms_mod(xv, norm, shift, scale), tq)

    norm_stage(0)
    for e in range(nb):
        z_scr[e % 2] = _dot(h_scr[e % 2].astype(BF16), w_ref[...])
        if e + 1 < nb:
            norm_stage(e + 1)
        _local_mixers(e, z_scr.at[e % 2], hs_ref, hb_ref, gain_ref, cos_ref, sin_ref, conv_ref, pm_ref, ps_ref,
                      q_ref, k_ref, v_ref, sc_ref, xf_ref, yp_ref, i=i, n=n, tq=tq, rope=rope)


def _local_mixers(e, z_scr, hs_ref, hb_ref, gain_ref, cos_ref, sin_ref, conv_ref, pm_ref, ps_ref,
                  q_ref, k_ref, v_ref, sc_ref, xf_ref, yp_ref, *, i, n, tq, rope):
    lane = lax.broadcasted_iota(jnp.int32, (tq, LANES), 1)
    low_half = lane < HEAD_DIM
    rows = slice(HALO, HALO + tq)

    y = _head_rms(z_scr[rows, 0:QK_WIDTH], hs_ref, hb_ref, gain_ref[...])
    second_half = (lane & AXIS_PAIRS) != 0
    for c in range(QK_WIDTH // LANES):
        yc = y[:, c * LANES:(c + 1) * LANES]
        if rope:
            partner = jnp.where(second_half, pltpu.roll(yc, AXIS_PAIRS, 1),
                                pltpu.roll(yc, LANES - AXIS_PAIRS, 1))
            yc = yc * cos_ref[...] + partner * sin_ref[...]
        if c < Q_WIDTH // LANES:
            q_ref[e, :, c * LANES:(c + 1) * LANES] = yc.astype(BF16)
        else:
            k0, k1 = _dup_heads(yc, low_half)
            k_ref[e, :, 0:LANES] = k0.astype(BF16)
            k_ref[e, :, LANES:2 * LANES] = k1.astype(BF16)
    v0, v1 = _dup_heads(z_scr[rows, OFF_V:OFF_SC], low_half)
    v_ref[e, :, 0:LANES] = v0.astype(BF16)
    v_ref[e, :, LANES:2 * LANES] = v1.astype(BF16)

    def u_at(off):
        r = slice(HALO + off, HALO + off + tq)
        return z_scr[r, OFF_SC + SC_WIDTH:OFF_SC + 2 * SC_WIDTH] * z_scr[r, OFF_SC + 2 * SC_WIDTH:OFF_F]

    conv = u_at(-1) * conv_ref[0:1, :] + u_at(0) * conv_ref[1:2, :] + u_at(1) * conv_ref[2:3, :]
    sc_ref[e] = (z_scr[rows, OFF_SC:OFF_SC + SC_WIDTH] * conv).astype(BF16)

    xf_ref[e] = z_scr[rows, OFF_F:OFF_P].astype(BF16)

    t = i * tq + lax.broadcasted_iota(jnp.int32, (tq, LANES), 0)

    def p_at(off, c):
        return z_scr[HALO + off:HALO + off + tq, OFF_P + c * LANES:OFF_P + (c + 1) * LANES]

    def window_mean(total, w_low, w_high):
        w = jnp.where(low_half, w_low, w_high)
        left, right = (w - 1) // 2, w // 2
        cnt = jnp.minimum(t + right + 1, n) - jnp.maximum(t - left, 0)
        return total / cnt.astype(F32)

    p0 = p_at(0, 0)
    s2 = p0 + p_at(1, 0)
    s4 = s2 + p_at(-1, 0) + p_at(2, 0)
    pooled0 = window_mean(jnp.where(low_half, s2, s4), POOL_WINDOWS[0], POOL_WINDOWS[1]) - p0
    p1 = p_at(0, 1)
    s8 = p1
    for off in (-3, -2, -1, 1, 2, 3, 4):
        s8 = s8 + p_at(off, 1)
    s16 = s8
    for off in (-7, -6, -5, -4, 5, 6, 7, 8):
        s16 = s16 + p_at(off, 1)
    pooled1 = window_mean(jnp.where(low_half, s8, s16), POOL_WINDOWS[2], POOL_WINDOWS[3]) - p1
    pooled = jnp.concatenate([pooled0, pooled1], axis=1).astype(BF16)
    yp_ref[e] = (_dot(pooled, pm_ref[...]) * ps_ref[...]).astype(BF16)


def _in_proj(xs, prm, layer, mod_row, cos_t, sin_t, *, tq, rope):
    b, n, _ = xs.shape
    out_w = (Q_WIDTH, 2 * LANES, 2 * LANES, SC_WIDTH, FOURIER_WIDTH, POOL_WIDTH)
    nb = IN_PROJ_BATCH
    kern = functools.partial(_in_proj_kernel, n=n, tq=tq, rope=rope, nb=nb)
    return pl.pallas_call(
        kern,
        out_shape=[jax.ShapeDtypeStruct((b, n, w), BF16) for w in out_w],
        grid=(n // tq, b // nb),
        in_specs=_halo_specs(n, tq, D_MODEL, nb) + [
            _mod_spec_multi(layer, mod_row, nb),
            _layer_spec((1, D_MODEL), layer),
            _pair_spec(prm["w_in"], (D_MODEL, OFF_G)),
            _fixed_spec((QK_WIDTH, LANES), (0, 0)),
            _fixed_spec((LANES, QK_WIDTH), (0, 0)),
            _layer_spec((1, QK_WIDTH), layer),
            pl.BlockSpec((tq, LANES), lambda i, bb: (i, 0)),
            pl.BlockSpec((tq, LANES), lambda i, bb: (i, 0)),
            _layer_spec((3, SC_WIDTH), layer),
            _layer_spec((POOL_WIDTH, POOL_WIDTH), layer),
            _layer_spec((1, POOL_WIDTH), layer),
        ],
        out_specs=[pl.BlockSpec((nb, tq, w), lambda i, bb: (bb, i, 0)) for w in out_w],
        scratch_shapes=[pltpu.VMEM((2, tq + 2 * HALO, D_MODEL), F32),
                        pltpu.VMEM((2, tq + 2 * HALO, OFF_G), F32)],
        compiler_params=_params(),
        name="in_proj",
    )(xs, xs, xs, prm["mod"], prm["norm1"], prm["w_in"][0], prm["head_sum"], prm["head_bcast"], prm["gain"],
      cos_t, sin_t,
      prm["conv_sc"], prm["pool_mat"], prm["pool_scale"])


def _kv_proj_kernel(x_ref, m_ref, norm_ref, w_ref, hs_ref, hb_ref, gain_ref, k_ref, v_ref, *, tq):
    h = _rms_mod(x_ref[0], norm_ref[...], m_ref[0:1, :], m_ref[1:2, :]).astype(BF16)
    z = _dot(h, w_ref[...])
    low_half = lax.broadcasted_iota(jnp.int32, (tq, LANES), 1) < HEAD_DIM
    k0, k1 = _dup_heads(_head_rms(z[:, 0:KV_WIDTH], hs_ref, hb_ref, gain_ref[...]), low_half)
    k_ref[0, :, 0:LANES] = k0.astype(BF16)
    k_ref[0, :, LANES:2 * LANES] = k1.astype(BF16)
    v0, v1 = _dup_heads(z[:, KV_WIDTH:2 * KV_WIDTH], low_half)
    v_ref[0, :, 0:LANES] = v0.astype(BF16)
    v_ref[0, :, LANES:2 * LANES] = v1.astype(BF16)


def _kv_proj(xs, prm, layer, mod_row, *, tq):
    b, n, _ = xs.shape
    kv_cols = 2 * KV_WIDTH
    return pl.pallas_call(
        functools.partial(_kv_proj_kernel, tq=tq),
        out_shape=[jax.ShapeDtypeStruct((b, n, 2 * LANES), BF16)] * 2,
        grid=(n // tq, b),
        in_specs=[pl.BlockSpec((1, tq, D_MODEL), lambda i, bb: (bb, i, 0)),
                  _mod_spec(layer, mod_row),
                  _layer_spec((1, D_MODEL), layer),
                  _pair_spec(prm["w_in"], (D_MODEL, kv_cols), (0, OFF_K // kv_cols)),
                  _fixed_spec((KV_WIDTH, LANES), (0, 0)),
                  _fixed_spec((LANES, KV_WIDTH), (0, 0)),
                  _layer_spec((1, KV_WIDTH), layer, (0, Q_WIDTH // KV_WIDTH))],
        out_specs=[pl.BlockSpec((1, tq, 2 * LANES), lambda i, bb: (bb, i, 0))] * 2,
        compiler_params=_params(),
        name="kv_proj",
    )(xs, prm["mod"], prm["norm1"], prm["w_in"][0], prm["head_sum"], prm["head_bcast"], prm["gain"])


def _attn_kernel(q_ref, *refs, tq):
    *kv_refs, o_ref = refs
    sub = min(tq, ATTN_SUB)
    lane = lax.broadcasted_iota(jnp.int32, (sub, LANES), 1)
    low_half = lane < HEAD_DIM
    sources = [(kv_refs[2 * a], kv_refs[2 * a + 1]) for a in range(len(kv_refs) // 2)]
    for j in range(Q_WIDTH // LANES):
        g = (2 * j) // (N_HEADS // N_KV_HEADS)
        gl = slice(g * LANES, (g + 1) * LANES)
        for r in range(tq // sub):
            rows = slice(r * sub, (r + 1) * sub)
            qp = q_ref[0, rows, j * LANES:(j + 1) * LANES]
            zero = jnp.zeros_like(qp)
            q2 = jnp.concatenate([jnp.where(low_half, qp, zero), jnp.where(low_half, zero, qp)], axis=0)
            scores = [_dot_nt(q2, k_ref[0, :, gl]) for k_ref, _ in sources]
            row_max = functools.reduce(jnp.maximum, [jnp.max(s, axis=-1, keepdims=True) for s in scores])
            o, denom = None, None
            for s, (_, v_ref) in zip(scores, sources):
                p = jnp.exp(s - row_max)
                part = jnp.sum(p, axis=-1, keepdims=True)
                term = _dot(p.astype(BF16), v_ref[0, :, gl])
                o, denom = (term, part) if o is None else (o + term, denom + part)
            o = o / denom
            o_ref[0, rows, j * LANES:(j + 1) * LANES] = jnp.where(low_half, o[:sub], o[sub:]).astype(BF16)


def _attention(q, kv_sources, *, tq):
    b, n, _ = q.shape
    kv_specs, kv_args = [], []
    for k, v in kv_sources:
        for a in (k, v):
            kv_specs.append(pl.BlockSpec((1, a.shape[1], 2 * LANES), lambda bb, i: (bb, 0, 0)))
            kv_args.append(a)
    return pl.pallas_call(
        functools.partial(_attn_kernel, tq=tq),
        out_shape=jax.ShapeDtypeStruct((b, n, Q_WIDTH), BF16),
        grid=(b, n // tq),
        in_specs=[pl.BlockSpec((1, tq, Q_WIDTH), lambda bb, i: (bb, i, 0))] + kv_specs,
        out_specs=pl.BlockSpec((1, tq, Q_WIDTH), lambda bb, i: (bb, i, 0)),
        compiler_params=_params(),
        name="attention",
    )(q, *kv_args)


def _fourier_kernel(x_ref, wc_ref, cn_ref, nsn_ref, o_ref, t_scr, *, n, tr):
    t_scr[...] = _dot(x_ref[0], wc_ref[...].astype(BF16)).astype(BF16)
    scale = 1.0 / math.sqrt(n * FOURIER_GROUP)
    for r in range(n // tr):
        rows = slice(r * tr, (r + 1) * tr)
        acc = (_dot(cn_ref[rows, :].astype(BF16), t_scr[:, 0:FOURIER_WIDTH])
               + _dot(nsn_ref[rows, :].astype(BF16), t_scr[:, FOURIER_WIDTH:2 * FOURIER_WIDTH]))
        o_ref[0, rows, :] = (acc * scale).astype(BF16)


def _fourier(xf, wc, cn, nsn):
    b, n, _ = xf.shape
    tr = min(n, 512)
    return pl.pallas_call(
        functools.partial(_fourier_kernel, n=n, tr=tr),
        out_shape=jax.ShapeDtypeStruct((b, n, FOURIER_WIDTH), BF16),
        grid=(b, 1),
        in_specs=[pl.BlockSpec((1, n, FOURIER_WIDTH), lambda bb, i: (bb, 0, 0)),
                  _fixed_spec((FOURIER_WIDTH, 2 * FOURIER_WIDTH), (0, 0)),
                  _fixed_spec((n, n), (0, 0)),
                  _fixed_spec((n, n), (0, 0))],
        out_specs=pl.BlockSpec((1, n, FOURIER_WIDTH), lambda bb, i: (bb, 0, 0)),
        scratch_shapes=[pltpu.VMEM((n, 2 * FOURIER_WIDTH), BF16)],
        compiler_params=_params(),
        name="fourier",
    )(xf, wc, cn, nsn)


N_MIX_INPUTS = 14


def _mix_kernel(*refs, nb, n_cast):
    (x_ref, m_ref, norm_ref, oa_ref, sc_ref, xf_ref, yp_ref,
     wg01_ref, wg23_ref, wa_ref, wsc_ref, wf_ref, wp_ref, wo_ref) = refs[:N_MIX_INPUTS]
    cast_in = refs[N_MIX_INPUTS:N_MIX_INPUTS + n_cast]
    out_ref = refs[N_MIX_INPUTS + n_cast]
    cast_out = refs[N_MIX_INPUTS + n_cast + 1:N_MIX_INPUTS + 2 * n_cast + 1]
    h_scr, y_scr = refs[-2:]
    _run_casts(cast_in, cast_out)
    branches = ((oa_ref, wa_ref), (sc_ref, wsc_ref), (xf_ref, wf_ref), (yp_ref, wp_ref))
    gate_w = (wg01_ref, wg23_ref)
    norm = norm_ref[...]

    def mod(e):
        return m_ref.at[min(e, m_ref.shape[0] - 1)]

    def norm_stage(e):
        m = mod(e)
        h_scr[e] = _rms_mod(x_ref[e], norm, m[0:1, :], m[1:2, :]).astype(BF16)

    norm_stage(0)
    for e in range(nb):
        h = h_scr[e]
        for j in range(D_MODEL // FF_CHUNK):
            cols = slice(j * FF_CHUNK, (j + 1) * FF_CHUNK)
            y = None
            for bi, (a_ref, w_ref) in enumerate(branches):
                off = (bi % 2) * D_MODEL + j * FF_CHUNK
                gate = 0.5 + 0.5 * jnp.tanh(0.5 * _dot(h, gate_w[bi // 2][:, off:off + FF_CHUNK]))
                term = gate * _dot(a_ref[e], w_ref[:, cols])
                y = term if y is None else y + term
            y_scr[e, :, cols] = y.astype(BF16)
            if j == 0 and e + 1 < nb:
                norm_stage(e + 1)
        out_ref[e] = x_ref[e] + mod(e)[2:3, :] * _dot(y_scr[e], wo_ref[...])


def _mix(xs, prm, layer, mod_row, oa, sc, xf, yp, *, tq, casts=()):
    b, n, _ = xs.shape
    nb = MIX_BATCH
    half_gates = N_BRANCHES * D_MODEL // 2
    grid = (n // tq, b // nb)
    cast_in, cast_out, cast_shapes = _cast_plumbing(casts, grid[0] * grid[1], lambda i, bb: i * grid[1] + bb)

    def tok(w):
        return pl.BlockSpec((nb, tq, w), lambda i, bb: (bb, i, 0))

    in_specs = [tok(D_MODEL), _mod_spec_multi(layer, mod_row, nb), _layer_spec((1, D_MODEL), layer),
                tok(Q_WIDTH), tok(SC_WIDTH), tok(FOURIER_WIDTH), tok(POOL_WIDTH),
                _pair_spec(prm["w_in"], (D_MODEL, half_gates), (0, OFF_G // half_gates)),
                _pair_spec(prm["w_in"], (D_MODEL, half_gates), (0, OFF_G // half_gates + 1)),
                _layer_spec((Q_WIDTH, D_MODEL), layer), _layer_spec((SC_WIDTH, D_MODEL), layer),
                _layer_spec((FOURIER_WIDTH, D_MODEL), layer), _layer_spec((POOL_WIDTH, D_MODEL), layer),
                _layer_spec((D_MODEL, D_MODEL), layer)]
    assert len(in_specs) == N_MIX_INPUTS
    return pl.pallas_call(
        functools.partial(_mix_kernel, nb=nb, n_cast=len(casts)),
        out_shape=[jax.ShapeDtypeStruct((b, n, D_MODEL), F32)] + cast_shapes,
        grid=grid,
        in_specs=in_specs + cast_in,
        out_specs=[tok(D_MODEL)] + cast_out,
        scratch_shapes=[pltpu.VMEM((nb, tq, D_MODEL), BF16), pltpu.VMEM((nb, tq, D_MODEL), BF16)],
        compiler_params=_params(),
        name="mix",
    )(xs, prm["mod"], prm["norm1"], oa, sc, xf, yp, prm["w_in"][0], prm["w_in"][0],
      prm["w_br_attn"], prm["w_br_sc"], prm["w_br_f"], prm["w_br_p"], prm["w_out"], *[a for a, _ in casts])


def _ffn_kernel(xp_ref, xc_ref, xn_ref, m_ref, norm_ref, wu_ref, cw_ref, wd_ref, fn_ref, out_ref,
                h_scr, hb_scr, u0_scr, u1_scr, *, tq, final, nb):
    norm = norm_ref[...]
    ext = tq + 2 * HALO

    def mod(e):
        return m_ref.at[min(e, m_ref.shape[0] - 1)]

    def cols(c, half):
        return slice(half * D_FF + c * FF_CHUNK, half * D_FF + (c + 1) * FF_CHUNK)

    def norm_stage(e):
        m = mod(e)
        shift, scale = m[3:4, :], m[4:5, :]
        _fill_halo_rows(h_scr.at[e], xp_ref.at[e:e + 1], xc_ref.at[e:e + 1], xn_ref.at[e:e + 1],
                        lambda xv: _rms_mod(xv, norm, shift, scale), tq)
        hb_scr[e] = h_scr[e].astype(BF16)

    def up(e, c, u_scr):
        hb = hb_scr[e]
        u_scr[0] = _dot(hb, wu_ref[:, cols(c, 0)])
        u_scr[1] = _dot(hb, wu_ref[:, cols(c, 1)])

    def conv3(u, w):
        full = pltpu.roll(u, 1, 0) * w[0:1, :] + u * w[1:2, :] + pltpu.roll(u, ext - 1, 0) * w[2:3, :]
        return full[HALO:HALO + tq]

    def down(c, u_scr):
        a = conv3(u_scr[0], cw_ref[:, cols(c, 0)])
        bv = conv3(u_scr[1], cw_ref[:, cols(c, 1)])
        half = 0.5 * a
        act = ((half + half * jnp.tanh(half)) * bv).astype(BF16)
        return _dot(act, wd_ref[c * FF_CHUNK:(c + 1) * FF_CHUNK, :])

    bufs = (u0_scr, u1_scr)
    units = [(e, c) for e in range(nb) for c in range(N_FF_CHUNKS)]
    norm_stage(0)
    up(0, 0, u0_scr)
    acc = None
    for k, (e, c) in enumerate(units):
        if k + 1 < len(units):
            e_next, c_next = units[k + 1]
            if c_next == 0:
                norm_stage(e_next)
            up(e_next, c_next, bufs[(k + 1) % 2])
        d = down(c, bufs[k % 2])
        acc = d if c == 0 else acc + d
        if c == N_FF_CHUNKS - 1:
            out = xc_ref[e] + mod(e)[5:6, :] * acc
            if final:
                ms = jnp.mean(out * out, axis=-1, keepdims=True)
                out = out * lax.rsqrt(ms + NORM_EPS) * fn_ref[...]
            out_ref[e] = out


def _ffn(xs, prm, layer, mod_row, *, tq, final):
    b, n, _ = xs.shape
    nb = FFN_BATCH
    return pl.pallas_call(
        functools.partial(_ffn_kernel, tq=tq, final=final, nb=nb),
        out_shape=jax.ShapeDtypeStruct((b, n, D_MODEL), F32),
        grid=(n // tq, b // nb),
        in_specs=_halo_specs(n, tq, D_MODEL, nb) + [
            _mod_spec_multi(layer, mod_row, nb),
            _layer_spec((1, D_MODEL), layer),
            _pair_spec(prm["w_up"], (D_MODEL, 2 * D_FF)),
            _layer_spec((3, 2 * D_FF), layer),
            _pair_spec(prm["w_down"], (D_FF, D_MODEL)),
            _fixed_spec((1, D_MODEL), (0, 0))],
        out_specs=pl.BlockSpec((nb, tq, D_MODEL), lambda i, bb: (bb, i, 0)),
        scratch_shapes=[pltpu.VMEM((nb, tq + 2 * HALO, D_MODEL), F32),
                        pltpu.VMEM((nb, tq + 2 * HALO, D_MODEL), BF16),
                        pltpu.VMEM((2, tq + 2 * HALO, FF_CHUNK), F32),
                        pltpu.VMEM((2, tq + 2 * HALO, FF_CHUNK), F32)],
        compiler_params=_params(),
        name="ffn",
    )(xs, xs, xs, prm["mod"], prm["norm2"], prm["w_up"][0], prm["w_conv_ffn"], prm["w_down"][0],
      prm["final_norm"])


def _rope_tables(n):
    d = np.arange(LANES)
    hd = d % HEAD_DIM
    axis = hd // (2 * AXIS_PAIRS)
    half = (hd % (2 * AXIS_PAIRS)) // AXIS_PAIRS
    pair = hd % AXIS_PAIRS
    t = np.arange(n)
    pos = np.where(axis[None, :] == 0, (t // GRID_W)[:, None], (t % GRID_W)[:, None]).astype(np.float64)
    inv = ROPE_THETA ** (-np.arange(AXIS_PAIRS, dtype=np.float64) / AXIS_PAIRS)
    ang = pos * inv[pair][None, :]
    sin = np.sin(ang)
    return (jnp.asarray(np.cos(ang), dtype=F32),
            jnp.asarray(np.where(half[None, :] == 0, -sin, sin), dtype=F32))


def _dft_tables(n):
    k = np.arange(n, dtype=np.int64)
    ang = ((k[:, None] * k[None, :]) % n).astype(np.float64) * (2.0 * math.pi / n)
    return jnp.asarray(np.cos(ang), dtype=F32), jnp.asarray(-np.sin(ang), dtype=F32)


def _channel_dft():
    c = np.arange(FOURIER_WIDTH, dtype=np.int64)
    same = (c[:, None] // FOURIER_GROUP) == (c[None, :] // FOURIER_GROUP)
    mc = ((c[:, None] % FOURIER_GROUP) * (c[None, :] % FOURIER_GROUP)) % FOURIER_GROUP
    ang = mc.astype(np.float64) * (2.0 * math.pi / FOURIER_GROUP)
    both = np.concatenate([np.where(same, np.cos(ang), 0.0), np.where(same, np.sin(ang), 0.0)], axis=1)
    return jnp.asarray(both, dtype=F32)


def _head_matrices():
    r = np.arange(QK_WIDTH) // HEAD_DIM
    m = (r[:, None] == np.arange(LANES)[None, :]).astype(np.float32)
    return jnp.asarray(m.astype(BF16)), jnp.asarray(m.T.copy().astype(BF16))


def _block_diag(mats):
    nl, g, a, b = mats.shape
    eye = jnp.eye(g, dtype=mats.dtype)
    return (mats[:, :, :, None, :] * eye[None, :, None, :, None]).reshape(nl, g * a, g * b)


def kernel(x, c, ctx, c_ctx, w_mod, b_mod, norm1, norm2, w_in, conv_sc, qk_gain, pool_mat, pool_scale,
           w_br_attn, w_br_sc, w_br_f, w_br_p, w_out, w_up, w_conv_ffn, w_down, final_norm):
    bsz, n_lat, _ = x.shape
    n_ctx = ctx.shape[1]
    nl = w_in.shape[0]
    tq_lat = min(n_lat, TOKEN_TILE)
    tq_ctx = min(n_ctx, TOKEN_TILE)
    tq_attn = min(n_lat, ATTN_TILE)
    tq_ffn_lat = min(n_lat, FFN_TILE)
    tq_ffn_ctx = min(n_ctx, FFN_TILE)

    mod_rows = 16
    ctx_row = bsz
    cvec = jnp.concatenate([c, c_ctx[None], jnp.zeros((mod_rows - bsz - 1, D_MODEL), F32)], axis=0)
    gain = jnp.concatenate([jnp.tile(qk_gain[:, 0], (1, N_HEADS)) * (HEAD_DIM ** -0.5),
                            jnp.tile(qk_gain[:, 1], (1, N_KV_HEADS))], axis=1)
    head_sum, head_bcast = _head_matrices()
    prm = {
        "mod": _modulation(cvec, w_mod, b_mod),
        "norm1": norm1.reshape(nl, 1, D_MODEL),
        "norm2": norm2.reshape(nl, 1, D_MODEL),
        "w_in": (w_in[:1].astype(BF16), 0),
        "head_sum": head_sum,
        "head_bcast": head_bcast,
        "gain": gain.reshape(nl, 1, QK_WIDTH),
        "conv_sc": conv_sc,
        "pool_mat": _block_diag(pool_mat).astype(BF16),
        "pool_scale": pool_scale.reshape(nl, 1, POOL_WIDTH),
        "w_br_attn": w_br_attn.astype(BF16),
        "w_br_sc": w_br_sc.astype(BF16),
        "w_br_f": w_br_f.astype(BF16),
        "w_br_p": w_br_p.astype(BF16),
        "w_out": w_out.astype(BF16),
        "w_conv_ffn": w_conv_ffn,
        "final_norm": final_norm.reshape(1, D_MODEL),
    }
    cos_t, sin_t = _rope_tables(n_lat)
    wc = _channel_dft()
    dft_lat = _dft_tables(n_lat)
    dft_ctx = _dft_tables(n_ctx)

    lat, cx = x, ctx
    for l in range(nl):
        last = l == nl - 1
        if last:
            kc, vc = _kv_proj(cx, prm, l, ctx_row, tq=tq_ctx)
        else:
            qc, kc, vc, scc, xfc, ypc = _in_proj(cx, prm, l, ctx_row, cos_t, sin_t, tq=tq_ctx, rope=False)
        ql, kl, vl, scl, xfl, ypl = _in_proj(lat, prm, l, None, cos_t, sin_t, tq=tq_lat, rope=True)
        oa = _attention(ql, [(kl, vl), (kc, vc)], tq=tq_attn)
        xfour = _fourier(xfl, wc, *dft_lat)
        lat, w_up_bf, w_down_bf, *next_w_in = _mix(
            lat, prm, l, None, oa, scl, xfour, ypl, tq=tq_lat,
            casts=[(w_up, l), (w_down, l)] + ([] if last else [(w_in, l + 1)]))
        prm["w_up"], prm["w_down"] = (w_up_bf, 0), (w_down_bf, 0)
        lat = _ffn(lat, prm, l, None, tq=tq_ffn_lat, final=last)

        if not last:
            oac = _attention(qc, [(kc, vc)], tq=tq_ctx)
            xfourc = _fourier(xfc, wc, *dft_ctx)
            cx, = _mix(cx, prm, l, ctx_row, oac, scc, xfourc, ypc, tq=tq_ctx)
            cx = _ffn(cx, prm, l, ctx_row, tq=tq_ffn_ctx, final=False)
            prm["w_in"] = (next_w_in[0], 0)
    return lat
```

```python
import functools
import math

import numpy as np
import jax
import jax.numpy as jnp
from jax import lax
from jax.experimental import pallas as pl
from jax.experimental.pallas import tpu as pltpu

D_MODEL = 1024
DEPTH = 2
GRID_W = 64
NORM_EPS = 1e-6
N_HEADS = 8
N_KV_HEADS = 2
HEAD_DIM = 64
Q_WIDTH = N_HEADS * HEAD_DIM
KV_WIDTH = N_KV_HEADS * HEAD_DIM
QK_WIDTH = Q_WIDTH + KV_WIDTH
ROPE_THETA = 10000.0
AXIS_PAIRS = HEAD_DIM // 4
SC_WIDTH = 256
FOURIER_GROUPS = 4
FOURIER_WIDTH = 256
FOURIER_GROUP = FOURIER_WIDTH // FOURIER_GROUPS
POOL_WINDOWS = (2, 4, 8, 16)
POOL_WIDTH = 256
POOL_GROUP = POOL_WIDTH // len(POOL_WINDOWS)
N_BRANCHES = 4
D_FF = 2816

OFF_K = Q_WIDTH
OFF_V = OFF_K + KV_WIDTH
OFF_SC = OFF_V + KV_WIDTH
OFF_F = OFF_SC + 3 * SC_WIDTH
OFF_P = OFF_F + FOURIER_WIDTH
OFF_G = OFF_P + POOL_WIDTH
IN_WIDTH = OFF_G + N_BRANCHES * D_MODEL

LANES = 128
HALO = 8
FF_CHUNK = 256
N_FF_CHUNKS = D_FF // FF_CHUNK
ATTN_SUB = 256
TOKEN_TILE = 512
IN_PROJ_BATCH = 2
MOD_COLS = 2048
MIX_BATCH = 2
FFN_BATCH = 4
ATTN_TILE = 1024
FFN_TILE = 256
N_MOD = 6
VMEM_LIMIT = 62 * 1024 * 1024

F32 = jnp.float32
BF16 = jnp.bfloat16


def _dot(a, b):
    return jnp.dot(a, b, preferred_element_type=F32)


def _dot_nt(a, b):
    return lax.dot_general(a, b, (((1,), (1,)), ((), ())), preferred_element_type=F32)


def _fixed_spec(block, index):
    return pl.BlockSpec(block, lambda *_: index, pipeline_mode=pl.Buffered(1))


def _layer_spec(tail, layer, tail_index=None):
    tail_index = (0,) * len(tail) if tail_index is None else tail_index
    return _fixed_spec((None,) + tuple(tail), (layer,) + tuple(tail_index))


def _pair_spec(pair, tail, tail_index=None):
    return _layer_spec(tail, pair[1], tail_index)


def _cast_plumbing(casts, steps, flat_index):
    in_specs, out_specs, out_shapes = [], [], []
    for arr, layer in casts:
        _, r, c = arr.shape
        rows = r // steps
        assert rows * steps == r and rows % 16 == 0, (arr.shape, steps)
        in_specs.append(pl.BlockSpec((None, rows, c), lambda i, b, layer=layer: (layer, flat_index(i, b), 0)))
        out_specs.append(pl.BlockSpec((None, rows, c), lambda i, b: (0, flat_index(i, b), 0)))
        out_shapes.append(jax.ShapeDtypeStruct((1, r, c), BF16))
    return in_specs, out_specs, out_shapes


def _run_casts(cast_in, cast_out):
    for src, dst in zip(cast_in, cast_out):
        dst[...] = src[...].astype(BF16)


def _mod_spec(layer, row):
    if row is None:
        return pl.BlockSpec((None, None, N_MOD, D_MODEL), lambda i, b: (layer, b, 0, 0))
    return _fixed_spec((None, None, N_MOD, D_MODEL), (layer, row, 0, 0))


def _mod_spec_multi(layer, row, nb):
    if row is None:
        return pl.BlockSpec((None, nb, N_MOD, D_MODEL), lambda i, b: (layer, b, 0, 0))
    return _fixed_spec((None, 1, N_MOD, D_MODEL), (layer, row, 0, 0))


def _params():
    return pltpu.CompilerParams(dimension_semantics=("arbitrary", "arbitrary"),
                                vmem_limit_bytes=VMEM_LIMIT)


def _rms_mod(xv, norm, shift, scale):
    ms = jnp.mean(xv * xv, axis=-1, keepdims=True)
    y = xv * lax.rsqrt(ms + NORM_EPS) * norm
    return y * (1.0 + scale) + shift


def _halo_specs(n, tq, width, nb=1):
    nblk = n // HALO
    per = tq // HALO
    prev = pl.BlockSpec((nb, HALO, width), lambda i, b: (b, jnp.maximum(i * per - 1, 0), 0))
    cur = pl.BlockSpec((nb, tq, width), lambda i, b: (b, i, 0))
    nxt = pl.BlockSpec((nb, HALO, width), lambda i, b: (b, jnp.minimum((i + 1) * per, nblk - 1), 0))
    return [prev, cur, nxt]


def _fill_halo_rows(h_scr, xp_ref, xc_ref, xn_ref, fn, tq):
    i = pl.program_id(0)
    last = pl.num_programs(0) - 1
    h_scr[HALO:HALO + tq, :] = fn(xc_ref[0])
    h_scr[0:HALO, :] = jnp.where(i > 0, fn(xp_ref[0]), 0.0)
    h_scr[HALO + tq:2 * HALO + tq, :] = jnp.where(i < last, fn(xn_ref[0]), 0.0)


def _split_bf16(a):
    hi = a.astype(BF16)
    return hi, (a - hi.astype(F32)).astype(BF16)


def _head_rms(z, sum_ref, bcast_ref, gain):
    sq_hi, sq_lo = _split_bf16(z * z)
    ss = _dot(sq_hi, sum_ref[...]) + _dot(sq_lo, sum_ref[...])
    r_hi, r_lo = _split_bf16(lax.rsqrt(ss * (1.0 / HEAD_DIM) + NORM_EPS))
    return z * (_dot(r_hi, bcast_ref[...]) + _dot(r_lo, bcast_ref[...])) * gain


def _dup_heads(a, low_half):
    sw = pltpu.roll(a, HEAD_DIM, 1)
    return jnp.where(low_half, a, sw), jnp.where(low_half, sw, a)


def _mod_kernel(c_ref, w_ref, b_ref, o_ref):
    cv = c_ref[...]
    a = (cv * jax.nn.sigmoid(cv)).astype(BF16)
    o_ref[0] = _dot(a, w_ref[0].astype(BF16)) + b_ref[0]


def _modulation(cvec, w_mod, b_mod):
    rows = cvec.shape[0]
    nl = w_mod.shape[0]
    out = pl.pallas_call(
        _mod_kernel,
        out_shape=jax.ShapeDtypeStruct((nl, rows, N_MOD * D_MODEL), F32),
        grid=(nl, N_MOD * D_MODEL // MOD_COLS),
        in_specs=[pl.BlockSpec((rows, D_MODEL), lambda l, j: (0, 0)),
                  pl.BlockSpec((1, D_MODEL, MOD_COLS), lambda l, j: (l, 0, j)),
                  pl.BlockSpec((1, 1, MOD_COLS), lambda l, j: (l, 0, j))],
        out_specs=pl.BlockSpec((1, rows, MOD_COLS), lambda l, j: (l, 0, j)),
        compiler_params=_params(),
        name="adaln_mod",
    )(cvec, w_mod, b_mod.reshape(nl, 1, N_MOD * D_MODEL))
    return out.reshape(nl, rows, N_MOD, D_MODEL)


def _in_proj_kernel(xp_ref, xc_ref, xn_ref, m_ref, norm_ref, w_ref, hs_ref, hb_ref, gain_ref, cos_ref, sin_ref,
                    conv_ref, pm_ref, ps_ref,
                    q_ref, k_ref, v_ref, sc_ref, xf_ref, yp_ref,
                    h_scr, z_scr, *, n, tq, rope, nb):
    i = pl.program_id(0)
    norm = norm_ref[...]

    def norm_stage(e):
        m = m_ref.at[min(e, m_ref.shape[0] - 1)]
        shift, scale = m[0:1, :], m[1:2, :]
        _fill_halo_rows(h_scr.at[e % 2], xp_ref.at[e:e + 1], xc_ref.at[e:e + 1], xn_ref.at[e:e + 1],
                        lambda xv: _rms_mod(xv, norm, shift, scale), tq)

    norm_stage(0)
    for e in range(nb):
        z_scr[e % 2] = _dot(h_scr[e % 2].astype(BF16), w_ref[...])
        if e + 1 < nb:
            norm_stage(e + 1)
        _local_mixers(e, z_scr.at[e % 2], hs_ref, hb_ref, gain_ref, cos_ref, sin_ref, conv_ref, pm_ref, ps_ref,
                      q_ref, k_ref, v_ref, sc_ref, xf_ref, yp_ref, i=i, n=n, tq=tq, rope=rope)


def _local_mixers(e, z_scr, hs_ref, hb_ref, gain_ref, cos_ref, sin_ref, conv_ref, pm_ref, ps_ref,
                  q_ref, k_ref, v_ref, sc_ref, xf_ref, yp_ref, *, i, n, tq, rope):
    lane = lax.broadcasted_iota(jnp.int32, (tq, LANES), 1)
    low_half = lane < HEAD_DIM
    rows = slice(HALO, HALO + tq)

    y = _head_rms(z_scr[rows, 0:QK_WIDTH], hs_ref, hb_ref, gain_ref[...])
    second_half = (lane & AXIS_PAIRS) != 0
    for c in range(QK_WIDTH // LANES):
        yc = y[:, c * LANES:(c + 1) * LANES]
        if rope:
            partner = jnp.where(second_half, pltpu.roll(yc, AXIS_PAIRS, 1),
                                pltpu.roll(yc, LANES - AXIS_PAIRS, 1))
            yc = yc * cos_ref[...] + partner * sin_ref[...]
        if c < Q_WIDTH // LANES:
            q_ref[e, :, c * LANES:(c + 1) * LANES] = yc.astype(BF16)
        else:
            k0, k1 = _dup_heads(yc, low_half)
            k_ref[e, :, 0:LANES] = k0.astype(BF16)
            k_ref[e, :, LANES:2 * LANES] = k1.astype(BF16)
    v0, v1 = _dup_heads(z_scr[rows, OFF_V:OFF_SC], low_half)
    v_ref[e, :, 0:LANES] = v0.astype(BF16)
    v_ref[e, :, LANES:2 * LANES] = v1.astype(BF16)

    def u_at(off):
        r = slice(HALO + off, HALO + off + tq)
        return z_scr[r, OFF_SC + SC_WIDTH:OFF_SC + 2 * SC_WIDTH] * z_scr[r, OFF_SC + 2 * SC_WIDTH:OFF_F]

    conv = u_at(-1) * conv_ref[0:1, :] + u_at(0) * conv_ref[1:2, :] + u_at(1) * conv_ref[2:3, :]
    sc_ref[e] = (z_scr[rows, OFF_SC:OFF_SC + SC_WIDTH] * conv).astype(BF16)

    xf_ref[e] = z_scr[rows, OFF_F:OFF_P].astype(BF16)

    t = i * tq + lax.broadcasted_iota(jnp.int32, (tq, LANES), 0)

    def p_at(off, c):
        return z_scr[HALO + off:HALO + off + tq, OFF_P + c * LANES:OFF_P + (c + 1) * LANES]

    def window_mean(total, w_low, w_high):
        w = jnp.where(low_half, w_low, w_high)
        left, right = (w - 1) // 2, w // 2
        cnt = jnp.minimum(t + right + 1, n) - jnp.maximum(t - left, 0)
        return total / cnt.astype(F32)

    p0 = p_at(0, 0)
    s2 = p0 + p_at(1, 0)
    s4 = s2 + p_at(-1, 0) + p_at(2, 0)
    pooled0 = window_mean(jnp.where(low_half, s2, s4), POOL_WINDOWS[0], POOL_WINDOWS[1]) - p0
    p1 = p_at(0, 1)
    s8 = p1
    for off in (-3, -2, -1, 1, 2, 3, 4):
        s8 = s8 + p_at(off, 1)
    s16 = s8
    for off in (-7, -6, -5, -4, 5, 6, 7, 8):
        s16 = s16 + p_at(off, 1)
    pooled1 = window_mean(jnp.where(low_half, s8, s16), POOL_WINDOWS[2], POOL_WINDOWS[3]) - p1
    pooled = jnp.concatenate([pooled0, pooled1], axis=1).astype(BF16)
    yp_ref[e] = (_dot(pooled, pm_ref[...]) * ps_ref[...]).astype(BF16)


def _in_proj(xs, prm, layer, mod_row, cos_t, sin_t, *, tq, rope):
    b, n, _ = xs.shape
    out_w = (Q_WIDTH, 2 * LANES, 2 * LANES, SC_WIDTH, FOURIER_WIDTH, POOL_WIDTH)
    nb = IN_PROJ_BATCH
    kern = functools.partial(_in_proj_kernel, n=n, tq=tq, rope=rope, nb=nb)
    return pl.pallas_call(
        kern,
        out_shape=[jax.ShapeDtypeStruct((b, n, w), BF16) for w in out_w],
        grid=(n // tq, b // nb),
        in_specs=_halo_specs(n, tq, D_MODEL, nb) + [
            _mod_spec_multi(layer, mod_row, nb),
            _layer_spec((1, D_MODEL), layer),
            _pair_spec(prm["w_in"], (D_MODEL, OFF_G)),
            _fixed_spec((QK_WIDTH, LANES), (0, 0)),
            _fixed_spec((LANES, QK_WIDTH), (0, 0)),
            _layer_spec((1, QK_WIDTH), layer),
            pl.BlockSpec((tq, LANES), lambda i, bb: (i, 0)),
            pl.BlockSpec((tq, LANES), lambda i, bb: (i, 0)),
            _layer_spec((3, SC_WIDTH), layer),
            _layer_spec((POOL_WIDTH, POOL_WIDTH), layer),
            _layer_spec((1, POOL_WIDTH), layer),
        ],
        out_specs=[pl.BlockSpec((nb, tq, w), lambda i, bb: (bb, i, 0)) for w in out_w],
        scratch_shapes=[pltpu.VMEM((2, tq + 2 * HALO, D_MODEL), F32),
                        pltpu.VMEM((2, tq + 2 * HALO, OFF_G), F32)],
        compiler_params=_params(),
        name="in_proj",
    )(xs, xs, xs, prm["mod"], prm["norm1"], prm["w_in"][0], prm["head_sum"], prm["head_bcast"], prm["gain"],
      cos_t, sin_t,
      prm["conv_sc"], prm["pool_mat"], prm["pool_scale"])


def _kv_proj_kernel(x_ref, m_ref, norm_ref, w_ref, hs_ref, hb_ref, gain_ref, k_ref, v_ref, *, tq):
    h = _rms_mod(x_ref[0], norm_ref[...], m_ref[0:1, :], m_ref[1:2, :]).astype(BF16)
    z = _dot(h, w_ref[...])
    low_half = lax.broadcasted_iota(jnp.int32, (tq, LANES), 1) < HEAD_DIM
    k0, k1 = _dup_heads(_head_rms(z[:, 0:KV_WIDTH], hs_ref, hb_ref, gain_ref[...]), low_half)
    k_ref[0, :, 0:LANES] = k0.astype(BF16)
    k_ref[0, :, LANES:2 * LANES] = k1.astype(BF16)
    v0, v1 = _dup_heads(z[:, KV_WIDTH:2 * KV_WIDTH], low_half)
    v_ref[0, :, 0:LANES] = v0.astype(BF16)
    v_ref[0, :, LANES:2 * LANES] = v1.astype(BF16)


def _kv_proj(xs, prm, layer, mod_row, *, tq):
    b, n, _ = xs.shape
    kv_cols = 2 * KV_WIDTH
    return pl.pallas_call(
        functools.partial(_kv_proj_kernel, tq=tq),
        out_shape=[jax.ShapeDtypeStruct((b, n, 2 * LANES), BF16)] * 2,
        grid=(n // tq, b),
        in_specs=[pl.BlockSpec((1, tq, D_MODEL), lambda i, bb: (bb, i, 0)),
                  _mod_spec(layer, mod_row),
                  _layer_spec((1, D_MODEL), layer),
                  _pair_spec(prm["w_in"], (D_MODEL, kv_cols), (0, OFF_K // kv_cols)),
                  _fixed_spec((KV_WIDTH, LANES), (0, 0)),
                  _fixed_spec((LANES, KV_WIDTH), (0, 0)),
                  _layer_spec((1, KV_WIDTH), layer, (0, Q_WIDTH // KV_WIDTH))],
        out_specs=[pl.BlockSpec((1, tq, 2 * LANES), lambda i, bb: (bb, i, 0))] * 2,
        compiler_params=_params(),
        name="kv_proj",
    )(xs, prm["mod"], prm["norm1"], prm["w_in"][0], prm["head_sum"], prm["head_bcast"], prm["gain"])


def _attn_kernel(q_ref, *refs, tq):
    *kv_refs, o_ref = refs
    sub = min(tq, ATTN_SUB)
    lane = lax.broadcasted_iota(jnp.int32, (sub, LANES), 1)
    low_half = lane < HEAD_DIM
    sources = [(kv_refs[2 * a], kv_refs[2 * a + 1]) for a in range(len(kv_refs) // 2)]
    for j in range(Q_WIDTH // LANES):
        g = (2 * j) // (N_HEADS // N_KV_HEADS)
        gl = slice(g * LANES, (g + 1) * LANES)
        for r in range(tq // sub):
            rows = slice(r * sub, (r + 1) * sub)
            qp = q_ref[0, rows, j * LANES:(j + 1) * LANES]
            zero = jnp.zeros_like(qp)
            q2 = jnp.concatenate([jnp.where(low_half, qp, zero), jnp.where(low_half, zero, qp)], axis=0)
            scores = [_dot_nt(q2, k_ref[0, :, gl]) for k_ref, _ in sources]
            row_max = functools.reduce(jnp.maximum, [jnp.max(s, axis=-1, keepdims=True) for s in scores])
            o, denom = None, None
            for s, (_, v_ref) in zip(scores, sources):
                p = jnp.exp(s - row_max)
                part = jnp.sum(p, axis=-1, keepdims=True)
                term = _dot(p.astype(BF16), v_ref[0, :, gl])
                o, denom = (term, part) if o is None else (o + term, denom + part)
            o = o / denom
            o_ref[0, rows, j * LANES:(j + 1) * LANES] = jnp.where(low_half, o[:sub], o[sub:]).astype(BF16)


def _attention(q, kv_sources, *, tq):
    b, n, _ = q.shape
    kv_specs, kv_args = [], []
    for k, v in kv_sources:
        for a in (k, v):
            kv_specs.append(pl.BlockSpec((1, a.shape[1], 2 * LANES), lambda bb, i: (bb, 0, 0)))
            kv_args.append(a)
    return pl.pallas_call(
        functools.partial(_attn_kernel, tq=tq),
        out_shape=jax.ShapeDtypeStruct((b, n, Q_WIDTH), BF16),
        grid=(b, n // tq),
        in_specs=[pl.BlockSpec((1, tq, Q_WIDTH), lambda bb, i: (bb, i, 0))] + kv_specs,
        out_specs=pl.BlockSpec((1, tq, Q_WIDTH), lambda bb, i: (bb, i, 0)),
        compiler_params=_params(),
        name="attention",
    )(q, *kv_args)


def _fourier_kernel(x_ref, wc_ref, cn_ref, nsn_ref, o_ref, t_scr, *, n, tr):
    t_scr[...] = _dot(x_ref[0], wc_ref[...].astype(BF16)).astype(BF16)
    scale = 1.0 / math.sqrt(n * FOURIER_GROUP)
    for r in range(n // tr):
        rows = slice(r * tr, (r + 1) * tr)
        acc = (_dot(cn_ref[rows, :].astype(BF16), t_scr[:, 0:FOURIER_WIDTH])
               + _dot(nsn_ref[rows, :].astype(BF16), t_scr[:, FOURIER_WIDTH:2 * FOURIER_WIDTH]))
        o_ref[0, rows, :] = (acc * scale).astype(BF16)


def _fourier(xf, wc, cn, nsn):
    b, n, _ = xf.shape
    tr = min(n, 512)
    return pl.pallas_call(
        functools.partial(_fourier_kernel, n=n, tr=tr),
        out_shape=jax.ShapeDtypeStruct((b, n, FOURIER_WIDTH), BF16),
        grid=(b, 1),
        in_specs=[pl.BlockSpec((1, n, FOURIER_WIDTH), lambda bb, i: (bb, 0, 0)),
                  _fixed_spec((FOURIER_WIDTH, 2 * FOURIER_WIDTH), (0, 0)),
                  _fixed_spec((n, n), (0, 0)),
                  _fixed_spec((n, n), (0, 0))],
        out_specs=pl.BlockSpec((1, n, FOURIER_WIDTH), lambda bb, i: (bb, 0, 0)),
        scratch_shapes=[pltpu.VMEM((n, 2 * FOURIER_WIDTH), BF16)],
        compiler_params=_params(),
        name="fourier",
    )(xf, wc, cn, nsn)


N_MIX_INPUTS = 14


def _mix_kernel(*refs, nb, n_cast):
    (x_ref, m_ref, norm_ref, oa_ref, sc_ref, xf_ref, yp_ref,
     wg01_ref, wg23_ref, wa_ref, wsc_ref, wf_ref, wp_ref, wo_ref) = refs[:N_MIX_INPUTS]
    cast_in = refs[N_MIX_INPUTS:N_MIX_INPUTS + n_cast]
    out_ref = refs[N_MIX_INPUTS + n_cast]
    cast_out = refs[N_MIX_INPUTS + n_cast + 1:N_MIX_INPUTS + 2 * n_cast + 1]
    h_scr, y_scr = refs[-2:]
    _run_casts(cast_in, cast_out)
    branches = ((oa_ref, wa_ref), (sc_ref, wsc_ref), (xf_ref, wf_ref), (yp_ref, wp_ref))
    gate_w = (wg01_ref, wg23_ref)
    norm = norm_ref[...]

    def mod(e):
        return m_ref.at[min(e, m_ref.shape[0] - 1)]

    def norm_stage(e):
        m = mod(e)
        h_scr[e] = _rms_mod(x_ref[e], norm, m[0:1, :], m[1:2, :]).astype(BF16)

    norm_stage(0)
    for e in range(nb):
        h = h_scr[e]
        for j in range(D_MODEL // FF_CHUNK):
            cols = slice(j * FF_CHUNK, (j + 1) * FF_CHUNK)
            y = None
            for bi, (a_ref, w_ref) in enumerate(branches):
                off = (bi % 2) * D_MODEL + j * FF_CHUNK
                gate = 0.5 + 0.5 * jnp.tanh(0.5 * _dot(h, gate_w[bi // 2][:, off:off + FF_CHUNK]))
                term = gate * _dot(a_ref[e], w_ref[:, cols])
                y = term if y is None else y + term
            y_scr[e, :, cols] = y.astype(BF16)
            if j == 0 and e + 1 < nb:
                norm_stage(e + 1)
        out_ref[e] = x_ref[e] + mod(e)[2:3, :] * _dot(y_scr[e], wo_ref[...])


def _mix(xs, prm, layer, mod_row, oa, sc, xf, yp, *, tq, casts=()):
    b, n, _ = xs.shape
    nb = MIX_BATCH
    half_gates = N_BRANCHES * D_MODEL // 2
    grid = (n // tq, b // nb)
    cast_in, cast_out, cast_shapes = _cast_plumbing(casts, grid[0] * grid[1], lambda i, bb: i * grid[1] + bb)

    def tok(w):
        return pl.BlockSpec((nb, tq, w), lambda i, bb: (bb, i, 0))

    in_specs = [tok(D_MODEL), _mod_spec_multi(layer, mod_row, nb), _layer_spec((1, D_MODEL), layer),
                tok(Q_WIDTH), tok(SC_WIDTH), tok(FOURIER_WIDTH), tok(POOL_WIDTH),
                _pair_spec(prm["w_in"], (D_MODEL, half_gates), (0, OFF_G // half_gates)),
                _pair_spec(prm["w_in"], (D_MODEL, half_gates), (0, OFF_G // half_gates + 1)),
                _layer_spec((Q_WIDTH, D_MODEL), layer), _layer_spec((SC_WIDTH, D_MODEL), layer),
                _layer_spec((FOURIER_WIDTH, D_MODEL), layer), _layer_spec((POOL_WIDTH, D_MODEL), layer),
                _layer_spec((D_MODEL, D_MODEL), layer)]
    assert len(in_specs) == N_MIX_INPUTS
    return pl.pallas_call(
        functools.partial(_mix_kernel, nb=nb, n_cast=len(casts)),
        out_shape=[jax.ShapeDtypeStruct((b, n, D_MODEL), F32)] + cast_shapes,
        grid=grid,
        in_specs=in_specs + cast_in,
        out_specs=[tok(D_MODEL)] + cast_out,
        scratch_shapes=[pltpu.VMEM((nb, tq, D_MODEL), BF16), pltpu.VMEM((nb, tq, D_MODEL), BF16)],
        compiler_params=_params(),
        name="mix",
    )(xs, prm["mod"], prm["norm1"], oa, sc, xf, yp, prm["w_in"][0], prm["w_in"][0],
      prm["w_br_attn"], prm["w_br_sc"], prm["w_br_f"], prm["w_br_p"], prm["w_out"], *[a for a, _ in casts])


def _ffn_kernel(xp_ref, xc_ref, xn_ref, m_ref, norm_ref, wu_ref, cw_ref, wd_ref, fn_ref, out_ref,
                h_scr, hb_scr, u0_scr, u1_scr, *, tq, final, nb):
    norm = norm_ref[...]
    ext = tq + 2 * HALO

    def mod(e):
        return m_ref.at[min(e, m_ref.shape[0] - 1)]

    def cols(c, half):
        return slice(half * D_FF + c * FF_CHUNK, half * D_FF + (c + 1) * FF_CHUNK)

    def norm_stage(e):
        m = mod(e)
        shift, scale = m[3:4, :], m[4:5, :]
        _fill_halo_rows(h_scr.at[e], xp_ref.at[e:e + 1], xc_ref.at[e:e + 1], xn_ref.at[e:e + 1],
                        lambda xv: _rms_mod(xv, norm, shift, scale), tq)
        hb_scr[e] = h_scr[e].astype(BF16)

    def up(e, c, u_scr):
        hb = hb_scr[e]
        u_scr[0] = _dot(hb, wu_ref[:, cols(c, 0)])
        u_scr[1] = _dot(hb, wu_ref[:, cols(c, 1)])

    def conv3(u, w):
        full = pltpu.roll(u, 1, 0) * w[0:1, :] + u * w[1:2, :] + pltpu.roll(u, ext - 1, 0) * w[2:3, :]
        return full[HALO:HALO + tq]

    def down(c, u_scr):
        a = conv3(u_scr[0], cw_ref[:, cols(c, 0)])
        bv = conv3(u_scr[1], cw_ref[:, cols(c, 1)])
        half = 0.5 * a
        act = ((half + half * jnp.tanh(half)) * bv).astype(BF16)
        return _dot(act, wd_ref[c * FF_CHUNK:(c + 1) * FF_CHUNK, :])

    bufs = (u0_scr, u1_scr)
    units = [(e, c) for e in range(nb) for c in range(N_FF_CHUNKS)]
    norm_stage(0)
    up(0, 0, u0_scr)
    acc = None
    for k, (e, c) in enumerate(units):
        if k + 1 < len(units):
            e_next, c_next = units[k + 1]
            if c_next == 0:
                norm_stage(e_next)
            up(e_next, c_next, bufs[(k + 1) % 2])
        d = down(c, bufs[k % 2])
        acc = d if c == 0 else acc + d
        if c == N_FF_CHUNKS - 1:
            out = xc_ref[e] + mod(e)[5:6, :] * acc
            if final:
                ms = jnp.mean(out * out, axis=-1, keepdims=True)
                out = out * lax.rsqrt(ms + NORM_EPS) * fn_ref[...]
            out_ref[e] = out


def _ffn(xs, prm, layer, mod_row, *, tq, final):
    b, n, _ = xs.shape
    nb = FFN_BATCH
    return pl.pallas_call(
        functools.partial(_ffn_kernel, tq=tq, final=final, nb=nb),
        out_shape=jax.ShapeDtypeStruct((b, n, D_MODEL), F32),
        grid=(n // tq, b // nb),
        in_specs=_halo_specs(n, tq, D_MODEL, nb) + [
            _mod_spec_multi(layer, mod_row, nb),
            _layer_spec((1, D_MODEL), layer),
            _pair_spec(prm["w_up"], (D_MODEL, 2 * D_FF)),
            _layer_spec((3, 2 * D_FF), layer),
            _pair_spec(prm["w_down"], (D_FF, D_MODEL)),
            _fixed_spec((1, D_MODEL), (0, 0))],
        out_specs=pl.BlockSpec((nb, tq, D_MODEL), lambda i, bb: (bb, i, 0)),
        scratch_shapes=[pltpu.VMEM((nb, tq + 2 * HALO, D_MODEL), F32),
                        pltpu.VMEM((nb, tq + 2 * HALO, D_MODEL), BF16),
                        pltpu.VMEM((2, tq + 2 * HALO, FF_CHUNK), F32),
                        pltpu.VMEM((2, tq + 2 * HALO, FF_CHUNK), F32)],
        compiler_params=_params(),
        name="ffn",
    )(xs, xs, xs, prm["mod"], prm["norm2"], prm["w_up"][0], prm["w_conv_ffn"], prm["w_down"][0],
      prm["final_norm"])


def _rope_tables(n):
    d = np.arange(LANES)
    hd = d % HEAD_DIM
    axis = hd // (2 * AXIS_PAIRS)
    half = (hd % (2 * AXIS_PAIRS)) // AXIS_PAIRS
    pair = hd % AXIS_PAIRS
    t = np.arange(n)
    pos = np.where(axis[None, :] == 0, (t // GRID_W)[:, None], (t % GRID_W)[:, None]).astype(np.float64)
    inv = ROPE_THETA ** (-np.arange(AXIS_PAIRS, dtype=np.float64) / AXIS_PAIRS)
    ang = pos * inv[pair][None, :]
    sin = np.sin(ang)
    return (jnp.asarray(np.cos(ang), dtype=F32),
            jnp.asarray(np.where(half[None, :] == 0, -sin, sin), dtype=F32))


def _dft_tables(n):
    k = np.arange(n, dtype=np.int64)
    ang = ((k[:, None] * k[None, :]) % n).astype(np.float64) * (2.0 * math.pi / n)
    return jnp.asarray(np.cos(ang), dtype=F32), jnp.asarray(-np.sin(ang), dtype=F32)


def _channel_dft():
    c = np.arange(FOURIER_WIDTH, dtype=np.int64)
    same = (c[:, None] // FOURIER_GROUP) == (c[None, :] // FOURIER_GROUP)
    mc = ((c[:, None] % FOURIER_GROUP) * (c[None, :] % FOURIER_GROUP)) % FOURIER_GROUP
    ang = mc.astype(np.float64) * (2.0 * math.pi / FOURIER_GROUP)
    both = np.concatenate([np.where(same, np.cos(ang), 0.0), np.where(same, np.sin(ang), 0.0)], axis=1)
    return jnp.asarray(both, dtype=F32)


def _head_matrices():
    r = np.arange(QK_WIDTH) // HEAD_DIM
    m = (r[:, None] == np.arange(LANES)[None, :]).astype(np.float32)
    return jnp.asarray(m.astype(BF16)), jnp.asarray(m.T.copy().astype(BF16))


def _block_diag(mats):
    nl, g, a, b = mats.shape
    eye = jnp.eye(g, dtype=mats.dtype)
    return (mats[:, :, :, None, :] * eye[None, :, None, :, None]).reshape(nl, g * a, g * b)


def kernel(x, c, ctx, c_ctx, w_mod, b_mod, norm1, norm2, w_in, conv_sc, qk_gain, pool_mat, pool_scale,
           w_br_attn, w_br_sc, w_br_f, w_br_p, w_out, w_up, w_conv_ffn, w_down, final_norm):
    bsz, n_lat, _ = x.shape
    n_ctx = ctx.shape[1]
    nl = w_in.shape[0]
    tq_lat = min(n_lat, TOKEN_TILE)
    tq_ctx = min(n_ctx, TOKEN_TILE)
    tq_attn = min(n_lat, ATTN_TILE)
    tq_ffn_lat = min(n_lat, FFN_TILE)
    tq_ffn_ctx = min(n_ctx, FFN_TILE)

    mod_rows = 16
    ctx_row = bsz
    cvec = jnp.concatenate([c, c_ctx[None], jnp.zeros((mod_rows - bsz - 1, D_MODEL), F32)], axis=0)
    gain = jnp.concatenate([jnp.tile(qk_gain[:, 0], (1, N_HEADS)) * (HEAD_DIM ** -0.5),
                            jnp.tile(qk_gain[:, 1], (1, N_KV_HEADS))], axis=1)
    head_sum, head_bcast = _head_matrices()
    prm = {
        "mod": _modulation(cvec, w_mod, b_mod),
        "norm1": norm1.reshape(nl, 1, D_MODEL),
        "norm2": norm2.reshape(nl, 1, D_MODEL),
        "w_in": (w_in[:1].astype(BF16), 0),
        "head_sum": head_sum,
        "head_bcast": head_bcast,
        "gain": gain.reshape(nl, 1, QK_WIDTH),
        "conv_sc": conv_sc,
        "pool_mat": _block_diag(pool_mat).astype(BF16),
        "pool_scale": pool_scale.reshape(nl, 1, POOL_WIDTH),
        "w_br_attn": w_br_attn.astype(BF16),
        "w_br_sc": w_br_sc.astype(BF16),
        "w_br_f": w_br_f.astype(BF16),
        "w_br_p": w_br_p.astype(BF16),
        "w_out": w_out.astype(BF16),
        "w_conv_ffn": w_conv_ffn,
        "final_norm": final_norm.reshape(1, D_MODEL),
    }
    cos_t, sin_t = _rope_tables(n_lat)
    wc = _channel_dft()
    dft_lat = _dft_tables(n_lat)
    dft_ctx = _dft_tables(n_ctx)

    lat, cx = x, ctx
    for l in range(nl):
        last = l == nl - 1
        if last:
            kc, vc = _kv_proj(cx, prm, l, ctx_row, tq=tq_ctx)
        else:
            qc, kc, vc, scc, xfc, ypc = _in_proj(cx, prm, l, ctx_row, cos_t, sin_t, tq=tq_ctx, rope=False)
        ql, kl, vl, scl, xfl, ypl = _in_proj(lat, prm, l, None, cos_t, sin_t, tq=tq_lat, rope=True)
        oa = _attention(ql, [(kl, vl), (kc, vc)], tq=tq_attn)
        xfour = _fourier(xfl, wc, *dft_lat)
        lat, w_up_bf, w_down_bf, *next_w_in = _mix(
            lat, prm, l, None, oa, scl, xfour, ypl, tq=tq_lat,
            casts=[(w_up, l), (w_down, l)] + ([] if last else [(w_in, l + 1)]))
        prm["w_up"], prm["w_down"] = (w_up_bf, 0), (w_down_bf, 0)
        lat = _ffn(lat, prm, l, None, tq=tq_ffn_lat, final=last)

        if not last:
            oac = _attention(qc, [(kc, vc)], tq=tq_ctx)
            xfourc = _fourier(xfc, wc, *dft_ctx)
            cx, = _mix(cx, prm, l, ctx_row, oac, scc, xfourc, ypc, tq=tq_ctx)
            cx = _ffn(cx, prm, l, ctx_row, tq=tq_ffn_ctx, final=False)
            prm["w_in"] = (next_w_in[0], 0)
    return lat
```

```python
import functools
import math

import numpy as np
import jax
import jax.numpy as jnp
from jax import lax
from jax.experimental import pallas as pl
from jax.experimental.pallas import tpu as pltpu

D_MODEL = 1024
DEPTH = 2
GRID_W = 64
NORM_EPS = 1e-6
N_HEADS = 8
N_KV_HEADS = 2
HEAD_DIM = 64
Q_WIDTH = N_HEADS * HEAD_DIM
KV_WIDTH = N_KV_HEADS * HEAD_DIM
QK_WIDTH = Q_WIDTH + KV_WIDTH
ROPE_THETA = 10000.0
AXIS_PAIRS = HEAD_DIM // 4
SC_WIDTH = 256
FOURIER_GROUPS = 4
FOURIER_WIDTH = 256
FOURIER_GROUP = FOURIER_WIDTH // FOURIER_GROUPS
POOL_WINDOWS = (2, 4, 8, 16)
POOL_WIDTH = 256
N_BRANCHES = 4
D_FF = 2816

OFF_K = Q_WIDTH
OFF_V = OFF_K + KV_WIDTH
OFF_SC = OFF_V + KV_WIDTH
OFF_F = OFF_SC + 3 * SC_WIDTH
OFF_P = OFF_F + FOURIER_WIDTH
OFF_G = OFF_P + POOL_WIDTH

LANES = 128
BF16_ROWS = 16
HALO = 8
FF_CHUNK = 256
N_FF_CHUNKS = D_FF // FF_CHUNK
ATTN_SUB = 256
TOKEN_TILE = 512
IN_PROJ_BATCH = 2
MOD_COLS = 2048
MIX_BATCH = 2
FFN_BATCH = 4
ATTN_TILE = 1024
FFN_TILE = 256
N_MOD = 6
VMEM_LIMIT = 62 * 1024 * 1024

F32 = jnp.float32
BF16 = jnp.bfloat16


def _dot(a, b):
    return jnp.dot(a, b, preferred_element_type=F32)


def _dot_nt(a, b):
    return lax.dot_general(a, b, (((1,), (1,)), ((), ())), preferred_element_type=F32)


def _fixed_spec(block, index):
    return pl.BlockSpec(block, lambda *_: index, pipeline_mode=pl.Buffered(1))


def _layer_spec(tail, layer, tail_index=None):
    tail_index = (0,) * len(tail) if tail_index is None else tail_index
    return _fixed_spec((None,) + tuple(tail), (layer,) + tuple(tail_index))


def _pair_spec(pair, tail, tail_index=None):
    return _layer_spec(tail, pair[1], tail_index)


def _cast_plumbing(casts, steps, flat_index):
    in_specs, out_specs, out_shapes = [], [], []
    for arr, layer in casts:
        _, r, c = arr.shape
        rows = r // steps
        assert rows * steps == r and rows % BF16_ROWS == 0, (arr.shape, steps)
        in_specs.append(pl.BlockSpec((None, rows, c), lambda i, b, layer=layer: (layer, flat_index(i, b), 0)))
        out_specs.append(pl.BlockSpec((None, rows, c), lambda i, b: (0, flat_index(i, b), 0)))
        out_shapes.append(jax.ShapeDtypeStruct((1, r, c), BF16))
    return in_specs, out_specs, out_shapes


def _run_casts(cast_in, cast_out):
    for src, dst in zip(cast_in, cast_out):
        dst[...] = src[...].astype(BF16)


def _mod_spec(layer, row):
    if row is None:
        return pl.BlockSpec((None, None, N_MOD, D_MODEL), lambda i, b: (layer, b, 0, 0))
    return _fixed_spec((None, None, N_MOD, D_MODEL), (layer, row, 0, 0))


def _mod_spec_multi(layer, row, nb):
    if row is None:
        return pl.BlockSpec((None, nb, N_MOD, D_MODEL), lambda i, b: (layer, b, 0, 0))
    return _fixed_spec((None, 1, N_MOD, D_MODEL), (layer, row, 0, 0))


def _params():
    return pltpu.CompilerParams(dimension_semantics=("arbitrary", "arbitrary"),
                                vmem_limit_bytes=VMEM_LIMIT)


def _rms_mod(xv, norm, shift, scale):
    ms = jnp.mean(xv * xv, axis=-1, keepdims=True)
    y = xv * lax.rsqrt(ms + NORM_EPS) * norm
    return y * (1.0 + scale) + shift


def _halo_specs(n, tq, width, nb=1):
    nblk = n // HALO
    per = tq // HALO
    prev = pl.BlockSpec((nb, HALO, width), lambda i, b: (b, jnp.maximum(i * per - 1, 0), 0))
    cur = pl.BlockSpec((nb, tq, width), lambda i, b: (b, i, 0))
    nxt = pl.BlockSpec((nb, HALO, width), lambda i, b: (b, jnp.minimum((i + 1) * per, nblk - 1), 0))
    return [prev, cur, nxt]


def _fill_halo_rows(h_scr, xp_ref, xc_ref, xn_ref, fn, tq):
    i = pl.program_id(0)
    last = pl.num_programs(0) - 1
    h_scr[HALO:HALO + tq, :] = fn(xc_ref[0])
    h_scr[0:HALO, :] = jnp.where(i > 0, fn(xp_ref[0]), 0.0)
    h_scr[HALO + tq:2 * HALO + tq, :] = jnp.where(i < last, fn(xn_ref[0]), 0.0)


def _split_bf16(a):
    hi = a.astype(BF16)
    return hi, (a - hi.astype(F32)).astype(BF16)


def _head_rms(z, sum_ref, bcast_ref, gain):
    sq_hi, sq_lo = _split_bf16(z * z)
    ss = _dot(sq_hi, sum_ref[...]) + _dot(sq_lo, sum_ref[...])
    r_hi, r_lo = _split_bf16(lax.rsqrt(ss * (1.0 / HEAD_DIM) + NORM_EPS))
    return z * (_dot(r_hi, bcast_ref[...]) + _dot(r_lo, bcast_ref[...])) * gain


def _dup_heads(a, low_half):
    sw = pltpu.roll(a, HEAD_DIM, 1)
    return jnp.where(low_half, a, sw), jnp.where(low_half, sw, a)


def _mod_kernel(c_ref, w_ref, b_ref, o_ref):
    cv = c_ref[...]
    a = (cv * jax.nn.sigmoid(cv)).astype(BF16)
    o_ref[0] = _dot(a, w_ref[0].astype(BF16)) + b_ref[0]


def _modulation(cvec, w_mod, b_mod):
    rows = cvec.shape[0]
    nl = w_mod.shape[0]
    out = pl.pallas_call(
        _mod_kernel,
        out_shape=jax.ShapeDtypeStruct((nl, rows, N_MOD * D_MODEL), F32),
        grid=(nl, N_MOD * D_MODEL // MOD_COLS),
        in_specs=[pl.BlockSpec((rows, D_MODEL), lambda l, j: (0, 0)),
                  pl.BlockSpec((1, D_MODEL, MOD_COLS), lambda l, j: (l, 0, j)),
                  pl.BlockSpec((1, 1, MOD_COLS), lambda l, j: (l, 0, j))],
        out_specs=pl.BlockSpec((1, rows, MOD_COLS), lambda l, j: (l, 0, j)),
        compiler_params=_params(),
        name="adaln_mod",
    )(cvec, w_mod, b_mod.reshape(nl, 1, N_MOD * D_MODEL))
    return out.reshape(nl, rows, N_MOD, D_MODEL)


def _in_proj_kernel(xp_ref, xc_ref, xn_ref, m_ref, norm_ref, w_ref, hs_ref, hb_ref, gain_ref, cos_ref, sin_ref,
                    conv_ref, pm_ref, ps_ref,
                    q_ref, k_ref, v_ref, sc_ref, xf_ref, yp_ref,
                    h_scr, z_scr, *, n, tq, rope, nb):
    i = pl.program_id(0)
    norm = norm_ref[...]

    def norm_stage(e):
        m = m_ref.at[min(e, m_ref.shape[0] - 1)]
        shift, scale = m[0:1, :], m[1:2, :]
        _fill_halo_rows(h_scr.at[e % 2], xp_ref.at[e:e + 1], xc_ref.at[e:e + 1], xn_ref.at[e:e + 1],
                        lambda xv: _rms_mod(xv, norm, shift, scale), tq)

    norm_stage(0)
    for e in range(nb):
        z_scr[e % 2] = _dot(h_scr[e % 2].astype(BF16), w_ref[...])
        if e + 1 < nb:
            norm_stage(e + 1)
        _local_mixers(e, z_scr.at[e % 2], hs_ref, hb_ref, gain_ref, cos_ref, sin_ref, conv_ref, pm_ref, ps_ref,
                      q_ref, k_ref, v_ref, sc_ref, xf_ref, yp_ref, i=i, n=n, tq=tq, rope=rope)


def _local_mixers(e, z_scr, hs_ref, hb_ref, gain_ref, cos_ref, sin_ref, conv_ref, pm_ref, ps_ref,
                  q_ref, k_ref, v_ref, sc_ref, xf_ref, yp_ref, *, i, n, tq, rope):
    lane = lax.broadcasted_iota(jnp.int32, (tq, LANES), 1)
    low_half = lane < HEAD_DIM
    rows = slice(HALO, HALO + tq)

    y = _head_rms(z_scr[rows, 0:QK_WIDTH], hs_ref, hb_ref, gain_ref[...])
    second_half = (lane & AXIS_PAIRS) != 0
    for c in range(QK_WIDTH // LANES):
        yc = y[:, c * LANES:(c + 1) * LANES]
        if rope:
            partner = jnp.where(second_half, pltpu.roll(yc, AXIS_PAIRS, 1),
                                pltpu.roll(yc, LANES - AXIS_PAIRS, 1))
            yc = yc * cos_ref[...] + partner * sin_ref[...]
        if c < Q_WIDTH // LANES:
            q_ref[e, :, c * LANES:(c + 1) * LANES] = yc.astype(BF16)
        else:
            k0, k1 = _dup_heads(yc, low_half)
            k_ref[e, :, 0:LANES] = k0.astype(BF16)
            k_ref[e, :, LANES:2 * LANES] = k1.astype(BF16)
    v0, v1 = _dup_heads(z_scr[rows, OFF_V:OFF_SC], low_half)
    v_ref[e, :, 0:LANES] = v0.astype(BF16)
    v_ref[e, :, LANES:2 * LANES] = v1.astype(BF16)

    def u_at(off):
        r = slice(HALO + off, HALO + off + tq)
        return z_scr[r, OFF_SC + SC_WIDTH:OFF_SC + 2 * SC_WIDTH] * z_scr[r, OFF_SC + 2 * SC_WIDTH:OFF_F]

    conv = u_at(-1) * conv_ref[0:1, :] + u_at(0) * conv_ref[1:2, :] + u_at(1) * conv_ref[2:3, :]
    sc_ref[e] = (z_scr[rows, OFF_SC:OFF_SC + SC_WIDTH] * conv).astype(BF16)

    xf_ref[e] = z_scr[rows, OFF_F:OFF_P].astype(BF16)

    t = i * tq + lax.broadcasted_iota(jnp.int32, (tq, LANES), 0)

    def p_at(off, c):
        return z_scr[HALO + off:HALO + off + tq, OFF_P + c * LANES:OFF_P + (c + 1) * LANES]

    def window_mean(total, w_low, w_high):
        w = jnp.where(low_half, w_low, w_high)
        left, right = (w - 1) // 2, w // 2
        cnt = jnp.minimum(t + right + 1, n) - jnp.maximum(t - left, 0)
        return total / cnt.astype(F32)

    p0 = p_at(0, 0)
    s2 = p0 + p_at(1, 0)
    s4 = s2 + p_at(-1, 0) + p_at(2, 0)
    pooled0 = window_mean(jnp.where(low_half, s2, s4), POOL_WINDOWS[0], POOL_WINDOWS[1]) - p0
    p1 = p_at(0, 1)
    s8 = p1
    for off in (-3, -2, -1, 1, 2, 3, 4):
        s8 = s8 + p_at(off, 1)
    s16 = s8
    for off in (-7, -6, -5, -4, 5, 6, 7, 8):
        s16 = s16 + p_at(off, 1)
    pooled1 = window_mean(jnp.where(low_half, s8, s16), POOL_WINDOWS[2], POOL_WINDOWS[3]) - p1
    pooled = jnp.concatenate([pooled0, pooled1], axis=1).astype(BF16)
    yp_ref[e] = (_dot(pooled, pm_ref[...]) * ps_ref[...]).astype(BF16)


def _in_proj(xs, prm, layer, mod_row, cos_t, sin_t, *, tq, rope):
    b, n, _ = xs.shape
    out_w = (Q_WIDTH, 2 * LANES, 2 * LANES, SC_WIDTH, FOURIER_WIDTH, POOL_WIDTH)
    nb = IN_PROJ_BATCH
    kern = functools.partial(_in_proj_kernel, n=n, tq=tq, rope=rope, nb=nb)
    return pl.pallas_call(
        kern,
        out_shape=[jax.ShapeDtypeStruct((b, n, w), BF16) for w in out_w],
        grid=(n // tq, b // nb),
        in_specs=_halo_specs(n, tq, D_MODEL, nb) + [
            _mod_spec_multi(layer, mod_row, nb),
            _layer_spec((1, D_MODEL), layer),
            _pair_spec(prm["w_in"], (D_MODEL, OFF_G)),
            _fixed_spec((QK_WIDTH, LANES), (0, 0)),
            _fixed_spec((LANES, QK_WIDTH), (0, 0)),
            _layer_spec((1, QK_WIDTH), layer),
            pl.BlockSpec((tq, LANES), lambda i, bb: (i, 0)),
            pl.BlockSpec((tq, LANES), lambda i, bb: (i, 0)),
            _layer_spec((3, SC_WIDTH), layer),
            _layer_spec((POOL_WIDTH, POOL_WIDTH), layer),
            _layer_spec((1, POOL_WIDTH), layer),
        ],
        out_specs=[pl.BlockSpec((nb, tq, w), lambda i, bb: (bb, i, 0)) for w in out_w],
        scratch_shapes=[pltpu.VMEM((2, tq + 2 * HALO, D_MODEL), F32),
                        pltpu.VMEM((2, tq + 2 * HALO, OFF_G), F32)],
        compiler_params=_params(),
        name="in_proj",
    )(xs, xs, xs, prm["mod"], prm["norm1"], prm["w_in"][0], prm["head_sum"], prm["head_bcast"], prm["gain"],
      cos_t, sin_t,
      prm["conv_sc"], prm["pool_mat"], prm["pool_scale"])


def _kv_proj_kernel(x_ref, m_ref, norm_ref, w_ref, hs_ref, hb_ref, gain_ref, k_ref, v_ref, *, tq):
    h = _rms_mod(x_ref[0], norm_ref[...], m_ref[0:1, :], m_ref[1:2, :]).astype(BF16)
    z = _dot(h, w_ref[...])
    low_half = lax.broadcasted_iota(jnp.int32, (tq, LANES), 1) < HEAD_DIM
    k0, k1 = _dup_heads(_head_rms(z[:, 0:KV_WIDTH], hs_ref, hb_ref, gain_ref[...]), low_half)
    k_ref[0, :, 0:LANES] = k0.astype(BF16)
    k_ref[0, :, LANES:2 * LANES] = k1.astype(BF16)
    v0, v1 = _dup_heads(z[:, KV_WIDTH:2 * KV_WIDTH], low_half)
    v_ref[0, :, 0:LANES] = v0.astype(BF16)
    v_ref[0, :, LANES:2 * LANES] = v1.astype(BF16)


def _kv_proj(xs, prm, layer, mod_row, *, tq):
    b, n, _ = xs.shape
    kv_cols = 2 * KV_WIDTH
    return pl.pallas_call(
        functools.partial(_kv_proj_kernel, tq=tq),
        out_shape=[jax.ShapeDtypeStruct((b, n, 2 * LANES), BF16)] * 2,
        grid=(n // tq, b),
        in_specs=[pl.BlockSpec((1, tq, D_MODEL), lambda i, bb: (bb, i, 0)),
                  _mod_spec(layer, mod_row),
                  _layer_spec((1, D_MODEL), layer),
                  _pair_spec(prm["w_in"], (D_MODEL, kv_cols), (0, OFF_K // kv_cols)),
                  _fixed_spec((KV_WIDTH, LANES), (0, 0)),
                  _fixed_spec((LANES, KV_WIDTH), (0, 0)),
                  _layer_spec((1, KV_WIDTH), layer, (0, Q_WIDTH // KV_WIDTH))],
        out_specs=[pl.BlockSpec((1, tq, 2 * LANES), lambda i, bb: (bb, i, 0))] * 2,
        compiler_params=_params(),
        name="kv_proj",
    )(xs, prm["mod"], prm["norm1"], prm["w_in"][0], prm["head_sum"], prm["head_bcast"], prm["gain"])


def _attn_kernel(q_ref, *refs, tq):
    *kv_refs, o_ref = refs
    sub = min(tq, ATTN_SUB)
    lane = lax.broadcasted_iota(jnp.int32, (sub, LANES), 1)
    low_half = lane < HEAD_DIM
    sources = [(kv_refs[2 * a], kv_refs[2 * a + 1]) for a in range(len(kv_refs) // 2)]
    for j in range(Q_WIDTH // LANES):
        g = (2 * j) // (N_HEADS // N_KV_HEADS)
        gl = slice(g * LANES, (g + 1) * LANES)
        for r in range(tq // sub):
            rows = slice(r * sub, (r + 1) * sub)
            qp = q_ref[0, rows, j * LANES:(j + 1) * LANES]
            zero = jnp.zeros_like(qp)
            q2 = jnp.concatenate([jnp.where(low_half, qp, zero), jnp.where(low_half, zero, qp)], axis=0)
            scores = [_dot_nt(q2, k_ref[0, :, gl]) for k_ref, _ in sources]
            row_max = functools.reduce(jnp.maximum, [jnp.max(s, axis=-1, keepdims=True) for s in scores])
            o, denom = None, None
            for s, (_, v_ref) in zip(scores, sources):
                p = jnp.exp(s - row_max)
                part = jnp.sum(p, axis=-1, keepdims=True)
                term = _dot(p.astype(BF16), v_ref[0, :, gl])
                o, denom = (term, part) if o is None else (o + term, denom + part)
            o = o / denom
            o_ref[0, rows, j * LANES:(j + 1) * LANES] = jnp.where(low_half, o[:sub], o[sub:]).astype(BF16)


def _attention(q, kv_sources, *, tq):
    b, n, _ = q.shape
    kv_specs, kv_args = [], []
    for k, v in kv_sources:
        for a in (k, v):
            kv_specs.append(pl.BlockSpec((1, a.shape[1], 2 * LANES), lambda bb, i: (bb, 0, 0)))
            kv_args.append(a)
    return pl.pallas_call(
        functools.partial(_attn_kernel, tq=tq),
        out_shape=jax.ShapeDtypeStruct((b, n, Q_WIDTH), BF16),
        grid=(b, n // tq),
        in_specs=[pl.BlockSpec((1, tq, Q_WIDTH), lambda bb, i: (bb, i, 0))] + kv_specs,
        out_specs=pl.BlockSpec((1, tq, Q_WIDTH), lambda bb, i: (bb, i, 0)),
        compiler_params=_params(),
        name="attention",
    )(q, *kv_args)


def _fourier_kernel(x_ref, wc_ref, cn_ref, nsn_ref, o_ref, t_scr, *, n, tr):
    t_scr[...] = _dot(x_ref[0], wc_ref[...].astype(BF16)).astype(BF16)
    scale = 1.0 / math.sqrt(n * FOURIER_GROUP)
    for r in range(n // tr):
        rows = slice(r * tr, (r + 1) * tr)
        acc = (_dot(cn_ref[rows, :].astype(BF16), t_scr[:, 0:FOURIER_WIDTH])
               + _dot(nsn_ref[rows, :].astype(BF16), t_scr[:, FOURIER_WIDTH:2 * FOURIER_WIDTH]))
        o_ref[0, rows, :] = (acc * scale).astype(BF16)


def _fourier(xf, wc, cn, nsn):
    b, n, _ = xf.shape
    tr = min(n, 512)
    return pl.pallas_call(
        functools.partial(_fourier_kernel, n=n, tr=tr),
        out_shape=jax.ShapeDtypeStruct((b, n, FOURIER_WIDTH), BF16),
        grid=(b, 1),
        in_specs=[pl.BlockSpec((1, n, FOURIER_WIDTH), lambda bb, i: (bb, 0, 0)),
                  _fixed_spec((FOURIER_WIDTH, 2 * FOURIER_WIDTH), (0, 0)),
                  _fixed_spec((n, n), (0, 0)),
                  _fixed_spec((n, n), (0, 0))],
        out_specs=pl.BlockSpec((1, n, FOURIER_WIDTH), lambda bb, i: (bb, 0, 0)),
        scratch_shapes=[pltpu.VMEM((n, 2 * FOURIER_WIDTH), BF16)],
        compiler_params=_params(),
        name="fourier",
    )(xf, wc, cn, nsn)


N_MIX_INPUTS = 14


def _mix_kernel(*refs, nb, n_cast):
    (x_ref, m_ref, norm_ref, oa_ref, sc_ref, xf_ref, yp_ref,
     wg01_ref, wg23_ref, wa_ref, wsc_ref, wf_ref, wp_ref, wo_ref) = refs[:N_MIX_INPUTS]
    cast_in = refs[N_MIX_INPUTS:N_MIX_INPUTS + n_cast]
    out_ref = refs[N_MIX_INPUTS + n_cast]
    cast_out = refs[N_MIX_INPUTS + n_cast + 1:N_MIX_INPUTS + 2 * n_cast + 1]
    h_scr, y_scr = refs[-2:]
    _run_casts(cast_in, cast_out)
    branches = ((oa_ref, wa_ref), (sc_ref, wsc_ref), (xf_ref, wf_ref), (yp_ref, wp_ref))
    gate_w = (wg01_ref, wg23_ref)
    norm = norm_ref[...]

    def mod(e):
        return m_ref.at[min(e, m_ref.shape[0] - 1)]

    def norm_stage(e):
        m = mod(e)
        h_scr[e] = _rms_mod(x_ref[e], norm, m[0:1, :], m[1:2, :]).astype(BF16)

    norm_stage(0)
    for e in range(nb):
        h = h_scr[e]
        for j in range(D_MODEL // FF_CHUNK):
            cols = slice(j * FF_CHUNK, (j + 1) * FF_CHUNK)
            y = None
            for bi, (a_ref, w_ref) in enumerate(branches):
                off = (bi % 2) * D_MODEL + j * FF_CHUNK
                gate = 0.5 + 0.5 * jnp.tanh(0.5 * _dot(h, gate_w[bi // 2][:, off:off + FF_CHUNK]))
                term = gate * _dot(a_ref[e], w_ref[:, cols])
                y = term if y is None else y + term
            y_scr[e, :, cols] = y.astype(BF16)
            if j == 0 and e + 1 < nb:
                norm_stage(e + 1)
        out_ref[e] = x_ref[e] + mod(e)[2:3, :] * _dot(y_scr[e], wo_ref[...])


def _mix(xs, prm, layer, mod_row, oa, sc, xf, yp, *, tq, casts=()):
    b, n, _ = xs.shape
    nb = MIX_BATCH
    half_gates = N_BRANCHES * D_MODEL // 2
    grid = (n // tq, b // nb)
    cast_in, cast_out, cast_shapes = _cast_plumbing(casts, grid[0] * grid[1], lambda i, bb: i * grid[1] + bb)

    def tok(w):
        return pl.BlockSpec((nb, tq, w), lambda i, bb: (bb, i, 0))

    in_specs = [tok(D_MODEL), _mod_spec_multi(layer, mod_row, nb), _layer_spec((1, D_MODEL), layer),
                tok(Q_WIDTH), tok(SC_WIDTH), tok(FOURIER_WIDTH), tok(POOL_WIDTH),
                _pair_spec(prm["w_in"], (D_MODEL, half_gates), (0, OFF_G // half_gates)),
                _pair_spec(prm["w_in"], (D_MODEL, half_gates), (0, OFF_G // half_gates + 1)),
                _layer_spec((Q_WIDTH, D_MODEL), layer), _layer_spec((SC_WIDTH, D_MODEL), layer),
                _layer_spec((FOURIER_WIDTH, D_MODEL), layer), _layer_spec((POOL_WIDTH, D_MODEL), layer),
                _layer_spec((D_MODEL, D_MODEL), layer)]
    assert len(in_specs) == N_MIX_INPUTS
    return pl.pallas_call(
        functools.partial(_mix_kernel, nb=nb, n_cast=len(casts)),
        out_shape=[jax.ShapeDtypeStruct((b, n, D_MODEL), F32)] + cast_shapes,
        grid=grid,
        in_specs=in_specs + cast_in,
        out_specs=[tok(D_MODEL)] + cast_out,
        scratch_shapes=[pltpu.VMEM((nb, tq, D_MODEL), BF16), pltpu.VMEM((nb, tq, D_MODEL), BF16)],
        compiler_params=_params(),
        name="mix",
    )(xs, prm["mod"], prm["norm1"], oa, sc, xf, yp, prm["w_in"][0], prm["w_in"][0],
      prm["w_br_attn"], prm["w_br_sc"], prm["w_br_f"], prm["w_br_p"], prm["w_out"], *[a for a, _ in casts])


def _ffn_kernel(xp_ref, xc_ref, xn_ref, m_ref, norm_ref, wu_ref, cw_ref, wd_ref, fn_ref, out_ref,
                h_scr, hb_scr, u0_scr, u1_scr, *, tq, final, nb):
    norm = norm_ref[...]
    ext = tq + 2 * HALO

    def mod(e):
        return m_ref.at[min(e, m_ref.shape[0] - 1)]

    def cols(c, half):
        return slice(half * D_FF + c * FF_CHUNK, half * D_FF + (c + 1) * FF_CHUNK)

    def norm_stage(e):
        m = mod(e)
        shift, scale = m[3:4, :], m[4:5, :]
        _fill_halo_rows(h_scr.at[e], xp_ref.at[e:e + 1], xc_ref.at[e:e + 1], xn_ref.at[e:e + 1],
                        lambda xv: _rms_mod(xv, norm, shift, scale), tq)
        hb_scr[e] = h_scr[e].astype(BF16)

    def up(e, c, u_scr):
        hb = hb_scr[e]
        u_scr[0] = _dot(hb, wu_ref[:, cols(c, 0)])
        u_scr[1] = _dot(hb, wu_ref[:, cols(c, 1)])

    def conv3(u, w):
        full = pltpu.roll(u, 1, 0) * w[0:1, :] + u * w[1:2, :] + pltpu.roll(u, ext - 1, 0) * w[2:3, :]
        return full[HALO:HALO + tq]

    def down(c, u_scr):
        a = conv3(u_scr[0], cw_ref[:, cols(c, 0)])
        bv = conv3(u_scr[1], cw_ref[:, cols(c, 1)])
        half = 0.5 * a
        act = ((half + half * jnp.tanh(half)) * bv).astype(BF16)
        return _dot(act, wd_ref[c * FF_CHUNK:(c + 1) * FF_CHUNK, :])

    bufs = (u0_scr, u1_scr)
    units = [(e, c) for e in range(nb) for c in range(N_FF_CHUNKS)]
    norm_stage(0)
    up(0, 0, u0_scr)
    acc = None
    for k, (e, c) in enumerate(units):
        if k + 1 < len(units):
            e_next, c_next = units[k + 1]
            if c_next == 0:
                norm_stage(e_next)
            up(e_next, c_next, bufs[(k + 1) % 2])
        d = down(c, bufs[k % 2])
        acc = d if c == 0 else acc + d
        if c == N_FF_CHUNKS - 1:
            out = xc_ref[e] + mod(e)[5:6, :] * acc
            if final:
                ms = jnp.mean(out * out, axis=-1, keepdims=True)
                out = out * lax.rsqrt(ms + NORM_EPS) * fn_ref[...]
            out_ref[e] = out


def _ffn(xs, prm, layer, mod_row, *, tq, final):
    b, n, _ = xs.shape
    nb = FFN_BATCH
    return pl.pallas_call(
        functools.partial(_ffn_kernel, tq=tq, final=final, nb=nb),
        out_shape=jax.ShapeDtypeStruct((b, n, D_MODEL), F32),
        grid=(n // tq, b // nb),
        in_specs=_halo_specs(n, tq, D_MODEL, nb) + [
            _mod_spec_multi(layer, mod_row, nb),
            _layer_spec((1, D_MODEL), layer),
            _pair_spec(prm["w_up"], (D_MODEL, 2 * D_FF)),
            _layer_spec((3, 2 * D_FF), layer),
            _pair_spec(prm["w_down"], (D_FF, D_MODEL)),
            _fixed_spec((1, D_MODEL), (0, 0))],
        out_specs=pl.BlockSpec((nb, tq, D_MODEL), lambda i, bb: (bb, i, 0)),
        scratch_shapes=[pltpu.VMEM((nb, tq + 2 * HALO, D_MODEL), F32),
                        pltpu.VMEM((nb, tq + 2 * HALO, D_MODEL), BF16),
                        pltpu.VMEM((2, tq + 2 * HALO, FF_CHUNK), F32),
                        pltpu.VMEM((2, tq + 2 * HALO, FF_CHUNK), F32)],
        compiler_params=_params(),
        name="ffn",
    )(xs, xs, xs, prm["mod"], prm["norm2"], prm["w_up"][0], prm["w_conv_ffn"], prm["w_down"][0],
      prm["final_norm"])


def _rope_tables(n):
    d = np.arange(LANES)
    hd = d % HEAD_DIM
    axis = hd // (2 * AXIS_PAIRS)
    half = (hd % (2 * AXIS_PAIRS)) // AXIS_PAIRS
    pair = hd % AXIS_PAIRS
    t = np.arange(n)
    pos = np.where(axis[None, :] == 0, (t // GRID_W)[:, None], (t % GRID_W)[:, None]).astype(np.float64)
    inv = ROPE_THETA ** (-np.arange(AXIS_PAIRS, dtype=np.float64) / AXIS_PAIRS)
    ang = pos * inv[pair][None, :]
    sin = np.sin(ang)
    return (jnp.asarray(np.cos(ang), dtype=F32),
            jnp.asarray(np.where(half[None, :] == 0, -sin, sin), dtype=F32))


def _dft_tables(n):
    k = np.arange(n, dtype=np.int64)
    ang = ((k[:, None] * k[None, :]) % n).astype(np.float64) * (2.0 * math.pi / n)
    return jnp.asarray(np.cos(ang), dtype=F32), jnp.asarray(-np.sin(ang), dtype=F32)


def _channel_dft():
    c = np.arange(FOURIER_WIDTH, dtype=np.int64)
    same = (c[:, None] // FOURIER_GROUP) == (c[None, :] // FOURIER_GROUP)
    mc = ((c[:, None] % FOURIER_GROUP) * (c[None, :] % FOURIER_GROUP)) % FOURIER_GROUP
    ang = mc.astype(np.float64) * (2.0 * math.pi / FOURIER_GROUP)
    both = np.concatenate([np.where(same, np.cos(ang), 0.0), np.where(same, np.sin(ang), 0.0)], axis=1)
    return jnp.asarray(both, dtype=F32)


def _head_matrices():
    r = np.arange(QK_WIDTH) // HEAD_DIM
    m = (r[:, None] == np.arange(LANES)[None, :]).astype(np.float32)
    return jnp.asarray(m.astype(BF16)), jnp.asarray(m.T.copy().astype(BF16))


def _block_diag(mats):
    nl, g, a, b = mats.shape
    eye = jnp.eye(g, dtype=mats.dtype)
    return (mats[:, :, :, None, :] * eye[None, :, None, :, None]).reshape(nl, g * a, g * b)


def kernel(x, c, ctx, c_ctx, w_mod, b_mod, norm1, norm2, w_in, conv_sc, qk_gain, pool_mat, pool_scale,
           w_br_attn, w_br_sc, w_br_f, w_br_p, w_out, w_up, w_conv_ffn, w_down, final_norm):
    bsz, n_lat, _ = x.shape
    n_ctx = ctx.shape[1]
    nl = w_in.shape[0]
    tq_lat = min(n_lat, TOKEN_TILE)
    tq_ctx = min(n_ctx, TOKEN_TILE)
    tq_attn = min(n_lat, ATTN_TILE)
    tq_ffn_lat = min(n_lat, FFN_TILE)
    tq_ffn_ctx = min(n_ctx, FFN_TILE)

    mod_rows = -(-(bsz + 1) // BF16_ROWS) * BF16_ROWS
    ctx_row = bsz
    cvec = jnp.concatenate([c, c_ctx[None], jnp.zeros((mod_rows - bsz - 1, D_MODEL), F32)], axis=0)
    gain = jnp.concatenate([jnp.tile(qk_gain[:, 0], (1, N_HEADS)) * (HEAD_DIM ** -0.5),
                            jnp.tile(qk_gain[:, 1], (1, N_KV_HEADS))], axis=1)
    head_sum, head_bcast = _head_matrices()
    prm = {
        "mod": _modulation(cvec, w_mod, b_mod),
        "norm1": norm1.reshape(nl, 1, D_MODEL),
        "norm2": norm2.reshape(nl, 1, D_MODEL),
        "w_in": (w_in[:1].astype(BF16), 0),
        "head_sum": head_sum,
        "head_bcast": head_bcast,
        "gain": gain.reshape(nl, 1, QK_WIDTH),
        "conv_sc": conv_sc,
        "pool_mat": _block_diag(pool_mat).astype(BF16),
        "pool_scale": pool_scale.reshape(nl, 1, POOL_WIDTH),
        "w_br_attn": w_br_attn.astype(BF16),
        "w_br_sc": w_br_sc.astype(BF16),
        "w_br_f": w_br_f.astype(BF16),
        "w_br_p": w_br_p.astype(BF16),
        "w_out": w_out.astype(BF16),
        "w_conv_ffn": w_conv_ffn,
        "final_norm": final_norm.reshape(1, D_MODEL),
    }
    cos_t, sin_t = _rope_tables(n_lat)
    wc = _channel_dft()
    dft_lat = _dft_tables(n_lat)
    dft_ctx = _dft_tables(n_ctx)

    lat, cx = x, ctx
    for l in range(nl):
        last = l == nl - 1
        if last:
            kc, vc = _kv_proj(cx, prm, l, ctx_row, tq=tq_ctx)
        else:
            qc, kc, vc, scc, xfc, ypc = _in_proj(cx, prm, l, ctx_row, cos_t, sin_t, tq=tq_ctx, rope=False)
        ql, kl, vl, scl, xfl, ypl = _in_proj(lat, prm, l, None, cos_t, sin_t, tq=tq_lat, rope=True)
        oa = _attention(ql, [(kl, vl), (kc, vc)], tq=tq_attn)
        xfour = _fourier(xfl, wc, *dft_lat)
        lat, w_up_bf, w_down_bf, *next_w_in = _mix(
            lat, prm, l, None, oa, scl, xfour, ypl, tq=tq_lat,
            casts=[(w_up, l), (w_down, l)] + ([] if last else [(w_in, l + 1)]))
        prm["w_up"], prm["w_down"] = (w_up_bf, 0), (w_down_bf, 0)
        lat = _ffn(lat, prm, l, None, tq=tq_ffn_lat, final=last)

        if not last:
            oac = _attention(qc, [(kc, vc)], tq=tq_ctx)
            xfourc = _fourier(xfc, wc, *dft_ctx)
            cx, = _mix(cx, prm, l, ctx_row, oac, scc, xfourc, ypc, tq=tq_ctx)
            cx = _ffn(cx, prm, l, ctx_row, tq=tq_ffn_ctx, final=False)
            prm["w_in"] = (next_w_in[0], 0)
    return lat
```

```python
import functools
import math

import numpy as np
import jax
import jax.numpy as jnp
from jax import lax
from jax.experimental import pallas as pl
from jax.experimental.pallas import tpu as pltpu

D_MODEL = 1024
DEPTH = 2
GRID_W = 64
NORM_EPS = 1e-6
N_HEADS = 8
N_KV_HEADS = 2
HEAD_DIM = 64
Q_WIDTH = N_HEADS * HEAD_DIM
KV_WIDTH = N_KV_HEADS * HEAD_DIM
QK_WIDTH = Q_WIDTH + KV_WIDTH
ROPE_THETA = 10000.0
AXIS_PAIRS = HEAD_DIM // 4
SC_WIDTH = 256
FOURIER_GROUPS = 4
FOURIER_WIDTH = 256
FOURIER_GROUP = FOURIER_WIDTH // FOURIER_GROUPS
POOL_WINDOWS = (2, 4, 8, 16)
POOL_WIDTH = 256
N_BRANCHES = 4
D_FF = 2816

OFF_K = Q_WIDTH
OFF_V = OFF_K + KV_WIDTH
OFF_SC = OFF_V + KV_WIDTH
OFF_F = OFF_SC + 3 * SC_WIDTH
OFF_P = OFF_F + FOURIER_WIDTH
OFF_G = OFF_P + POOL_WIDTH

LANES = 128
BF16_ROWS = 16
HALO = 8
FF_CHUNK = 256
N_FF_CHUNKS = D_FF // FF_CHUNK
ATTN_SUB = 256
TOKEN_TILE = 512
IN_PROJ_BATCH = 2
MOD_COLS = 2048
MIX_ROWS = 1024
FFN_BATCH = 4
ATTN_TILE = 1024
FFN_TILE = 256
N_MOD = 6
VMEM_LIMIT = 62 * 1024 * 1024

F32 = jnp.float32
BF16 = jnp.bfloat16


def _dot(a, b):
    return jnp.dot(a, b, preferred_element_type=F32)


def _dot_nt(a, b):
    return lax.dot_general(a, b, (((1,), (1,)), ((), ())), preferred_element_type=F32)


def _fixed_spec(block, index):
    return pl.BlockSpec(block, lambda *_: index, pipeline_mode=pl.Buffered(1))


def _layer_spec(tail, layer, tail_index=None):
    tail_index = (0,) * len(tail) if tail_index is None else tail_index
    return _fixed_spec((None,) + tuple(tail), (layer,) + tuple(tail_index))


def _pair_spec(pair, tail, tail_index=None):
    return _layer_spec(tail, pair[1], tail_index)


def _cast_plumbing(casts, steps, flat_index):
    in_specs, out_specs, out_shapes = [], [], []
    for arr, layer in casts:
        _, r, c = arr.shape
        rows = r // steps
        assert rows * steps == r and rows % BF16_ROWS == 0, (arr.shape, steps)
        in_specs.append(pl.BlockSpec((None, rows, c), lambda i, b, layer=layer: (layer, flat_index(i, b), 0)))
        out_specs.append(pl.BlockSpec((None, rows, c), lambda i, b: (0, flat_index(i, b), 0)))
        out_shapes.append(jax.ShapeDtypeStruct((1, r, c), BF16))
    return in_specs, out_specs, out_shapes


def _run_casts(cast_in, cast_out):
    for src, dst in zip(cast_in, cast_out):
        dst[...] = src[...].astype(BF16)


def _mod_spec(layer, row):
    if row is None:
        return pl.BlockSpec((None, None, N_MOD, D_MODEL), lambda i, b: (layer, b, 0, 0))
    return _fixed_spec((None, None, N_MOD, D_MODEL), (layer, row, 0, 0))


def _mod_spec_multi(layer, row, nb):
    if row is None:
        return pl.BlockSpec((None, nb, N_MOD, D_MODEL), lambda i, b: (layer, b, 0, 0))
    return _fixed_spec((None, 1, N_MOD, D_MODEL), (layer, row, 0, 0))


def _params():
    return pltpu.CompilerParams(dimension_semantics=("arbitrary", "arbitrary"),
                                vmem_limit_bytes=VMEM_LIMIT)


def _rms_mod(xv, norm, shift, scale):
    ms = jnp.mean(xv * xv, axis=-1, keepdims=True)
    y = xv * lax.rsqrt(ms + NORM_EPS) * norm
    return y * (1.0 + scale) + shift


def _halo_specs(n, tq, width, nb=1):
    nblk = n // HALO
    per = tq // HALO
    prev = pl.BlockSpec((nb, HALO, width), lambda i, b: (b, jnp.maximum(i * per - 1, 0), 0))
    cur = pl.BlockSpec((nb, tq, width), lambda i, b: (b, i, 0))
    nxt = pl.BlockSpec((nb, HALO, width), lambda i, b: (b, jnp.minimum((i + 1) * per, nblk - 1), 0))
    return [prev, cur, nxt]


def _fill_halo_rows(h_scr, xp_ref, xc_ref, xn_ref, fn, tq):
    i = pl.program_id(0)
    last = pl.num_programs(0) - 1
    h_scr[HALO:HALO + tq, :] = fn(xc_ref[0])
    h_scr[0:HALO, :] = jnp.where(i > 0, fn(xp_ref[0]), 0.0)
    h_scr[HALO + tq:2 * HALO + tq, :] = jnp.where(i < last, fn(xn_ref[0]), 0.0)


def _split_bf16(a):
    hi = a.astype(BF16)
    return hi, (a - hi.astype(F32)).astype(BF16)


def _head_rms(z, sum_ref, bcast_ref, gain):
    sq_hi, sq_lo = _split_bf16(z * z)
    ss = _dot(sq_hi, sum_ref[...]) + _dot(sq_lo, sum_ref[...])
    r_hi, r_lo = _split_bf16(lax.rsqrt(ss * (1.0 / HEAD_DIM) + NORM_EPS))
    return z * (_dot(r_hi, bcast_ref[...]) + _dot(r_lo, bcast_ref[...])) * gain


def _dup_heads(a, low_half):
    sw = pltpu.roll(a, HEAD_DIM, 1)
    return jnp.where(low_half, a, sw), jnp.where(low_half, sw, a)


def _mod_kernel(c_ref, w_ref, b_ref, o_ref):
    cv = c_ref[...]
    a = (cv * jax.nn.sigmoid(cv)).astype(BF16)
    o_ref[0] = _dot(a, w_ref[0].astype(BF16)) + b_ref[0]


def _modulation(cvec, w_mod, b_mod):
    rows = cvec.shape[0]
    nl = w_mod.shape[0]
    out = pl.pallas_call(
        _mod_kernel,
        out_shape=jax.ShapeDtypeStruct((nl, rows, N_MOD * D_MODEL), F32),
        grid=(nl, N_MOD * D_MODEL // MOD_COLS),
        in_specs=[pl.BlockSpec((rows, D_MODEL), lambda l, j: (0, 0)),
                  pl.BlockSpec((1, D_MODEL, MOD_COLS), lambda l, j: (l, 0, j)),
                  pl.BlockSpec((1, 1, MOD_COLS), lambda l, j: (l, 0, j))],
        out_specs=pl.BlockSpec((1, rows, MOD_COLS), lambda l, j: (l, 0, j)),
        compiler_params=_params(),
        name="adaln_mod",
    )(cvec, w_mod, b_mod.reshape(nl, 1, N_MOD * D_MODEL))
    return out.reshape(nl, rows, N_MOD, D_MODEL)


def _in_proj_kernel(xp_ref, xc_ref, xn_ref, m_ref, norm_ref, w_ref, hs_ref, hb_ref, gain_ref, cos_ref, sin_ref,
                    conv_ref, pm_ref, ps_ref,
                    q_ref, k_ref, v_ref, sc_ref, xf_ref, yp_ref,
                    h_scr, z_scr, *, n, tq, rope, nb):
    i = pl.program_id(0)
    norm = norm_ref[...]

    def norm_stage(e):
        m = m_ref.at[min(e, m_ref.shape[0] - 1)]
        shift, scale = m[0:1, :], m[1:2, :]
        _fill_halo_rows(h_scr.at[e % 2], xp_ref.at[e:e + 1], xc_ref.at[e:e + 1], xn_ref.at[e:e + 1],
                        lambda xv: _rms_mod(xv, norm, shift, scale), tq)

    norm_stage(0)
    for e in range(nb):
        z_scr[e % 2] = _dot(h_scr[e % 2].astype(BF16), w_ref[...])
        if e + 1 < nb:
            norm_stage(e + 1)
        _local_mixers(e, z_scr.at[e % 2], hs_ref, hb_ref, gain_ref, cos_ref, sin_ref, conv_ref, pm_ref, ps_ref,
                      q_ref, k_ref, v_ref, sc_ref, xf_ref, yp_ref, i=i, n=n, tq=tq, rope=rope)


def _local_mixers(e, z_scr, hs_ref, hb_ref, gain_ref, cos_ref, sin_ref, conv_ref, pm_ref, ps_ref,
                  q_ref, k_ref, v_ref, sc_ref, xf_ref, yp_ref, *, i, n, tq, rope):
    lane = lax.broadcasted_iota(jnp.int32, (tq, LANES), 1)
    low_half = lane < HEAD_DIM
    rows = slice(HALO, HALO + tq)

    y = _head_rms(z_scr[rows, 0:QK_WIDTH], hs_ref, hb_ref, gain_ref[...])
    second_half = (lane & AXIS_PAIRS) != 0
    for c in range(QK_WIDTH // LANES):
        yc = y[:, c * LANES:(c + 1) * LANES]
        if rope:
            partner = jnp.where(second_half, pltpu.roll(yc, AXIS_PAIRS, 1),
                                pltpu.roll(yc, LANES - AXIS_PAIRS, 1))
            yc = yc * cos_ref[...] + partner * sin_ref[...]
        if c < Q_WIDTH // LANES:
            q_ref[e, :, c * LANES:(c + 1) * LANES] = yc.astype(BF16)
        else:
            k0, k1 = _dup_heads(yc, low_half)
            k_ref[e, :, 0:LANES] = k0.astype(BF16)
            k_ref[e, :, LANES:2 * LANES] = k1.astype(BF16)
    v0, v1 = _dup_heads(z_scr[rows, OFF_V:OFF_SC], low_half)
    v_ref[e, :, 0:LANES] = v0.astype(BF16)
    v_ref[e, :, LANES:2 * LANES] = v1.astype(BF16)

    def u_at(off):
        r = slice(HALO + off, HALO + off + tq)
        return z_scr[r, OFF_SC + SC_WIDTH:OFF_SC + 2 * SC_WIDTH] * z_scr[r, OFF_SC + 2 * SC_WIDTH:OFF_F]

    conv = u_at(-1) * conv_ref[0:1, :] + u_at(0) * conv_ref[1:2, :] + u_at(1) * conv_ref[2:3, :]
    sc_ref[e] = (z_scr[rows, OFF_SC:OFF_SC + SC_WIDTH] * conv).astype(BF16)

    xf_ref[e] = z_scr[rows, OFF_F:OFF_P].astype(BF16)

    t = i * tq + lax.broadcasted_iota(jnp.int32, (tq, LANES), 0)

    def p_at(off, c):
        return z_scr[HALO + off:HALO + off + tq, OFF_P + c * LANES:OFF_P + (c + 1) * LANES]

    def window_mean(total, w_low, w_high):
        w = jnp.where(low_half, w_low, w_high)
        left, right = (w - 1) // 2, w // 2
        cnt = jnp.minimum(t + right + 1, n) - jnp.maximum(t - left, 0)
        return total / cnt.astype(F32)

    p0 = p_at(0, 0)
    s2 = p0 + p_at(1, 0)
    s4 = s2 + p_at(-1, 0) + p_at(2, 0)
    pooled0 = window_mean(jnp.where(low_half, s2, s4), POOL_WINDOWS[0], POOL_WINDOWS[1]) - p0
    p1 = p_at(0, 1)
    s8 = p1
    for off in (-3, -2, -1, 1, 2, 3, 4):
        s8 = s8 + p_at(off, 1)
    s16 = s8
    for off in (-7, -6, -5, -4, 5, 6, 7, 8):
        s16 = s16 + p_at(off, 1)
    pooled1 = window_mean(jnp.where(low_half, s8, s16), POOL_WINDOWS[2], POOL_WINDOWS[3]) - p1
    pooled = jnp.concatenate([pooled0, pooled1], axis=1).astype(BF16)
    yp_ref[e] = (_dot(pooled, pm_ref[...]) * ps_ref[...]).astype(BF16)


def _in_proj(xs, prm, layer, mod_row, cos_t, sin_t, *, tq, rope):
    b, n, _ = xs.shape
    out_w = (Q_WIDTH, 2 * LANES, 2 * LANES, SC_WIDTH, FOURIER_WIDTH, POOL_WIDTH)
    nb = IN_PROJ_BATCH
    kern = functools.partial(_in_proj_kernel, n=n, tq=tq, rope=rope, nb=nb)
    return pl.pallas_call(
        kern,
        out_shape=[jax.ShapeDtypeStruct((b, n, w), BF16) for w in out_w],
        grid=(n // tq, b // nb),
        in_specs=_halo_specs(n, tq, D_MODEL, nb) + [
            _mod_spec_multi(layer, mod_row, nb),
            _layer_spec((1, D_MODEL), layer),
            _pair_spec(prm["w_in"], (D_MODEL, OFF_G)),
            _fixed_spec((QK_WIDTH, LANES), (0, 0)),
            _fixed_spec((LANES, QK_WIDTH), (0, 0)),
            _layer_spec((1, QK_WIDTH), layer),
            pl.BlockSpec((tq, LANES), lambda i, bb: (i, 0)),
            pl.BlockSpec((tq, LANES), lambda i, bb: (i, 0)),
            _layer_spec((3, SC_WIDTH), layer),
            _layer_spec((POOL_WIDTH, POOL_WIDTH), layer),
            _layer_spec((1, POOL_WIDTH), layer),
        ],
        out_specs=[pl.BlockSpec((nb, tq, w), lambda i, bb: (bb, i, 0)) for w in out_w],
        scratch_shapes=[pltpu.VMEM((2, tq + 2 * HALO, D_MODEL), F32),
                        pltpu.VMEM((2, tq + 2 * HALO, OFF_G), F32)],
        compiler_params=_params(),
        name="in_proj",
    )(xs, xs, xs, prm["mod"], prm["norm1"], prm["w_in"][0], prm["head_sum"], prm["head_bcast"], prm["gain"],
      cos_t, sin_t,
      prm["conv_sc"], prm["pool_mat"], prm["pool_scale"])


def _kv_proj_kernel(x_ref, m_ref, norm_ref, w_ref, hs_ref, hb_ref, gain_ref, k_ref, v_ref, *, tq):
    h = _rms_mod(x_ref[0], norm_ref[...], m_ref[0:1, :], m_ref[1:2, :]).astype(BF16)
    z = _dot(h, w_ref[...])
    low_half = lax.broadcasted_iota(jnp.int32, (tq, LANES), 1) < HEAD_DIM
    k0, k1 = _dup_heads(_head_rms(z[:, 0:KV_WIDTH], hs_ref, hb_ref, gain_ref[...]), low_half)
    k_ref[0, :, 0:LANES] = k0.astype(BF16)
    k_ref[0, :, LANES:2 * LANES] = k1.astype(BF16)
    v0, v1 = _dup_heads(z[:, KV_WIDTH:2 * KV_WIDTH], low_half)
    v_ref[0, :, 0:LANES] = v0.astype(BF16)
    v_ref[0, :, LANES:2 * LANES] = v1.astype(BF16)


def _kv_proj(xs, prm, layer, mod_row, *, tq):
    b, n, _ = xs.shape
    kv_cols = 2 * KV_WIDTH
    return pl.pallas_call(
        functools.partial(_kv_proj_kernel, tq=tq),
        out_shape=[jax.ShapeDtypeStruct((b, n, 2 * LANES), BF16)] * 2,
        grid=(n // tq, b),
        in_specs=[pl.BlockSpec((1, tq, D_MODEL), lambda i, bb: (bb, i, 0)),
                  _mod_spec(layer, mod_row),
                  _layer_spec((1, D_MODEL), layer),
                  _pair_spec(prm["w_in"], (D_MODEL, kv_cols), (0, OFF_K // kv_cols)),
                  _fixed_spec((KV_WIDTH, LANES), (0, 0)),
                  _fixed_spec((LANES, KV_WIDTH), (0, 0)),
                  _layer_spec((1, KV_WIDTH), layer, (0, Q_WIDTH // KV_WIDTH))],
        out_specs=[pl.BlockSpec((1, tq, 2 * LANES), lambda i, bb: (bb, i, 0))] * 2,
        compiler_params=_params(),
        name="kv_proj",
    )(xs, prm["mod"], prm["norm1"], prm["w_in"][0], prm["head_sum"], prm["head_bcast"], prm["gain"])


def _attn_kernel(q_ref, *refs, tq):
    *kv_refs, o_ref = refs
    sub = min(tq, ATTN_SUB)
    lane = lax.broadcasted_iota(jnp.int32, (sub, LANES), 1)
    low_half = lane < HEAD_DIM
    sources = [(kv_refs[2 * a], kv_refs[2 * a + 1]) for a in range(len(kv_refs) // 2)]
    for j in range(Q_WIDTH // LANES):
        g = (2 * j) // (N_HEADS // N_KV_HEADS)
        gl = slice(g * LANES, (g + 1) * LANES)
        for r in range(tq // sub):
            rows = slice(r * sub, (r + 1) * sub)
            qp = q_ref[0, rows, j * LANES:(j + 1) * LANES]
            zero = jnp.zeros_like(qp)
            q2 = jnp.concatenate([jnp.where(low_half, qp, zero), jnp.where(low_half, zero, qp)], axis=0)
            scores = [_dot_nt(q2, k_ref[0, :, gl]) for k_ref, _ in sources]
            row_max = functools.reduce(jnp.maximum, [jnp.max(s, axis=-1, keepdims=True) for s in scores])
            o, denom = None, None
            for s, (_, v_ref) in zip(scores, sources):
                p = jnp.exp(s - row_max)
                part = jnp.sum(p, axis=-1, keepdims=True)
                term = _dot(p.astype(BF16), v_ref[0, :, gl])
                o, denom = (term, part) if o is None else (o + term, denom + part)
            o = o / denom
            o_ref[0, rows, j * LANES:(j + 1) * LANES] = jnp.where(low_half, o[:sub], o[sub:]).astype(BF16)


def _attention(q, kv_sources, *, tq):
    b, n, _ = q.shape
    kv_specs, kv_args = [], []
    for k, v in kv_sources:
        for a in (k, v):
            kv_specs.append(pl.BlockSpec((1, a.shape[1], 2 * LANES), lambda bb, i: (bb, 0, 0)))
            kv_args.append(a)
    return pl.pallas_call(
        functools.partial(_attn_kernel, tq=tq),
        out_shape=jax.ShapeDtypeStruct((b, n, Q_WIDTH), BF16),
        grid=(b, n // tq),
        in_specs=[pl.BlockSpec((1, tq, Q_WIDTH), lambda bb, i: (bb, i, 0))] + kv_specs,
        out_specs=pl.BlockSpec((1, tq, Q_WIDTH), lambda bb, i: (bb, i, 0)),
        compiler_params=_params(),
        name="attention",
    )(q, *kv_args)


def _fourier_kernel(x_ref, wc_ref, cn_ref, nsn_ref, o_ref, t_scr, *, n, tr):
    t_scr[...] = _dot(x_ref[0], wc_ref[...].astype(BF16)).astype(BF16)
    scale = 1.0 / math.sqrt(n * FOURIER_GROUP)
    for r in range(n // tr):
        rows = slice(r * tr, (r + 1) * tr)
        acc = (_dot(cn_ref[rows, :].astype(BF16), t_scr[:, 0:FOURIER_WIDTH])
               + _dot(nsn_ref[rows, :].astype(BF16), t_scr[:, FOURIER_WIDTH:2 * FOURIER_WIDTH]))
        o_ref[0, rows, :] = (acc * scale).astype(BF16)


def _fourier(xf, wc, cn, nsn):
    b, n, _ = xf.shape
    tr = min(n, 512)
    return pl.pallas_call(
        functools.partial(_fourier_kernel, n=n, tr=tr),
        out_shape=jax.ShapeDtypeStruct((b, n, FOURIER_WIDTH), BF16),
        grid=(b, 1),
        in_specs=[pl.BlockSpec((1, n, FOURIER_WIDTH), lambda bb, i: (bb, 0, 0)),
                  _fixed_spec((FOURIER_WIDTH, 2 * FOURIER_WIDTH), (0, 0)),
                  _fixed_spec((n, n), (0, 0)),
                  _fixed_spec((n, n), (0, 0))],
        out_specs=pl.BlockSpec((1, n, FOURIER_WIDTH), lambda bb, i: (bb, 0, 0)),
        scratch_shapes=[pltpu.VMEM((n, 2 * FOURIER_WIDTH), BF16)],
        compiler_params=_params(),
        name="fourier",
    )(xf, wc, cn, nsn)


N_MIX_INPUTS = 14


def _mix_kernel(*refs, nb, n_cast):
    (x_ref, m_ref, norm_ref, oa_ref, sc_ref, xf_ref, yp_ref,
     wg01_ref, wg23_ref, wa_ref, wsc_ref, wf_ref, wp_ref, wo_ref) = refs[:N_MIX_INPUTS]
    cast_in = refs[N_MIX_INPUTS:N_MIX_INPUTS + n_cast]
    out_ref = refs[N_MIX_INPUTS + n_cast]
    cast_out = refs[N_MIX_INPUTS + n_cast + 1:N_MIX_INPUTS + 2 * n_cast + 1]
    h_scr, y_scr = refs[-2:]
    _run_casts(cast_in, cast_out)
    branches = ((oa_ref, wa_ref), (sc_ref, wsc_ref), (xf_ref, wf_ref), (yp_ref, wp_ref))
    gate_w = (wg01_ref, wg23_ref)
    norm = norm_ref[...]

    def mod(e):
        return m_ref.at[min(e, m_ref.shape[0] - 1)]

    def norm_stage(e):
        m = mod(e)
        h_scr[e] = _rms_mod(x_ref[e], norm, m[0:1, :], m[1:2, :]).astype(BF16)

    norm_stage(0)
    for e in range(nb):
        h = h_scr[e]
        for j in range(D_MODEL // FF_CHUNK):
            cols = slice(j * FF_CHUNK, (j + 1) * FF_CHUNK)
            y = None
            for bi, (a_ref, w_ref) in enumerate(branches):
                off = (bi % 2) * D_MODEL + j * FF_CHUNK
                gate = 0.5 + 0.5 * jnp.tanh(0.5 * _dot(h, gate_w[bi // 2][:, off:off + FF_CHUNK]))
                term = gate * _dot(a_ref[e], w_ref[:, cols])
                y = term if y is None else y + term
            y_scr[e, :, cols] = y.astype(BF16)
            if j == 0 and e + 1 < nb:
                norm_stage(e + 1)
        out_ref[e] = x_ref[e] + mod(e)[2:3, :] * _dot(y_scr[e], wo_ref[...])


def _mix(xs, prm, layer, mod_row, oa, sc, xf, yp, *, tq, casts=()):
    b, n, _ = xs.shape
    nb = MIX_ROWS // tq
    half_gates = N_BRANCHES * D_MODEL // 2
    grid = (n // tq, b // nb)
    cast_in, cast_out, cast_shapes = _cast_plumbing(casts, grid[0] * grid[1], lambda i, bb: i * grid[1] + bb)

    def tok(w):
        return pl.BlockSpec((nb, tq, w), lambda i, bb: (bb, i, 0))

    in_specs = [tok(D_MODEL), _mod_spec_multi(layer, mod_row, nb), _layer_spec((1, D_MODEL), layer),
                tok(Q_WIDTH), tok(SC_WIDTH), tok(FOURIER_WIDTH), tok(POOL_WIDTH),
                _pair_spec(prm["w_in"], (D_MODEL, half_gates), (0, OFF_G // half_gates)),
                _pair_spec(prm["w_in"], (D_MODEL, half_gates), (0, OFF_G // half_gates + 1)),
                _layer_spec((Q_WIDTH, D_MODEL), layer), _layer_spec((SC_WIDTH, D_MODEL), layer),
                _layer_spec((FOURIER_WIDTH, D_MODEL), layer), _layer_spec((POOL_WIDTH, D_MODEL), layer),
                _layer_spec((D_MODEL, D_MODEL), layer)]
    assert len(in_specs) == N_MIX_INPUTS
    return pl.pallas_call(
        functools.partial(_mix_kernel, nb=nb, n_cast=len(casts)),
        out_shape=[jax.ShapeDtypeStruct((b, n, D_MODEL), F32)] + cast_shapes,
        grid=grid,
        in_specs=in_specs + cast_in,
        out_specs=[tok(D_MODEL)] + cast_out,
        scratch_shapes=[pltpu.VMEM((nb, tq, D_MODEL), BF16), pltpu.VMEM((nb, tq, D_MODEL), BF16)],
        compiler_params=_params(),
        name="mix",
    )(xs, prm["mod"], prm["norm1"], oa, sc, xf, yp, prm["w_in"][0], prm["w_in"][0],
      prm["w_br_attn"], prm["w_br_sc"], prm["w_br_f"], prm["w_br_p"], prm["w_out"], *[a for a, _ in casts])


def _ffn_kernel(xp_ref, xc_ref, xn_ref, m_ref, norm_ref, wu_ref, cw_ref, wd_ref, fn_ref, out_ref,
                h_scr, hb_scr, u0_scr, u1_scr, *, tq, final, nb):
    norm = norm_ref[...]
    ext = tq + 2 * HALO

    def mod(e):
        return m_ref.at[min(e, m_ref.shape[0] - 1)]

    def cols(c, half):
        return slice(half * D_FF + c * FF_CHUNK, half * D_FF + (c + 1) * FF_CHUNK)

    def norm_stage(e):
        m = mod(e)
        shift, scale = m[3:4, :], m[4:5, :]
        _fill_halo_rows(h_scr.at[e], xp_ref.at[e:e + 1], xc_ref.at[e:e + 1], xn_ref.at[e:e + 1],
                        lambda xv: _rms_mod(xv, norm, shift, scale), tq)
        hb_scr[e] = h_scr[e].astype(BF16)

    def up(e, c, u_scr):
        hb = hb_scr[e]
        u_scr[0] = _dot(hb, wu_ref[:, cols(c, 0)])
        u_scr[1] = _dot(hb, wu_ref[:, cols(c, 1)])

    def conv3(u, w):
        full = pltpu.roll(u, 1, 0) * w[0:1, :] + u * w[1:2, :] + pltpu.roll(u, ext - 1, 0) * w[2:3, :]
        return full[HALO:HALO + tq]

    def down(c, u_scr):
        a = conv3(u_scr[0], cw_ref[:, cols(c, 0)])
        bv = conv3(u_scr[1], cw_ref[:, cols(c, 1)])
        half = 0.5 * a
        act = ((half + half * jnp.tanh(half)) * bv).astype(BF16)
        return _dot(act, wd_ref[c * FF_CHUNK:(c + 1) * FF_CHUNK, :])

    bufs = (u0_scr, u1_scr)
    units = [(e, c) for e in range(nb) for c in range(N_FF_CHUNKS)]
    norm_stage(0)
    up(0, 0, u0_scr)
    acc = None
    for k, (e, c) in enumerate(units):
        if k + 1 < len(units):
            e_next, c_next = units[k + 1]
            if c_next == 0:
                norm_stage(e_next)
            up(e_next, c_next, bufs[(k + 1) % 2])
        d = down(c, bufs[k % 2])
        acc = d if c == 0 else acc + d
        if c == N_FF_CHUNKS - 1:
            out = xc_ref[e] + mod(e)[5:6, :] * acc
            if final:
                ms = jnp.mean(out * out, axis=-1, keepdims=True)
                out = out * lax.rsqrt(ms + NORM_EPS) * fn_ref[...]
            out_ref[e] = out


def _ffn(xs, prm, layer, mod_row, *, tq, final):
    b, n, _ = xs.shape
    nb = FFN_BATCH
    return pl.pallas_call(
        functools.partial(_ffn_kernel, tq=tq, final=final, nb=nb),
        out_shape=jax.ShapeDtypeStruct((b, n, D_MODEL), F32),
        grid=(n // tq, b // nb),
        in_specs=_halo_specs(n, tq, D_MODEL, nb) + [
            _mod_spec_multi(layer, mod_row, nb),
            _layer_spec((1, D_MODEL), layer),
            _pair_spec(prm["w_up"], (D_MODEL, 2 * D_FF)),
            _layer_spec((3, 2 * D_FF), layer),
            _pair_spec(prm["w_down"], (D_FF, D_MODEL)),
            _fixed_spec((1, D_MODEL), (0, 0))],
        out_specs=pl.BlockSpec((nb, tq, D_MODEL), lambda i, bb: (bb, i, 0)),
        scratch_shapes=[pltpu.VMEM((nb, tq + 2 * HALO, D_MODEL), F32),
                        pltpu.VMEM((nb, tq + 2 * HALO, D_MODEL), BF16),
                        pltpu.VMEM((2, tq + 2 * HALO, FF_CHUNK), F32),
                        pltpu.VMEM((2, tq + 2 * HALO, FF_CHUNK), F32)],
        compiler_params=_params(),
        name="ffn",
    )(xs, xs, xs, prm["mod"], prm["norm2"], prm["w_up"][0], prm["w_conv_ffn"], prm["w_down"][0],
      prm["final_norm"])


def _rope_tables(n):
    d = np.arange(LANES)
    hd = d % HEAD_DIM
    axis = hd // (2 * AXIS_PAIRS)
    half = (hd % (2 * AXIS_PAIRS)) // AXIS_PAIRS
    pair = hd % AXIS_PAIRS
    t = np.arange(n)
    pos = np.where(axis[None, :] == 0, (t // GRID_W)[:, None], (t % GRID_W)[:, None]).astype(np.float64)
    inv = ROPE_THETA ** (-np.arange(AXIS_PAIRS, dtype=np.float64) / AXIS_PAIRS)
    ang = pos * inv[pair][None, :]
    sin = np.sin(ang)
    return (jnp.asarray(np.cos(ang), dtype=F32),
            jnp.asarray(np.where(half[None, :] == 0, -sin, sin), dtype=F32))


def _dft_tables(n):
    k = np.arange(n, dtype=np.int64)
    ang = ((k[:, None] * k[None, :]) % n).astype(np.float64) * (2.0 * math.pi / n)
    return jnp.asarray(np.cos(ang), dtype=F32), jnp.asarray(-np.sin(ang), dtype=F32)


def _channel_dft():
    c = np.arange(FOURIER_WIDTH, dtype=np.int64)
    same = (c[:, None] // FOURIER_GROUP) == (c[None, :] // FOURIER_GROUP)
    mc = ((c[:, None] % FOURIER_GROUP) * (c[None, :] % FOURIER_GROUP)) % FOURIER_GROUP
    ang = mc.astype(np.float64) * (2.0 * math.pi / FOURIER_GROUP)
    both = np.concatenate([np.where(same, np.cos(ang), 0.0), np.where(same, np.sin(ang), 0.0)], axis=1)
    return jnp.asarray(both, dtype=F32)


def _head_matrices():
    r = np.arange(QK_WIDTH) // HEAD_DIM
    m = (r[:, None] == np.arange(LANES)[None, :]).astype(np.float32)
    return jnp.asarray(m.astype(BF16)), jnp.asarray(m.T.copy().astype(BF16))


def _block_diag(mats):
    nl, g, a, b = mats.shape
    eye = jnp.eye(g, dtype=mats.dtype)
    return (mats[:, :, :, None, :] * eye[None, :, None, :, None]).reshape(nl, g * a, g * b)


def kernel(x, c, ctx, c_ctx, w_mod, b_mod, norm1, norm2, w_in, conv_sc, qk_gain, pool_mat, pool_scale,
           w_br_attn, w_br_sc, w_br_f, w_br_p, w_out, w_up, w_conv_ffn, w_down, final_norm):
    bsz, n_lat, _ = x.shape
    n_ctx = ctx.shape[1]
    nl = w_in.shape[0]
    tq_lat = min(n_lat, TOKEN_TILE)
    tq_ctx = min(n_ctx, TOKEN_TILE)
    tq_attn = min(n_lat, ATTN_TILE)
    tq_ffn_lat = min(n_lat, FFN_TILE)
    tq_ffn_ctx = min(n_ctx, FFN_TILE)

    mod_rows = -(-(bsz + 1) // BF16_ROWS) * BF16_ROWS
    ctx_row = bsz
    cvec = jnp.concatenate([c, c_ctx[None], jnp.zeros((mod_rows - bsz - 1, D_MODEL), F32)], axis=0)
    gain = jnp.concatenate([jnp.tile(qk_gain[:, 0], (1, N_HEADS)) * (HEAD_DIM ** -0.5),
                            jnp.tile(qk_gain[:, 1], (1, N_KV_HEADS))], axis=1)
    head_sum, head_bcast = _head_matrices()
    prm = {
        "mod": _modulation(cvec, w_mod, b_mod),
        "norm1": norm1.reshape(nl, 1, D_MODEL),
        "norm2": norm2.reshape(nl, 1, D_MODEL),
        "w_in": (w_in[:1].astype(BF16), 0),
        "head_sum": head_sum,
        "head_bcast": head_bcast,
        "gain": gain.reshape(nl, 1, QK_WIDTH),
        "conv_sc": conv_sc,
        "pool_mat": _block_diag(pool_mat).astype(BF16),
        "pool_scale": pool_scale.reshape(nl, 1, POOL_WIDTH),
        "w_br_attn": w_br_attn.astype(BF16),
        "w_br_sc": w_br_sc.astype(BF16),
        "w_br_f": w_br_f.astype(BF16),
        "w_br_p": w_br_p.astype(BF16),
        "w_out": w_out.astype(BF16),
        "w_conv_ffn": w_conv_ffn,
        "final_norm": final_norm.reshape(1, D_MODEL),
    }
    cos_t, sin_t = _rope_tables(n_lat)
    wc = _channel_dft()
    dft_lat = _dft_tables(n_lat)
    dft_ctx = _dft_tables(n_ctx)

    lat, cx = x, ctx
    for l in range(nl):
        last = l == nl - 1
        if last:
            kc, vc = _kv_proj(cx, prm, l, ctx_row, tq=tq_ctx)
        else:
            qc, kc, vc, scc, xfc, ypc = _in_proj(cx, prm, l, ctx_row, cos_t, sin_t, tq=tq_ctx, rope=False)
        ql, kl, vl, scl, xfl, ypl = _in_proj(lat, prm, l, None, cos_t, sin_t, tq=tq_lat, rope=True)
        oa = _attention(ql, [(kl, vl), (kc, vc)], tq=tq_attn)
        xfour = _fourier(xfl, wc, *dft_lat)
        lat, w_up_bf, w_down_bf, *next_w_in = _mix(
            lat, prm, l, None, oa, scl, xfour, ypl, tq=min(n_lat, MIX_ROWS),
            casts=[(w_up, l), (w_down, l)] + ([] if last else [(w_in, l + 1)]))
        prm["w_up"], prm["w_down"] = (w_up_bf, 0), (w_down_bf, 0)
        lat = _ffn(lat, prm, l, None, tq=tq_ffn_lat, final=last)

        if not last:
            oac = _attention(qc, [(kc, vc)], tq=tq_ctx)
            xfourc = _fourier(xfc, wc, *dft_ctx)
            cx, = _mix(cx, prm, l, ctx_row, oac, scc, xfourc, ypc, tq=min(n_ctx, MIX_ROWS))
            cx = _ffn(cx, prm, l, ctx_row, tq=tq_ffn_ctx, final=False)
            prm["w_in"] = (next_w_in[0], 0)
    return lat
```
